```python
import math
import jax, jax.numpy as jnp
from jax import lax
import numpy as np

D_MODEL = 1024
BATCH = 4
SEQ = 4096
DEPTH = 2

HEAD_DIM = 64
BRANCH_WIDTH = D_MODEL // 2
N_BRANCHES = 4
Q_BLOCK = 128
FOX_HEADS = BRANCH_WIDTH // HEAD_DIM
LRU_BLOCKS = 8
LRU_BLOCK_DIM = BRANCH_WIDTH // LRU_BLOCKS
CONV_WIDTH = 4
LRU_C = 8.0
NSA_HEADS = BRANCH_WIDTH // HEAD_DIM
NSA_KV_HEADS = 2
NSA_GROUP = NSA_HEADS // NSA_KV_HEADS
NSA_KV_WIDTH = NSA_KV_HEADS * HEAD_DIM
CMP_BLOCK = 32
CMP_STRIDE = 16
CMP_HIDDEN = 2 * HEAD_DIM
SLC_BLOCK = 64
SLC_TOPK = 16
SLC_LOCAL = 2
SLC_FORCE_SCORE = 1e6
WINDOW = 512
SG_CHUNK = 128
SG_GROUPS = 8
SG_GROUP_DIM = BRANCH_WIDTH // SG_GROUPS

NORM_EPS = 1e-6
NEG_INF = -1e30

kernel_name = "hybrid_fox_rglru_nsa_sgmlp_block"


def _in_split_sizes():
    W = BRANCH_WIDTH
    return (W, W, W, FOX_HEADS, W,
            W, W,
            W, NSA_KV_WIDTH, NSA_KV_WIDTH, NSA_KV_WIDTH,
            NSA_KV_WIDTH, NSA_KV_WIDTH, NSA_KV_WIDTH,
            3 * NSA_HEADS, W,
            W, W, W,
            N_BRANCHES * D_MODEL)


def rms_norm(x, g):
    x32 = x.astype(jnp.float32)
    y = x32 * lax.rsqrt(jnp.mean(x32 * x32, axis=-1, keepdims=True) + NORM_EPS)
    return (y * g.astype(jnp.float32)).astype(x.dtype)


def layer_norm(x, g):
    x32 = x.astype(jnp.float32)
    mu = jnp.mean(x32, axis=-1, keepdims=True)
    xc = x32 - mu
    y = xc * lax.rsqrt(jnp.mean(xc * xc, axis=-1, keepdims=True) + NORM_EPS)
    return (y * g.astype(jnp.float32)).astype(x.dtype)


def masked_softmax(s, mask):
    p = jax.nn.softmax(jnp.where(mask, s, NEG_INF), axis=-1)
    return jnp.where(mask, p, 0.0)


def fox_attention(q, k, v, f_logit):
    B, S, H, dh = q.shape
    n_blk = S // Q_BLOCK
    scale = 1.0 / math.sqrt(dh)
    c = jnp.cumsum(jax.nn.log_sigmoid(f_logit.astype(jnp.float32)), axis=1)
    c_k = c.transpose(0, 2, 1)[:, :, None, :]
    qb = q.reshape(B, n_blk, Q_BLOCK, H, dh).transpose(1, 0, 2, 3, 4)
    cb = c.reshape(B, n_blk, Q_BLOCK, H).transpose(1, 0, 2, 3)
    pos_k = jnp.arange(S)

    def block(args):
        q_i, c_i, i = args
        t = i * Q_BLOCK + jnp.arange(Q_BLOCK)
        s = jnp.einsum('bqhd,bkhd->bhqk', q_i, k).astype(jnp.float32) * scale
        s = s + c_i.transpose(0, 2, 1)[..., None] - c_k
        p = masked_softmax(s, pos_k[None, :] <= t[:, None])
        return jnp.einsum('bhqk,bkhd->bqhd', p.astype(v.dtype), v)

    out = lax.map(block, (qb, cb, jnp.arange(n_blk)))
    return out.transpose(1, 0, 2, 3, 4).reshape(B, S, H, dh)


def rg_lru(xb, conv_w, conv_b, w_a, b_a, w_x, b_x, lam):
    B, S, W = xb.shape
    xc = lax.conv_general_dilated(
        xb, conv_w.reshape(CONV_WIDTH, 1, W).astype(xb.dtype), window_strides=(1,),
        padding=[(CONV_WIDTH - 1, 0)], dimension_numbers=('NWC', 'WIO', 'NWC'),
        feature_group_count=W) + conv_b
    xh = xc.reshape(B, S, LRU_BLOCKS, LRU_BLOCK_DIM)
    r = jax.nn.sigmoid((jnp.einsum('bsnd,nde->bsne', xh, w_a).reshape(B, S, W) + b_a).astype(jnp.float32))
    i_g = jax.nn.sigmoid((jnp.einsum('bsnd,nde->bsne', xh, w_x).reshape(B, S, W) + b_x).astype(jnp.float32))
    log_a = -LRU_C * r * jax.nn.softplus(-lam.astype(jnp.float32))
    a = jnp.exp(log_a)
    b = jnp.sqrt(-jnp.expm1(2.0 * log_a)) * (i_g * xc.astype(jnp.float32))

    def combine(left, right):
        a_l, b_l = left
        a_r, b_r = right
        return a_l * a_r, a_r * b_l + b_r

    _, h = lax.associative_scan(combine, (a, b), axis=1)
    return h.astype(xb.dtype)


def _cmp_slc_overlap(n_cmp, n_slc):
    c0 = np.arange(n_cmp) * CMP_STRIDE
    s0 = np.arange(n_slc) * SLC_BLOCK
    ov = np.minimum(c0[:, None] + CMP_BLOCK, s0[None, :] + SLC_BLOCK) - np.maximum(c0[:, None], s0[None, :])
    return (np.clip(ov, 0, None) / CMP_BLOCK).astype(np.float32)


def nsa_attention(q, k_cmp, v_cmp, k_slc, v_slc, k_win, v_win, gates,
                  kn_g, cmp_pos, wk1, wk2, wv1, wv2):
    B, S, H, dh = q.shape
    G, R = NSA_KV_HEADS, NSA_GROUP
    scale = 1.0 / math.sqrt(dh)
    n_cmp = (S - CMP_BLOCK) // CMP_STRIDE + 1
    blk_idx = np.arange(n_cmp)[:, None] * CMP_STRIDE + np.arange(CMP_BLOCK)[None, :]
    cmp_end = jnp.asarray(blk_idx[:, -1])

    def compress(t, w1, w2):
        tb = t[:, blk_idx] + cmp_pos[None, None, :, None, :]
        tb = tb.transpose(0, 1, 3, 2, 4).reshape(B, n_cmp, G, CMP_BLOCK * dh)
        return jax.nn.silu(tb @ w1) @ w2

    k_c = rms_norm(compress(k_cmp, wk1, wk2), kn_g)
    v_c = compress(v_cmp, wv1, wv2)
    n_slc = S // SLC_BLOCK
    top_k = min(SLC_TOPK, n_slc)
    overlap = jnp.asarray(_cmp_slc_overlap(n_cmp, n_slc))
    ks_blk = rms_norm(k_slc, kn_g).reshape(B, n_slc, SLC_BLOCK, G, dh).transpose(0, 3, 1, 2, 4)
    vs_blk = v_slc.reshape(B, n_slc, SLC_BLOCK, G, dh).transpose(0, 3, 1, 2, 4)
    kw_pad = jnp.pad(rms_norm(k_win, kn_g), ((0, 0), (WINDOW, 0), (0, 0), (0, 0)))
    vw_pad = jnp.pad(v_win, ((0, 0), (WINDOW, 0), (0, 0), (0, 0)))
    span = WINDOW + Q_BLOCK

    n_blk = S // Q_BLOCK
    qg = q.reshape(B, n_blk, Q_BLOCK, G, R, dh).transpose(1, 0, 2, 3, 4, 5)
    gb = gates.reshape(B, n_blk, Q_BLOCK, 3, H).transpose(1, 0, 2, 3, 4)
    b_ix = jnp.arange(B)[:, None, None, None]
    g_ix = jnp.arange(G)[None, :, None, None]
    j = jnp.arange(n_slc)

    def block(args):
        q_i, g_i, i = args
        t = i * Q_BLOCK + jnp.arange(Q_BLOCK)
        s_c = jnp.einsum('bqgrd,bcgd->bgrqc', q_i, k_c).astype(jnp.float32) * scale
        p_c = masked_softmax(s_c, cmp_end[None, :] <= t[:, None])
        o_c = jnp.einsum('bgrqc,bcgd->bqgrd', p_c.astype(v_c.dtype), v_c)
        imp = jnp.einsum('bgrqc,cj->bgqj', p_c, overlap)
        jt = t // SLC_BLOCK
        valid = j[None, :] <= jt[:, None]
        forced = (j[None, :] == 0) | (valid & (j[None, :] > jt[:, None] - SLC_LOCAL))
        score = jnp.where(forced, SLC_FORCE_SCORE, jnp.where(valid, imp, -1.0))
        top_val, top_idx = lax.top_k(score, top_k)
        k_sel = ks_blk[b_ix, g_ix, top_idx]
        v_sel = vs_blk[b_ix, g_ix, top_idx]
        key_pos = top_idx[..., None] * SLC_BLOCK + jnp.arange(SLC_BLOCK)
        m_s = (top_val >= 0.0)[..., None] & (key_pos <= t[None, None, :, None, None])
        s_s = jnp.einsum('bqgrd,bgqkld->bgrqkl', q_i, k_sel).astype(jnp.float32) * scale
        n_sel = top_k * SLC_BLOCK
        p_s = masked_softmax(s_s.reshape(B, G, R, Q_BLOCK, n_sel), m_s.reshape(B, G, 1, Q_BLOCK, n_sel))
        o_s = jnp.einsum('bgrqn,bgqnd->bqgrd', p_s.astype(v_sel.dtype),
                         v_sel.reshape(B, G, Q_BLOCK, n_sel, dh))
        k_wi = lax.dynamic_slice_in_dim(kw_pad, i * Q_BLOCK, span, axis=1)
        v_wi = lax.dynamic_slice_in_dim(vw_pad, i * Q_BLOCK, span, axis=1)
        pos_w = i * Q_BLOCK - WINDOW + jnp.arange(span)
        m_w = (pos_w[None, :] >= 0) & (pos_w[None, :] <= t[:, None]) & (pos_w[None, :] > t[:, None] - WINDOW)
        s_w = jnp.einsum('bqgrd,bkgd->bgrqk', q_i, k_wi).astype(jnp.float32) * scale
        p_w = masked_softmax(s_w, m_w)
        o_w = jnp.einsum('bgrqk,bkgd->bqgrd', p_w.astype(v_wi.dtype), v_wi)
        g = g_i.reshape(B, Q_BLOCK, 3, G, R)[..., None]
        return g[:, :, 0] * o_c + g[:, :, 1] * o_s + g[:, :, 2] * o_w

    out = lax.map(block, (qg, gb, jnp.arange(n_blk)))
    return out.transpose(1, 0, 2, 3, 4, 5).reshape(B, S, H * dh)


def spatial_gating(u, v, ln_g, w_s, b_s):
    B, S, W = u.shape
    n_chunk = S // SG_CHUNK
    u = jax.nn.gelu(u, approximate=False)
    v = layer_norm(jax.nn.gelu(v, approximate=False), ln_g)
    vc = v.reshape(B, n_chunk, SG_CHUNK, SG_GROUPS, SG_GROUP_DIM)
    causal = jnp.tril(jnp.ones((SG_CHUNK, SG_CHUNK), dtype=w_s.dtype))
    mixed = jnp.einsum('gts,bcsgd->bctgd', w_s * causal, vc) + b_s.T[None, None, :, :, None]
    return u * mixed.reshape(B, S, W)


def hybrid_layer(x, norm_g, w_in, b_forget, qn_a, kn_a, conv_w, conv_b, w_rg_a, b_rg_a,
                 w_rg_x, b_rg_x, lru_lambda, qn_c, kn_c, cmp_pos, cmp_k_w1, cmp_k_w2,
                 cmp_v_w1, cmp_v_w2, ln_v_g, w_spatial, b_spatial, w_branch, w_out):
    B, S, D = x.shape
    W = BRANCH_WIDTH
    xn = rms_norm(x, norm_g)
    z = xn @ w_in
    split_at = np.cumsum(_in_split_sizes())[:-1].tolist()
    (qa, ka, va, fa, ga, xb, gb, qc, kcc, vcc, ksc, vsc, kwc, vwc, gate_c, gc,
     ud, vd, gd, mg) = jnp.split(z, split_at, axis=-1)
    heads = lambda t, n: t.reshape(B, S, n, HEAD_DIM)
    y_a = fox_attention(rms_norm(heads(qa, FOX_HEADS), qn_a), rms_norm(heads(ka, FOX_HEADS), kn_a),
                        heads(va, FOX_HEADS), fa + b_forget).reshape(B, S, W)
    y_b = rg_lru(xb, conv_w, conv_b, w_rg_a, b_rg_a, w_rg_x, b_rg_x, lru_lambda)
    y_c = nsa_attention(rms_norm(heads(qc, NSA_HEADS), qn_c),
                        heads(kcc, NSA_KV_HEADS), heads(vcc, NSA_KV_HEADS),
                        heads(ksc, NSA_KV_HEADS), heads(vsc, NSA_KV_HEADS),
                        heads(kwc, NSA_KV_HEADS), heads(vwc, NSA_KV_HEADS),
                        jax.nn.sigmoid(gate_c).reshape(B, S, 3, NSA_HEADS),
                        kn_c, cmp_pos, cmp_k_w1, cmp_k_w2, cmp_v_w1, cmp_v_w2)
    y_d = spatial_gating(ud, vd, ln_v_g, w_spatial, b_spatial)
    ys = jnp.stack([y_a * jax.nn.silu(ga), y_b * jax.nn.silu(gb),
                    y_c * jax.nn.silu(gc), y_d * jax.nn.silu(gd)], axis=2)
    proj = jnp.einsum('bsnw,nwd->bsnd', ys, w_branch)
    merged = jnp.sum(jax.nn.sigmoid(mg.reshape(B, S, N_BRANCHES, D)) * proj, axis=2)
    return x + merged @ w_out


def setup_inputs(seed: int = 0) -> dict:
    key = jax.random.key(seed)
    ks = iter(jax.random.split(key, 32))
    L, D, W, dh = DEPTH, D_MODEL, BRANCH_WIDTH, HEAD_DIM

    def nrm(shape, scale):
        return jax.random.normal(next(ks), shape, jnp.float32) * scale

    w_in_width = sum(_in_split_sizes())
    x = nrm((BATCH, SEQ, D), 1.0)
    norm_g = 1.0 + nrm((L, D), 0.05)
    w_in = nrm((L, D, w_in_width), D ** -0.5)
    b_forget = 3.0 + nrm((L, FOX_HEADS), 0.5)
    qn_a = 1.0 + nrm((L, dh), 0.05)
    kn_a = 1.0 + nrm((L, dh), 0.05)
    conv_w = nrm((L, CONV_WIDTH, W), CONV_WIDTH ** -0.5)
    conv_b = nrm((L, W), 0.01)
    w_rg_a = nrm((L, LRU_BLOCKS, LRU_BLOCK_DIM, LRU_BLOCK_DIM), LRU_BLOCK_DIM ** -0.5)
    b_rg_a = nrm((L, W), 0.01)
    w_rg_x = nrm((L, LRU_BLOCKS, LRU_BLOCK_DIM, LRU_BLOCK_DIM), LRU_BLOCK_DIM ** -0.5)
    b_rg_x = nrm((L, W), 0.01)
    a_c = jax.random.uniform(next(ks), (L, W), jnp.float32, 0.9, 0.999)
    a0 = a_c ** (1.0 / LRU_C)
    lru_lambda = jnp.log(a0) - jnp.log1p(-a0)
    qn_c = 1.0 + nrm((L, dh), 0.05)
    kn_c = 1.0 + nrm((L, dh), 0.05)
    cmp_pos = nrm((L, CMP_BLOCK, dh), 0.02)
    cmp_k_w1 = nrm((L, CMP_BLOCK * dh, CMP_HIDDEN), (CMP_BLOCK * dh) ** -0.5)
    cmp_k_w2 = nrm((L, CMP_HIDDEN, dh), CMP_HIDDEN ** -0.5)
    cmp_v_w1 = nrm((L, CMP_BLOCK * dh, CMP_HIDDEN), (CMP_BLOCK * dh) ** -0.5)
    cmp_v_w2 = nrm((L, CMP_HIDDEN, dh), CMP_HIDDEN ** -0.5)
    ln_v_g = 1.0 + nrm((L, W), 0.05)
    w_spatial = nrm((L, SG_GROUPS, SG_CHUNK, SG_CHUNK), SG_CHUNK ** -0.5)
    b_spatial = 1.0 + nrm((L, SG_GROUPS, SG_CHUNK), 0.1)
    w_branch = nrm((L, N_BRANCHES, W, D), W ** -0.5)
    w_out = nrm((L, D, D), D ** -0.5)
    return {"x": x, "norm_g": norm_g, "w_in": w_in, "b_forget": b_forget, "qn_a": qn_a,
            "kn_a": kn_a, "conv_w": conv_w, "conv_b": conv_b, "w_rg_a": w_rg_a,
            "b_rg_a": b_rg_a, "w_rg_x": w_rg_x, "b_rg_x": b_rg_x, "lru_lambda": lru_lambda,
            "qn_c": qn_c, "kn_c": kn_c, "cmp_pos": cmp_pos, "cmp_k_w1": cmp_k_w1,
            "cmp_k_w2": cmp_k_w2, "cmp_v_w1": cmp_v_w1, "cmp_v_w2": cmp_v_w2,
            "ln_v_g": ln_v_g, "w_spatial": w_spatial, "b_spatial": b_spatial,
            "w_branch": w_branch, "w_out": w_out}


def reference(x, norm_g, w_in, b_forget, qn_a, kn_a, conv_w, conv_b, w_rg_a, b_rg_a,
              w_rg_x, b_rg_x, lru_lambda, qn_c, kn_c, cmp_pos, cmp_k_w1, cmp_k_w2,
              cmp_v_w1, cmp_v_w2, ln_v_g, w_spatial, b_spatial, w_branch, w_out):
    for l in range(DEPTH):
        x = hybrid_layer(x, norm_g[l], w_in[l], b_forget[l], qn_a[l], kn_a[l], conv_w[l],
                         conv_b[l], w_rg_a[l], b_rg_a[l], w_rg_x[l], b_rg_x[l], lru_lambda[l],
                         qn_c[l], kn_c[l], cmp_pos[l], cmp_k_w1[l], cmp_k_w2[l], cmp_v_w1[l],
                         cmp_v_w2[l], ln_v_g[l], w_spatial[l], b_spatial[l], w_branch[l], w_out[l])
    return x
```

```python
import functools
import math

import numpy as np
import jax
import jax.numpy as jnp
from jax import lax
from jax.experimental import pallas as pl
from jax.experimental.pallas import tpu as pltpu

F32 = jnp.float32
BF16 = jnp.bfloat16

D_MODEL = 1024
HEAD_DIM = 64
BRANCH_WIDTH = 512
N_BRANCHES = 4
FOX_HEADS = 8
LRU_BLOCKS = 8
CONV_WIDTH = 4
LRU_C = 8.0
NSA_HEADS = 8
NSA_KV_HEADS = 2
NSA_GROUP = 4
CMP_BLOCK = 32
CMP_STRIDE = 16
CMP_HIDDEN = 128
SLC_BLOCK = 64
SLC_TOPK = 16
SLC_LOCAL = 2
SLC_FORCE_SCORE = 1e6
WINDOW = 512
Q_BLOCK = 128
SG_CHUNK = 128
SG_GROUPS = 8
NORM_EPS = 1e-6
NEG_INF = -1e30
SEL_NEG = -32768.0

LANES = 128
V7X_VMEM_LIMIT = 56 * 1024 * 1024

ZT = 512
OFF_MG = 0
OFF_QA = 8 * ZT
OFF_KA = 9 * ZT
OFF_VA = 10 * ZT
OFF_GA = 11 * ZT
OFF_XB = 12 * ZT
OFF_GB = 13 * ZT
OFF_QC = 14 * ZT
OFF_GC = 15 * ZT
OFF_UD = 16 * ZT
OFF_VD = 17 * ZT
OFF_GD = 18 * ZT
OFF_KSD = 19 * ZT
OFF_KWD = 19 * ZT + 256
OFF_VSD = 20 * ZT
OFF_VWD = 20 * ZT + 256
OFF_CMP = 21 * ZT
ZW = 22 * ZT
NORM_TILES = (8, 9, 14, 19)
SMALL_W = 384


def _nt(a, b):
    return lax.dot_general(a, b, (((1,), (1,)), ((), ())), preferred_element_type=F32)


def _mm(a, b):
    return jnp.dot(a, b, preferred_element_type=F32)


def _sigmoid(x):
    return 1.0 / (1.0 + jnp.exp(-x))


def _silu(x):
    return x * _sigmoid(x)


def _split_sizes():
    W, kv = BRANCH_WIDTH, NSA_KV_HEADS * HEAD_DIM
    return (W, W, W, FOX_HEADS, W, W, W, W, kv, kv, kv, kv, kv, kv, 3 * NSA_HEADS, W,
            W, W, W, N_BRANCHES * D_MODEL)


def _inproj_kernel(x_ref, g_ref, w_ref, gain_ref, bd_ref, ws_ref, z_ref, small_ref, xn_ref):
    j = pl.program_id(1)

    @pl.when(j == 0)
    def _():
        x = x_ref[...]
        ms = jnp.mean(x * x, axis=-1, keepdims=True)
        xn = ((x * lax.rsqrt(ms + NORM_EPS)) * g_ref[...]).astype(BF16)
        xn_ref[...] = xn
        small_ref[...] = _mm(xn, ws_ref[...])

    acc = _mm(xn_ref[...], w_ref[...])
    is_norm = functools.reduce(jnp.logical_or, [j == t for t in NORM_TILES])

    @pl.when(is_norm)
    def _():
        ms = _mm((acc * acc).astype(BF16), bd_ref[...])
        z_ref[...] = (acc * lax.rsqrt(ms + NORM_EPS) * gain_ref[...]).astype(BF16)

    @pl.when(jnp.logical_not(is_norm))
    def _():
        z_ref[...] = acc.astype(BF16)


def _inproj(x2, norm_g, w_z, gain_row, bd, w_small, tm):
    n = x2.shape[0]
    grid = (n // tm, ZW // ZT)
    return pl.pallas_call(
        _inproj_kernel,
        grid=grid,
        in_specs=[
            pl.BlockSpec((tm, D_MODEL), lambda i, j: (i, 0)),
            pl.BlockSpec((1, D_MODEL), lambda i, j: (0, 0)),
            pl.BlockSpec((D_MODEL, ZT), lambda i, j: (0, j)),
            pl.BlockSpec((1, ZT), lambda i, j: (0, j)),
            pl.BlockSpec((ZT, ZT), lambda i, j: (0, 0)),
            pl.BlockSpec((D_MODEL, SMALL_W), lambda i, j: (0, 0)),
        ],
        out_specs=[
            pl.BlockSpec((tm, ZT), lambda i, j: (i, j)),
            pl.BlockSpec((tm, SMALL_W), lambda i, j: (i, 0)),
        ],
        out_shape=[
            jax.ShapeDtypeStruct((n, ZW), BF16),
            jax.ShapeDtypeStruct((n, SMALL_W), F32),
        ],
        scratch_shapes=[pltpu.VMEM((tm, D_MODEL), BF16)],
        compiler_params=pltpu.CompilerParams(
            dimension_semantics=("parallel", "arbitrary"), vmem_limit_bytes=V7X_VMEM_LIMIT),
        name="inproj",
    )(x2, norm_g, w_z, gain_row, bd, w_small)


def _cprep_kernel(fa_ref, bf_ref, pq_ref, pk_ref, oq_ref, ok_ref, caq_ref, cak_ref, carry_ref,
                  *, tt):
    @pl.when(pl.program_id(1) == 0)
    def _():
        carry_ref[...] = jnp.zeros_like(carry_ref)

    x = fa_ref[...] + bf_ref[...]
    c = jnp.minimum(x, 0.0) - jnp.log(1.0 + jnp.exp(-jnp.abs(x)))
    row = lax.broadcasted_iota(jnp.int32, c.shape, 0)
    d = 1
    while d < tt:
        c = c + jnp.where(row >= d, pltpu.roll(c, d, axis=0), 0.0)
        d *= 2
    c = c + carry_ref[...]
    carry_ref[...] = c[tt - 1:tt, :]
    hi = c.astype(BF16)
    r1 = c - hi.astype(F32)
    mid = r1.astype(BF16)
    lo = (r1 - mid.astype(F32)).astype(BF16)
    parts = jnp.concatenate([hi, mid, lo], axis=1)
    caq_ref[...] = (_mm(parts, pq_ref[...]) + oq_ref[...]).astype(BF16)
    cak_ref[...] = (_mm(parts, pk_ref[...]) + ok_ref[...]).astype(BF16)


def _aug_lane(h):
    return (h // 2) * LANES + (h % 2) * 8


@functools.lru_cache(maxsize=None)
def _cprep_consts():
    pq = np.zeros((3 * LANES, 4 * LANES), np.float32)
    pk = np.zeros((3 * LANES, 4 * LANES), np.float32)
    oq = np.zeros((1, 4 * LANES), np.float32)
    ok = np.zeros((1, 4 * LANES), np.float32)
    for h in range(FOX_HEADS):
        base = _aug_lane(h)
        for part in range(3):
            pq[part * LANES + h, base + part] = 1.0
            pk[part * LANES + h, base + 3 + part] = -1.0
            oq[0, base + 3 + part] = 1.0
            ok[0, base + part] = 1.0
    return pq, pk, oq, ok


def _cprep(small, bf_row, b, s, tt):
    n = b * s
    nt = s // tt
    pq, pk, oq, ok = _cprep_consts()
    full = lambda shape: pl.BlockSpec(shape, lambda bi, ti: (0, 0))
    return pl.pallas_call(
        functools.partial(_cprep_kernel, tt=tt),
        grid=(b, nt),
        in_specs=[
            pl.BlockSpec((tt, LANES), lambda bi, ti: (bi * nt + ti, 0)),
            full((1, LANES)),
            full((3 * LANES, 4 * LANES)),
            full((3 * LANES, 4 * LANES)),
            full((1, 4 * LANES)),
            full((1, 4 * LANES)),
        ],
        out_specs=[
            pl.BlockSpec((tt, 4 * LANES), lambda bi, ti: (bi * nt + ti, 0)),
            pl.BlockSpec((tt, 4 * LANES), lambda bi, ti: (bi * nt + ti, 0)),
        ],
        out_shape=[jax.ShapeDtypeStruct((n, 4 * LANES), BF16)] * 2,
        scratch_shapes=[pltpu.VMEM((1, LANES), F32)],
        compiler_params=pltpu.CompilerParams(dimension_semantics=("parallel", "arbitrary")),
        name="cprep",
    )(small, bf_row, jnp.asarray(pq, BF16), jnp.asarray(pk, BF16), jnp.asarray(oq), jnp.asarray(ok))


def _fox_kernel(q_ref, caq_ref, k_ref, cak_ref, v_ref, ga_ref, o_ref,
                lhs_ref, m_ref, l_ref, acc_ref, *, t):
    qi = pl.program_id(2)
    ki = pl.program_id(3)

    @pl.when(ki == 0)
    def _():
        lhs = jnp.concatenate([q_ref[...], caq_ref[...]], axis=1).astype(F32)
        lane = lax.broadcasted_iota(jnp.int32, lhs.shape, 1)
        head_of_lane = jnp.where(lane < LANES, lane // HEAD_DIM, (lane - LANES) // 8)
        for hh in range(2):
            lhs_ref[hh] = jnp.where(head_of_lane == hh, lhs, 0.0).astype(BF16)
        m_ref[...] = jnp.full_like(m_ref, NEG_INF)
        l_ref[...] = jnp.zeros_like(l_ref)
        acc_ref[...] = jnp.zeros_like(acc_ref)

    def step(diag):
        rhs = jnp.concatenate([k_ref[...], cak_ref[...]], axis=1)
        v = v_ref[...]
        for hh in range(2):
            s = _nt(lhs_ref[hh], rhs)
            if diag:
                row = lax.broadcasted_iota(jnp.int32, s.shape, 0)
                col = lax.broadcasted_iota(jnp.int32, s.shape, 1)
                s = jnp.where(col <= row, s, NEG_INF)
            m_prev = m_ref[hh]
            m_new = jnp.maximum(m_prev, jnp.max(s, axis=1, keepdims=True))
            alpha = jnp.exp(m_prev - m_new)
            p = jnp.exp(s - m_new)
            l_ref[hh] = alpha * l_ref[hh] + jnp.sum(p, axis=1, keepdims=True)
            acc_ref[hh] = alpha * acc_ref[hh] + _mm(p.astype(BF16), v)
            m_ref[hh] = m_new

    @pl.when(ki < qi)
    def _():
        step(False)

    @pl.when(ki == qi)
    def _():
        step(True)
        lane = lax.broadcasted_iota(jnp.int32, (t, LANES), 1)
        o = jnp.where(lane < HEAD_DIM, acc_ref[0] / l_ref[0], acc_ref[1] / l_ref[1])
        o_ref[...] = (o * _silu(ga_ref[...].astype(F32))).astype(BF16)


def _fox(z, caq, cak, b, s, t):
    n = b * s
    nq = s // t
    qrow = lambda bi, hp, qi, ki: bi * nq + qi
    krow = lambda bi, hp, qi, ki: bi * nq + jnp.minimum(ki, qi)
    zq = lambda off: pl.BlockSpec((t, LANES), lambda bi, hp, qi, ki: (qrow(bi, hp, qi, ki), off // LANES + hp))
    zk = lambda off: pl.BlockSpec((t, LANES), lambda bi, hp, qi, ki: (krow(bi, hp, qi, ki), off // LANES + hp))
    return pl.pallas_call(
        functools.partial(_fox_kernel, t=t),
        grid=(b, FOX_HEADS // 2, nq, nq),
        in_specs=[zq(OFF_QA), zq(0), zk(OFF_KA), zk(0), zk(OFF_VA), zq(OFF_GA)],
        out_specs=pl.BlockSpec((t, LANES), lambda bi, hp, qi, ki: (qrow(bi, hp, qi, ki), hp)),
        out_shape=jax.ShapeDtypeStruct((n, BRANCH_WIDTH), BF16),
        scratch_shapes=[
            pltpu.VMEM((2, t, 2 * LANES), BF16),
            pltpu.VMEM((2, t, 1), F32),
            pltpu.VMEM((2, t, 1), F32),
            pltpu.VMEM((2, t, LANES), F32),
        ],
        compiler_params=pltpu.CompilerParams(
            dimension_semantics=("parallel", "parallel", "parallel", "arbitrary"),
            vmem_limit_bytes=V7X_VMEM_LIMIT),
        name="fox",
    )(z, caq, z, cak, z, z)


def _lru_kernel(xb_ref, gb_ref, cw_ref, cb_ref, wa_ref, ba_ref, wx_ref, bx_ref, lam_ref, o_ref,
                xpad_ref, h_ref, *, tt):
    @pl.when(pl.program_id(1) == 0)
    def _():
        xpad_ref[0:8, :] = jnp.zeros((8, BRANCH_WIDTH), F32)
        h_ref[...] = jnp.zeros_like(h_ref)

    x = xb_ref[...].astype(F32)
    xpad_ref[8:tt + 8, :] = x
    xc = cb_ref[...] + cw_ref[3:4, :] * x
    for j in range(CONV_WIDTH - 1):
        sh = CONV_WIDTH - 1 - j
        xc = xc + cw_ref[j:j + 1, :] * xpad_ref[8 - sh:8 - sh + tt, :]
    xpad_ref[0:8, :] = x[tt - 8:tt, :]

    xcb = xc.astype(BF16)
    r = _sigmoid(_mm(xcb, wa_ref[...]) + ba_ref[...])
    ig = _sigmoid(_mm(xcb, wx_ref[...]) + bx_ref[...])
    nl = -lam_ref[...]
    softplus = jnp.maximum(nl, 0.0) + jnp.log(1.0 + jnp.exp(-jnp.abs(nl)))
    log_a = (-LRU_C) * r * softplus
    a = jnp.exp(log_a)
    bb = jnp.sqrt(1.0 - jnp.exp(2.0 * log_a)) * (ig * xc)

    row = lax.broadcasted_iota(jnp.int32, a.shape, 0)
    d = 1
    while d < tt:
        keep = row >= d
        bb = bb + a * jnp.where(keep, pltpu.roll(bb, d, axis=0), 0.0)
        a = a * jnp.where(keep, pltpu.roll(a, d, axis=0), 1.0)
        d *= 2
    h = bb + a * h_ref[...]
    h_ref[...] = h[tt - 1:tt, :]
    o_ref[...] = (h * _silu(gb_ref[...].astype(F32))).astype(BF16)


def _lru(z, conv_w, conv_b, wa_bd, b_a, wx_bd, b_x, lam, b, s, tt):
    n = b * s
    nt = s // tt
    W = BRANCH_WIDTH
    full = lambda shape: pl.BlockSpec(shape, lambda bi, ti: (0, 0))
    zb = lambda off: pl.BlockSpec((tt, W), lambda bi, ti: (bi * nt + ti, off // W))
    return pl.pallas_call(
        functools.partial(_lru_kernel, tt=tt),
        grid=(b, nt),
        in_specs=[zb(OFF_XB), zb(OFF_GB), full((CONV_WIDTH, W)), full((1, W)), full((W, W)),
                  full((1, W)), full((W, W)), full((1, W)), full((1, W))],
        out_specs=pl.BlockSpec((tt, W), lambda bi, ti: (bi * nt + ti, 0)),
        out_shape=jax.ShapeDtypeStruct((n, W), BF16),
        scratch_shapes=[pltpu.VMEM((tt + 8, W), F32), pltpu.VMEM((1, W), F32)],
        compiler_params=pltpu.CompilerParams(dimension_semantics=("parallel", "arbitrary")),
        name="lru",
    )(z, z, conv_w, conv_b, wa_bd, b_a, wx_bd, b_x, lam)


def _gelu(x):
    return 0.5 * x * (1.0 + lax.erf(x * (1.0 / math.sqrt(2.0))))


def _sgu_kernel(u_ref, v_ref, g_ref, lng_ref, wm_ref, bs_ref, o_ref, *, tr):
    u = _gelu(u_ref[...].astype(F32))
    v = _gelu(v_ref[...].astype(F32))
    mu = jnp.mean(v, axis=-1, keepdims=True)
    vc = v - mu
    vn = (vc * lax.rsqrt(jnp.mean(vc * vc, axis=-1, keepdims=True) + NORM_EPS) * lng_ref[...]).astype(BF16)
    lane = lax.broadcasted_iota(jnp.int32, (SG_CHUNK, LANES), 1)
    chunks = []
    for c in range(tr // SG_CHUNK):
        vch = vn[c * SG_CHUNK:(c + 1) * SG_CHUNK, :]
        pairs = []
        for p in range(SG_GROUPS // 2):
            vp = vch[:, p * LANES:(p + 1) * LANES]
            pairs.append(jnp.where(lane < HEAD_DIM, _mm(wm_ref[2 * p], vp), _mm(wm_ref[2 * p + 1], vp)))
        chunks.append(jnp.concatenate(pairs, axis=1) + bs_ref[...])
    mixed = jnp.concatenate(chunks, axis=0)
    o_ref[...] = (u * mixed * _silu(g_ref[...].astype(F32))).astype(BF16)


def _sgu(z, ln_g, wm, bs_full, n, tr):
    W = BRANCH_WIDTH
    zb = lambda off: pl.BlockSpec((tr, W), lambda i: (i, off // W))
    return pl.pallas_call(
        functools.partial(_sgu_kernel, tr=tr),
        grid=(n // tr,),
        in_specs=[zb(OFF_UD), zb(OFF_VD), zb(OFF_GD),
                  pl.BlockSpec((1, W), lambda i: (0, 0)),
                  pl.BlockSpec((SG_GROUPS, SG_CHUNK, SG_CHUNK), lambda i: (0, 0, 0)),
                  pl.BlockSpec((SG_CHUNK, W), lambda i: (0, 0))],
        out_specs=pl.BlockSpec((tr, W), lambda i: (i, 0)),
        out_shape=jax.ShapeDtypeStruct((n, W), BF16),
        compiler_params=pltpu.CompilerParams(dimension_semantics=("parallel",)),
        name="sgu",
    )(z, z, z, ln_g, wm, bs_full)


def _compress_kernel(t16_ref, ptop_ref, pbot_ref, w1t_ref, w1b_ref, w2_ref, bd_ref, gain_ref,
                     kc_ref, vc_ref, *, nc):
    t16 = t16_ref[...].astype(F32)
    a = _mm((t16 + ptop_ref[...]).astype(BF16), w1t_ref[...])
    bm = _mm((t16 + pbot_ref[...]).astype(BF16), w1b_ref[...])
    h = a + pltpu.roll(bm, nc - 1, axis=0)
    out = _mm(_silu(h).astype(BF16), w2_ref[...])
    ms = _mm((out * out).astype(BF16), bd_ref[...])
    kn = out * lax.rsqrt(ms + NORM_EPS) * gain_ref[...]
    for g in range(NSA_KV_HEADS):
        kc_ref[g] = kn[:, g * LANES:(g + 1) * LANES].astype(BF16)
        vc_ref[g] = out[:, (2 + g) * LANES:(3 + g) * LANES].astype(BF16)


def _compress(t16, ptop, pbot, w1t, w1b, w2, bd, gain, b, nc):
    kw = 16 * 4 * HEAD_DIM
    full = lambda shape: pl.BlockSpec(shape, lambda bi: tuple(0 for _ in shape))
    outspec = pl.BlockSpec((None, NSA_KV_HEADS, nc, LANES), lambda bi: (bi, 0, 0, 0))
    return pl.pallas_call(
        functools.partial(_compress_kernel, nc=nc),
        grid=(b,),
        in_specs=[pl.BlockSpec((None, nc, kw), lambda bi: (bi, 0, 0)),
                  full((1, kw)), full((1, kw)), full((kw, 4 * CMP_HIDDEN)), full((kw, 4 * CMP_HIDDEN)),
                  full((4 * CMP_HIDDEN, ZT)), full((ZT, ZT)), full((1, ZT))],
        out_specs=[outspec, outspec],
        out_shape=[jax.ShapeDtypeStruct((b, NSA_KV_HEADS, nc, LANES), BF16)] * 2,
        compiler_params=pltpu.CompilerParams(
            dimension_semantics=("parallel",), vmem_limit_bytes=V7X_VMEM_LIMIT),
        name="compress",
    )(t16, ptop, pbot, w1t, w1b, w2, bd, gain)


def _nsa_kernel(q_ref, gc_ref, ks_ref, kw_ref, vs_ref, vw_ref, kc_ref, vc_ref, gt_ref, e_ref, ov_ref,
                o_ref, sc_ref, m_ref, l_ref, acc_ref, *, s_len, nc):
    i = pl.program_id(2)
    qb, rows = Q_BLOCK, NSA_GROUP * Q_BLOCK
    ns = s_len // SLC_BLOCK
    top_k = min(SLC_TOPK, ns)
    kt_w = 2 * LANES
    span = min(WINDOW + Q_BLOCK, s_len)
    lane = lax.broadcasted_iota(jnp.int32, (qb, LANES), 1)

    parts = []
    for r in range(NSA_GROUP):
        blk = q_ref[:, (r // 2) * LANES:(r // 2 + 1) * LANES].astype(F32)
        parts.append(jnp.where(lane // HEAD_DIM == r % 2, blk, 0.0).astype(BF16))
    qs = jnp.concatenate(parts, axis=0)
    trow = i * qb + lax.broadcasted_iota(jnp.int32, (rows, 1), 0) % qb

    s_c = _nt(qs, kc_ref[...])
    cend = lax.broadcasted_iota(jnp.int32, s_c.shape, 1) * CMP_STRIDE + (CMP_BLOCK - 1)
    mask_c = cend <= trow
    sm = jnp.where(mask_c, s_c, NEG_INF)
    p = jnp.where(mask_c, jnp.exp(sm - jnp.max(sm, axis=1, keepdims=True)), 0.0)
    l_c = jnp.sum(p, axis=1, keepdims=True)
    p_c = p * jnp.where(l_c > 0.0, 1.0 / l_c, 0.0)
    o_c = _mm(p_c.astype(BF16), vc_ref[...])

    ps = p_c[0:qb] + p_c[qb:2 * qb] + p_c[2 * qb:3 * qb] + p_c[3 * qb:4 * qb]
    ps_hi = ps.astype(BF16)
    ps_lo = (ps - ps_hi.astype(F32)).astype(BF16)
    imp = _mm(ps_hi, ov_ref[...]) + _mm(ps_lo, ov_ref[...])

    jt = (i * qb + lax.broadcasted_iota(jnp.int32, (qb, LANES), 0)) // SLC_BLOCK
    valid = (lane <= jt) & (lane < ns)
    forced = (lane == 0) | (valid & (lane > jt - SLC_LOCAL))
    score = jnp.where(forced, SLC_FORCE_SCORE, jnp.where(valid, imp, -1.0))
    score = jnp.where(lane < ns, score, -2.0)
    nsp = -(-ns // 8) * 8
    sc_ref[...] = score.T
    sc_t = sc_ref[0:nsp, :]
    jrow = lax.broadcasted_iota(jnp.int32, (nsp, qb), 0)
    cnt = jnp.zeros((nsp, qb), jnp.int32)
    for jp in range(ns):
        other = sc_ref[jp:jp + 1, :]
        beats = (other > sc_t) | ((other == sc_t) & (jrow > jp))
        cnt = cnt + beats.astype(jnp.int32)
    sel_t = (cnt < top_k) & (sc_t >= 0.0)
    selb_t = jnp.where(sel_t, 0.0, SEL_NEG)
    if nsp < LANES:
        selb_t = jnp.concatenate([selb_t, jnp.full((LANES - nsp, qb), SEL_NEG, F32)], axis=0)
    selb = selb_t.T.astype(BF16)
    lhs = jnp.concatenate([qs, jnp.concatenate([selb] * NSA_GROUP, axis=0)], axis=1)

    m_ref[...] = jnp.full_like(m_ref, NEG_INF)
    l_ref[...] = jnp.zeros_like(l_ref)
    acc_ref[...] = jnp.zeros_like(acc_ref)

    def sel_tile(kt, diag):
        k0 = pl.multiple_of(kt * kt_w, kt_w)
        rhs = jnp.concatenate([ks_ref[pl.ds(k0, kt_w), :], e_ref[pl.ds(k0, kt_w), :]], axis=1)
        s = _nt(lhs, rhs)
        if diag:
            pos = k0 + lax.broadcasted_iota(jnp.int32, s.shape, 1)
            s = jnp.where(pos <= trow, s, NEG_INF)
        m_prev = m_ref[...]
        m_new = jnp.maximum(m_prev, jnp.max(s, axis=1, keepdims=True))
        alpha = jnp.exp(m_prev - m_new)
        pr = jnp.exp(s - m_new)
        l_ref[...] = alpha * l_ref[...] + jnp.sum(pr, axis=1, keepdims=True)
        acc_ref[...] = alpha * acc_ref[...] + _mm(pr.astype(BF16), vs_ref[pl.ds(k0, kt_w), :])
        m_ref[...] = m_new

    n_tiles = (i + 2) // 2

    def body(kt, carry):
        sel_tile(kt, False)
        return carry

    lax.fori_loop(0, n_tiles - 1, body, 0)
    sel_tile(n_tiles - 1, True)
    o_s = acc_ref[...] / l_ref[...]

    start = pl.multiple_of(jnp.maximum(i - WINDOW // qb, 0) * qb, qb)
    s_w = _nt(qs, kw_ref[pl.ds(start, span), :])
    pos = start + lax.broadcasted_iota(jnp.int32, s_w.shape, 1)
    mask_w = (pos <= trow) & (pos > trow - WINDOW)
    sm = jnp.where(mask_w, s_w, NEG_INF)
    p = jnp.where(mask_w, jnp.exp(sm - jnp.max(sm, axis=1, keepdims=True)), 0.0)
    l_w = jnp.sum(p, axis=1, keepdims=True)
    o_w = _mm(p.astype(BF16), vw_ref[pl.ds(start, span), :]) / l_w

    gts = _sigmoid(gt_ref[...])
    pairs = []
    for pidx in range(NSA_GROUP // 2):
        halves = []
        for hh in range(2):
            r = 2 * pidx + hh
            sl = slice(r * qb, (r + 1) * qb)
            halves.append(gts[:, r:r + 1] * o_c[sl] + gts[:, 4 + r:5 + r] * o_s[sl]
                          + gts[:, 8 + r:9 + r] * o_w[sl])
        pairs.append(jnp.where(lane < HEAD_DIM, halves[0], halves[1]))
    y = jnp.concatenate(pairs, axis=1)
    o_ref[...] = (y * _silu(gc_ref[...].astype(F32))).astype(BF16)


def _nsa(z, small, kc, vc, e128, ov, b, s):
    n = b * s
    nqb = s // Q_BLOCK
    nc = s // CMP_STRIDE
    gw = NSA_GROUP * HEAD_DIM
    rows = NSA_GROUP * Q_BLOCK
    qspec = lambda off: pl.BlockSpec((Q_BLOCK, gw), lambda bi, g, i: (bi * nqb + i, off // gw + g))
    seq = lambda off: pl.BlockSpec((s, LANES), lambda bi, g, i: (bi, off // LANES + g))
    cspec = pl.BlockSpec((None, None, nc, LANES), lambda bi, g, i: (bi, g, 0, 0))
    return pl.pallas_call(
        functools.partial(_nsa_kernel, s_len=s, nc=nc),
        grid=(b, NSA_KV_HEADS, nqb),
        in_specs=[qspec(OFF_QC), qspec(OFF_GC), seq(OFF_KSD), seq(OFF_KWD), seq(OFF_VSD), seq(OFF_VWD),
                  cspec, cspec,
                  pl.BlockSpec((Q_BLOCK, LANES), lambda bi, g, i: (bi * nqb + i, 1 + g)),
                  pl.BlockSpec((s, LANES), lambda bi, g, i: (0, 0)),
                  pl.BlockSpec((nc, LANES), lambda bi, g, i: (0, 0))],
        out_specs=pl.BlockSpec((Q_BLOCK, gw), lambda bi, g, i: (bi * nqb + i, g)),
        out_shape=jax.ShapeDtypeStruct((n, BRANCH_WIDTH), BF16),
        scratch_shapes=[pltpu.VMEM((LANES, Q_BLOCK), F32), pltpu.VMEM((rows, 1), F32),
                        pltpu.VMEM((rows, 1), F32), pltpu.VMEM((rows, LANES), F32)],
        compiler_params=pltpu.CompilerParams(
            dimension_semantics=("parallel", "parallel", "arbitrary"),
            vmem_limit_bytes=V7X_VMEM_LIMIT),
        name="nsa",
    )(z, z, z, z, z, z, kc, vc, small, e128, ov)


def _merge_kernel(ya_ref, yb_ref, yc_ref, yd_ref, mg_ref, x_ref, wb_ref, wo_ref, o_ref):
    merged = None
    for nb, y_ref in enumerate((ya_ref, yb_ref, yc_ref, yd_ref)):
        gate = _sigmoid(mg_ref[:, nb * D_MODEL:(nb + 1) * D_MODEL].astype(F32))
        term = gate * _mm(y_ref[...], wb_ref[nb])
        merged = term if merged is None else merged + term
    o_ref[...] = x_ref[...] + _mm(merged.astype(BF16), wo_ref[...])


def _merge(ys, z, x2, wb, wo, tm):
    n = x2.shape[0]
    W = BRANCH_WIDTH
    yspec = pl.BlockSpec((tm, W), lambda i: (i, 0))
    return pl.pallas_call(
        _merge_kernel,
        grid=(n // tm,),
        in_specs=[yspec, yspec, yspec, yspec,
                  pl.BlockSpec((tm, N_BRANCHES * D_MODEL), lambda i: (i, 0)),
                  pl.BlockSpec((tm, D_MODEL), lambda i: (i, 0)),
                  pl.BlockSpec((N_BRANCHES, W, D_MODEL), lambda i: (0, 0, 0)),
                  pl.BlockSpec((D_MODEL, D_MODEL), lambda i: (0, 0))],
        out_specs=pl.BlockSpec((tm, D_MODEL), lambda i: (i, 0)),
        out_shape=jax.ShapeDtypeStruct((n, D_MODEL), F32),
        compiler_params=pltpu.CompilerParams(
            dimension_semantics=("parallel",), vmem_limit_bytes=V7X_VMEM_LIMIT),
        name="merge",
    )(*ys, z, x2, wb, wo)


def _block_diag(blocks):
    nb, a, b_ = blocks.shape
    eye = jnp.eye(nb, dtype=blocks.dtype)
    return jnp.einsum('nab,nm->namb', blocks, eye).reshape(nb * a, nb * b_)


def _mean_matrix(width):
    idx = np.arange(width) // HEAD_DIM
    return jnp.asarray((idx[:, None] == idx[None, :]).astype(np.float32) / HEAD_DIM, BF16)


def _layout_w_in(w_in, qn_a, kn_a, qn_c, kn_c):
    sizes = _split_sizes()
    offs = np.concatenate([[0], np.cumsum(sizes)])
    seg = lambda k: w_in[:, int(offs[k]):int(offs[k + 1])]
    (qa, ka, va, fa, ga, xb, gb, qc, kcc, vcc, ksc, vsc, kwc, vwc, gate_c, gc, ud, vd, gd, mg) = [
        seg(k) for k in range(len(sizes))]
    dup = lambda w: jnp.concatenate([w[:, :HEAD_DIM], w[:, :HEAD_DIM], w[:, HEAD_DIM:], w[:, HEAD_DIM:]], axis=1)
    pad = jnp.zeros((D_MODEL, 256), w_in.dtype)
    w_z = jnp.concatenate([mg, qa, ka, va, ga, xb, gb, qc, gc, ud, vd, gd,
                           dup(ksc), dup(kwc), dup(vsc), dup(vwc), kcc, vcc, pad], axis=1).astype(BF16)
    scale = 1.0 / math.sqrt(HEAD_DIM)
    gain = jnp.ones((ZW,), F32)
    gain = gain.at[OFF_QA:OFF_QA + ZT].set(jnp.tile(qn_a * scale, FOX_HEADS))
    gain = gain.at[OFF_KA:OFF_KA + ZT].set(jnp.tile(kn_a, FOX_HEADS))
    gain = gain.at[OFF_QC:OFF_QC + ZT].set(jnp.tile(qn_c * scale, NSA_HEADS))
    gain = gain.at[OFF_KSD:OFF_KSD + ZT].set(jnp.tile(kn_c, ZT // HEAD_DIM))
    gate3 = gate_c.reshape(D_MODEL, 3, NSA_KV_HEADS, NSA_GROUP)
    zpad = lambda w: jnp.pad(w, ((0, 0), (0, LANES - w.shape[1])))
    w_small = jnp.concatenate(
        [zpad(fa)] + [zpad(gate3[:, :, g, :].reshape(D_MODEL, 3 * NSA_GROUP)) for g in range(NSA_KV_HEADS)],
        axis=1).astype(BF16)
    return w_z, gain.reshape(1, ZW), w_small


def _layout_compress(cmp_pos, wk1, wk2, wv1, wv2, kn_c):
    half = CMP_BLOCK // 2
    eye4 = jnp.eye(4, dtype=F32)

    def w1_half(lo):
        k = wk1.reshape(CMP_BLOCK, HEAD_DIM, CMP_HIDDEN)[lo:lo + half]
        v = wv1.reshape(CMP_BLOCK, HEAD_DIM, CMP_HIDDEN)[lo:lo + half]
        stack = jnp.stack([k, k, v, v])
        return jnp.einsum('spdn,st->psdtn', stack, eye4).reshape(half * 4 * HEAD_DIM, 4 * CMP_HIDDEN).astype(BF16)

    def pos_half(lo):
        p = cmp_pos[lo:lo + half]
        return jnp.broadcast_to(p[:, None, :], (half, 4, HEAD_DIM)).reshape(1, half * 4 * HEAD_DIM)

    w2 = jnp.zeros((4, CMP_HIDDEN, 4, 2, HEAD_DIM), F32)
    for slot, w in enumerate((wk2, wk2, wv2, wv2)):
        w2 = w2.at[slot, :, slot, :, :].set(jnp.broadcast_to(w[:, None, :], (CMP_HIDDEN, 2, HEAD_DIM)))
    w2 = w2.reshape(4 * CMP_HIDDEN, ZT).astype(BF16)
    gain = jnp.tile(kn_c, ZT // HEAD_DIM).reshape(1, ZT)
    return pos_half(0), pos_half(half), w1_half(0), w1_half(half), w2, gain


@functools.lru_cache(maxsize=None)
def _nsa_consts(s):
    nc = s // CMP_STRIDE
    ns = s // SLC_BLOCK
    e = np.zeros((s, LANES), np.float32)
    e[np.arange(s), np.arange(s) // SLC_BLOCK] = 1.0
    c0 = np.arange(nc) * CMP_STRIDE
    s0 = np.arange(ns) * SLC_BLOCK
    ovl = np.minimum(c0[:, None] + CMP_BLOCK, s0[None, :] + SLC_BLOCK) - np.maximum(c0[:, None], s0[None, :])
    ov = np.zeros((nc, LANES), np.float32)
    ov[:, :ns] = np.clip(ovl, 0, None) / CMP_BLOCK
    ov[nc - 1, :] = 0.0
    return e, ov


def _tile(s, pref):
    t = pref
    while s % t:
        t //= 2
    return t


def _layer(x2, b, s, norm_g, w_in, b_forget, qn_a, kn_a, conv_w, conv_b, w_rg_a, b_rg_a, w_rg_x,
           b_rg_x, lru_lambda, qn_c, kn_c, cmp_pos, wk1, wk2, wv1, wv2, ln_v_g, w_spatial,
           b_spatial, w_branch, w_out):
    n = b * s
    W = BRANCH_WIDTH
    w_z, gain_row, w_small = _layout_w_in(w_in, qn_a, kn_a, qn_c, kn_c)
    bd = _mean_matrix(ZT)
    z, small = _inproj(x2, norm_g.reshape(1, D_MODEL), w_z, gain_row, bd, w_small, _tile(n, 1024))

    bf_row = jnp.pad(b_forget, (0, LANES - FOX_HEADS)).reshape(1, LANES)
    caq, cak = _cprep(small, bf_row, b, s, _tile(s, 512))
    ys_a = _fox(z, caq, cak, b, s, _tile(s, 512))

    ys_b = _lru(z, conv_w, conv_b.reshape(1, W), _block_diag(w_rg_a).astype(BF16), b_rg_a.reshape(1, W),
                _block_diag(w_rg_x).astype(BF16), b_rg_x.reshape(1, W), lru_lambda.reshape(1, W),
                b, s, _tile(s, 256))

    nc = s // CMP_STRIDE
    t16 = z[:, OFF_CMP:OFF_CMP + 4 * HEAD_DIM].reshape(b, nc, CMP_STRIDE * 4 * HEAD_DIM)
    ptop, pbot, w1t, w1b, w2c, cgain = _layout_compress(cmp_pos, wk1, wk2, wv1, wv2, kn_c)
    kc, vc = _compress(t16, ptop, pbot, w1t, w1b, w2c, bd, cgain, b, nc)
    e_np, ov_np = _nsa_consts(s)
    ys_c = _nsa(z, small, kc, vc, jnp.asarray(e_np, BF16), jnp.asarray(ov_np, BF16), b, s)

    causal = jnp.tril(jnp.ones((SG_CHUNK, SG_CHUNK), F32))
    wm = (w_spatial * causal).astype(BF16)
    bs_full = jnp.broadcast_to(b_spatial.T[:, :, None], (SG_CHUNK, SG_GROUPS, HEAD_DIM)).reshape(SG_CHUNK, W)
    ys_d = _sgu(z, ln_v_g.reshape(1, W), wm, bs_full, n, _tile(s, 512))

    return _merge((ys_a, ys_b, ys_c, ys_d), z, x2, w_branch.astype(BF16), w_out.astype(BF16), _tile(n, 512))


def kernel(x, norm_g, w_in, b_forget, qn_a, kn_a, conv_w, conv_b, w_rg_a, b_rg_a, w_rg_x, b_rg_x,
           lru_lambda, qn_c, kn_c, cmp_pos, cmp_k_w1, cmp_k_w2, cmp_v_w1, cmp_v_w2, ln_v_g,
           w_spatial, b_spatial, w_branch, w_out):
    b, s, d = x.shape
    x2 = x.reshape(b * s, d)
    params = (norm_g, w_in, b_forget, qn_a, kn_a, conv_w, conv_b, w_rg_a, b_rg_a, w_rg_x, b_rg_x,
              lru_lambda, qn_c, kn_c, cmp_pos, cmp_k_w1, cmp_k_w2, cmp_v_w1, cmp_v_w2, ln_v_g,
              w_spatial, b_spatial, w_branch, w_out)
    for layer in range(norm_g.shape[0]):
        x2 = _layer(x2, b, s, *[p[layer] for p in params])
    return x2.reshape(b, s, d)
```

```python
import functools
import math

import numpy as np
import jax
import jax.numpy as jnp
from jax import lax
from jax.experimental import pallas as pl
from jax.experimental.pallas import tpu as pltpu

F32 = jnp.float32
BF16 = jnp.bfloat16

D_MODEL = 1024
HEAD_DIM = 64
BRANCH_WIDTH = 512
N_BRANCHES = 4
FOX_HEADS = 8
LRU_BLOCKS = 8
CONV_WIDTH = 4
LRU_C = 8.0
NSA_HEADS = 8
NSA_KV_HEADS = 2
NSA_GROUP = 4
CMP_BLOCK = 32
CMP_STRIDE = 16
CMP_HIDDEN = 128
SLC_BLOCK = 64
SLC_TOPK = 16
SLC_LOCAL = 2
SLC_FORCE_SCORE = 1e6
WINDOW = 512
Q_BLOCK = 128
SG_CHUNK = 128
SG_GROUPS = 8
NORM_EPS = 1e-6
NEG_INF = -1e30
SEL_NEG = -32768.0

LANES = 128
V7X_VMEM_LIMIT = 56 * 1024 * 1024
QSTRIP = 256

ZT = 512
OFF_MG = 0
OFF_QA = 8 * ZT
OFF_KA = 9 * ZT
OFF_VA = 10 * ZT
OFF_GA = 11 * ZT
OFF_XB = 12 * ZT
OFF_GB = 13 * ZT
OFF_QC = 14 * ZT
OFF_GC = 15 * ZT
OFF_UD = 16 * ZT
OFF_VD = 17 * ZT
OFF_GD = 18 * ZT
OFF_KSD = 19 * ZT
OFF_KWD = 19 * ZT + 256
OFF_VSD = 20 * ZT
OFF_VWD = 20 * ZT + 256
OFF_CMP = 21 * ZT
ZW = 22 * ZT
NORM_TILES = (8, 9, 14, 19)
SMALL_W = 384
VT_TILES = (OFF_VA // ZT, OFF_VSD // ZT)


def _nt(a, b):
    return lax.dot_general(a, b, (((1,), (1,)), ((), ())), preferred_element_type=F32)


def _mm(a, b):
    return jnp.dot(a, b, preferred_element_type=F32)


def _sigmoid(x):
    return 1.0 / (1.0 + jnp.exp(-x))


def _silu(x):
    return x * _sigmoid(x)


def _split_sizes():
    W, kv = BRANCH_WIDTH, NSA_KV_HEADS * HEAD_DIM
    return (W, W, W, FOX_HEADS, W, W, W, W, kv, kv, kv, kv, kv, kv, 3 * NSA_HEADS, W,
            W, W, W, N_BRANCHES * D_MODEL)


def _inproj_kernel(x_ref, g_ref, w_ref, gain_ref, bd_ref, ws_ref, z_ref, small_ref, xn_ref):
    j = pl.program_id(1)

    @pl.when(j == 0)
    def _():
        x = x_ref[...]
        ms = jnp.mean(x * x, axis=-1, keepdims=True)
        xn = ((x * lax.rsqrt(ms + NORM_EPS)) * g_ref[...]).astype(BF16)
        xn_ref[...] = xn
        small_ref[...] = _mm(xn, ws_ref[...])

    acc = _mm(xn_ref[...], w_ref[...])
    is_norm = functools.reduce(jnp.logical_or, [j == t for t in NORM_TILES])

    @pl.when(is_norm)
    def _():
        ms = _mm((acc * acc).astype(BF16), bd_ref[...])
        z_ref[...] = (acc * lax.rsqrt(ms + NORM_EPS) * gain_ref[...]).astype(BF16)

    @pl.when(jnp.logical_not(is_norm))
    def _():
        z_ref[...] = acc.astype(BF16)


def _inproj(x2, norm_g, w_z, gain_row, bd, w_small, tm):
    n = x2.shape[0]
    grid = (n // tm, ZW // ZT)
    return pl.pallas_call(
        _inproj_kernel,
        grid=grid,
        in_specs=[
            pl.BlockSpec((tm, D_MODEL), lambda i, j: (i, 0)),
            pl.BlockSpec((1, D_MODEL), lambda i, j: (0, 0)),
            pl.BlockSpec((D_MODEL, ZT), lambda i, j: (0, j)),
            pl.BlockSpec((1, ZT), lambda i, j: (0, j)),
            pl.BlockSpec((ZT, ZT), lambda i, j: (0, 0)),
            pl.BlockSpec((D_MODEL, SMALL_W), lambda i, j: (0, 0)),
        ],
        out_specs=[
            pl.BlockSpec((tm, ZT), lambda i, j: (i, j)),
            pl.BlockSpec((tm, SMALL_W), lambda i, j: (i, 0)),
        ],
        out_shape=[
            jax.ShapeDtypeStruct((n, ZW), BF16),
            jax.ShapeDtypeStruct((n, SMALL_W), F32),
        ],
        scratch_shapes=[pltpu.VMEM((tm, D_MODEL), BF16)],
        compiler_params=pltpu.CompilerParams(
            dimension_semantics=("parallel", "arbitrary"), vmem_limit_bytes=V7X_VMEM_LIMIT),
        name="inproj",
    )(x2, norm_g, w_z, gain_row, bd, w_small)


def _vtrans_kernel(z_ref, o_ref):
    o_ref[...] = z_ref[...].astype(F32).T.astype(BF16)


def _vtrans(z, tm):
    n = z.shape[0]
    step = VT_TILES[1] - VT_TILES[0]
    return pl.pallas_call(
        _vtrans_kernel,
        grid=(n // tm, len(VT_TILES)),
        in_specs=[pl.BlockSpec((tm, ZT), lambda i, j: (i, VT_TILES[0] + step * j))],
        out_specs=pl.BlockSpec((ZT, tm), lambda i, j: (j, i)),
        out_shape=jax.ShapeDtypeStruct((len(VT_TILES) * ZT, n), BF16),
        compiler_params=pltpu.CompilerParams(dimension_semantics=("parallel", "parallel")),
        name="vtrans",
    )(z)


def _cprep_kernel(fa_ref, bf_ref, pq_ref, pk_ref, oq_ref, ok_ref, caq_ref, cak_ref, carry_ref,
                  *, tt):
    @pl.when(pl.program_id(1) == 0)
    def _():
        carry_ref[...] = jnp.zeros_like(carry_ref)

    x = fa_ref[...] + bf_ref[...]
    c = jnp.minimum(x, 0.0) - jnp.log(1.0 + jnp.exp(-jnp.abs(x)))
    row = lax.broadcasted_iota(jnp.int32, c.shape, 0)
    d = 1
    while d < tt:
        c = c + jnp.where(row >= d, pltpu.roll(c, d, axis=0), 0.0)
        d *= 2
    c = c + carry_ref[...]
    carry_ref[...] = c[tt - 1:tt, :]
    hi = c.astype(BF16)
    r1 = c - hi.astype(F32)
    mid = r1.astype(BF16)
    lo = (r1 - mid.astype(F32)).astype(BF16)
    parts = jnp.concatenate([hi, mid, lo], axis=1)
    caq_ref[...] = (_mm(parts, pq_ref[...]) + oq_ref[...]).astype(BF16)
    cak_ref[...] = (_mm(parts, pk_ref[...]) + ok_ref[...]).astype(BF16)


def _aug_lane(h):
    return (h // 2) * LANES + (h % 2) * 8


@functools.lru_cache(maxsize=None)
def _cprep_consts():
    pq = np.zeros((3 * LANES, 4 * LANES), np.float32)
    pk = np.zeros((3 * LANES, 4 * LANES), np.float32)
    oq = np.zeros((1, 4 * LANES), np.float32)
    ok = np.zeros((1, 4 * LANES), np.float32)
    for h in range(FOX_HEADS):
        base = _aug_lane(h)
        for part in range(3):
            pq[part * LANES + h, base + part] = 1.0
            pk[part * LANES + h, base + 3 + part] = -1.0
            oq[0, base + 3 + part] = 1.0
            ok[0, base + part] = 1.0
    return pq, pk, oq, ok


def _cprep(small, bf_row, b, s, tt):
    n = b * s
    nt = s // tt
    pq, pk, oq, ok = _cprep_consts()
    full = lambda shape: pl.BlockSpec(shape, lambda bi, ti: (0, 0))
    return pl.pallas_call(
        functools.partial(_cprep_kernel, tt=tt),
        grid=(b, nt),
        in_specs=[
            pl.BlockSpec((tt, LANES), lambda bi, ti: (bi * nt + ti, 0)),
            full((1, LANES)),
            full((3 * LANES, 4 * LANES)),
            full((3 * LANES, 4 * LANES)),
            full((1, 4 * LANES)),
            full((1, 4 * LANES)),
        ],
        out_specs=[
            pl.BlockSpec((tt, 4 * LANES), lambda bi, ti: (bi * nt + ti, 0)),
            pl.BlockSpec((tt, 4 * LANES), lambda bi, ti: (bi * nt + ti, 0)),
        ],
        out_shape=[jax.ShapeDtypeStruct((n, 4 * LANES), BF16)] * 2,
        scratch_shapes=[pltpu.VMEM((1, LANES), F32)],
        compiler_params=pltpu.CompilerParams(dimension_semantics=("parallel", "arbitrary")),
        name="cprep",
    )(small, bf_row, jnp.asarray(pq, BF16), jnp.asarray(pk, BF16), jnp.asarray(oq), jnp.asarray(ok))


def _fox_kernel(qi_ref, ki_ref, q_ref, caq_ref, k_ref, cak_ref, vt_ref, ga_ref, o_ref,
                lhs_ref, rhs_ref, m_ref, l_ref, acc_ref, *, t):
    pair = pl.program_id(1)
    qi = qi_ref[pair]
    ki = ki_ref[pair]
    n_pairs = FOX_HEADS // 2

    @pl.when(ki == 0)
    def _():
        lane = lax.broadcasted_iota(jnp.int32, (t, 2 * LANES), 1)
        head_of_lane = jnp.where(lane < LANES, lane // HEAD_DIM, (lane - LANES) // 8)
        for hp in range(n_pairs):
            cols = slice(hp * LANES, (hp + 1) * LANES)
            lhs = jnp.concatenate([q_ref[:, cols], caq_ref[:, cols]], axis=1).astype(F32)
            for hh in range(2):
                lhs_ref[2 * hp + hh] = jnp.where(head_of_lane == hh, lhs, 0.0).astype(BF16)
        m_ref[...] = jnp.full_like(m_ref, NEG_INF)
        l_ref[...] = jnp.zeros_like(l_ref)
        acc_ref[...] = jnp.zeros_like(acc_ref)

    def step(diag):
        for hp in range(n_pairs):
            cols = slice(hp * LANES, (hp + 1) * LANES)
            rhs_ref[hp] = jnp.concatenate([k_ref[:, cols], cak_ref[:, cols]], axis=1)

        s_next = _nt(rhs_ref[0], lhs_ref[0])
        for h in range(FOX_HEADS):
            s = s_next
            if h + 1 < FOX_HEADS:
                s_next = _nt(rhs_ref[(h + 1) // 2], lhs_ref[h + 1])
            if diag:
                key = lax.broadcasted_iota(jnp.int32, s.shape, 0)
                qry = lax.broadcasted_iota(jnp.int32, s.shape, 1)
                s = jnp.where(key <= qry, s, NEG_INF)
            m_prev = m_ref[h]
            m_new = jnp.maximum(m_prev, jnp.max(s, axis=0, keepdims=True))
            alpha = jnp.exp(m_prev - m_new)
            p = jnp.exp(s - m_new)
            l_ref[h] = alpha * l_ref[h] + jnp.sum(p, axis=0, keepdims=True)
            vt = vt_ref[h * HEAD_DIM:(h + 1) * HEAD_DIM, :]
            acc_ref[h] = alpha * acc_ref[h] + _mm(vt, p.astype(BF16))
            m_ref[h] = m_new

    @pl.when(ki < qi)
    def _():
        step(False)

    @pl.when(ki == qi)
    def _():
        step(True)
        for hp in range(n_pairs):
            cols = slice(hp * LANES, (hp + 1) * LANES)
            o_t = jnp.concatenate([acc_ref[2 * hp] / l_ref[2 * hp],
                                   acc_ref[2 * hp + 1] / l_ref[2 * hp + 1]], axis=0)
            o_ref[:, cols] = (o_t.T * _silu(ga_ref[:, cols].astype(F32))).astype(BF16)


def _fox(z, vt, caq, cak, b, s, t):
    n = b * s
    nq = s // t
    W = BRANCH_WIDTH
    qi_of = np.concatenate([np.full(q + 1, q) for q in range(nq)]).astype(np.int32)
    ki_of = np.concatenate([np.arange(q + 1) for q in range(nq)]).astype(np.int32)
    qmap = lambda off: (lambda bi, p, qi, ki: (bi * nq + qi[p], off // W))
    kmap = lambda off: (lambda bi, p, qi, ki: (bi * nq + ki[p], off // W))
    grid_spec = pltpu.PrefetchScalarGridSpec(
        num_scalar_prefetch=2,
        grid=(b, len(qi_of)),
        in_specs=[
            pl.BlockSpec((t, W), qmap(OFF_QA)),
            pl.BlockSpec((t, W), qmap(0)),
            pl.BlockSpec((t, W), kmap(OFF_KA)),
            pl.BlockSpec((t, W), kmap(0)),
            pl.BlockSpec((W, t), lambda bi, p, qi, ki: (0, bi * nq + ki[p])),
            pl.BlockSpec((t, W), qmap(OFF_GA)),
        ],
        out_specs=pl.BlockSpec((t, W), qmap(0)),
        scratch_shapes=[
            pltpu.VMEM((FOX_HEADS, t, 2 * LANES), BF16),
            pltpu.VMEM((FOX_HEADS // 2, t, 2 * LANES), BF16),
            pltpu.VMEM((FOX_HEADS, 1, t), F32),
            pltpu.VMEM((FOX_HEADS, 1, t), F32),
            pltpu.VMEM((FOX_HEADS, HEAD_DIM, t), F32),
        ],
    )
    return pl.pallas_call(
        functools.partial(_fox_kernel, t=t),
        grid_spec=grid_spec,
        out_shape=jax.ShapeDtypeStruct((n, W), BF16),
        compiler_params=pltpu.CompilerParams(
            dimension_semantics=("parallel", "arbitrary"), vmem_limit_bytes=V7X_VMEM_LIMIT),
        name="fox",
    )(jnp.asarray(qi_of), jnp.asarray(ki_of), z, caq, z, cak, vt, z)


def _lru_kernel(xb_ref, gb_ref, cw_ref, cb_ref, wa_ref, ba_ref, wx_ref, bx_ref, lam_ref, o_ref,
                xpad_ref, h_ref, *, tt):
    @pl.when(pl.program_id(1) == 0)
    def _():
        xpad_ref[0:8, :] = jnp.zeros((8, BRANCH_WIDTH), F32)
        h_ref[...] = jnp.zeros_like(h_ref)

    x = xb_ref[...].astype(F32)
    xpad_ref[8:tt + 8, :] = x
    xc = cb_ref[...] + cw_ref[3:4, :] * x
    for j in range(CONV_WIDTH - 1):
        sh = CONV_WIDTH - 1 - j
        xc = xc + cw_ref[j:j + 1, :] * xpad_ref[8 - sh:8 - sh + tt, :]
    xpad_ref[0:8, :] = x[tt - 8:tt, :]

    xcb = xc.astype(BF16)
    r = _sigmoid(_mm(xcb, wa_ref[...]) + ba_ref[...])
    ig = _sigmoid(_mm(xcb, wx_ref[...]) + bx_ref[...])
    nl = -lam_ref[...]
    softplus = jnp.maximum(nl, 0.0) + jnp.log(1.0 + jnp.exp(-jnp.abs(nl)))
    log_a = (-LRU_C) * r * softplus
    a = jnp.exp(log_a)
    bb = jnp.sqrt(1.0 - jnp.exp(2.0 * log_a)) * (ig * xc)

    row = lax.broadcasted_iota(jnp.int32, a.shape, 0)
    d = 1
    while d < tt:
        keep = row >= d
        bb = bb + a * jnp.where(keep, pltpu.roll(bb, d, axis=0), 0.0)
        a = a * jnp.where(keep, pltpu.roll(a, d, axis=0), 1.0)
        d *= 2
    h = bb + a * h_ref[...]
    h_ref[...] = h[tt - 1:tt, :]
    o_ref[...] = (h * _silu(gb_ref[...].astype(F32))).astype(BF16)


def _lru(z, conv_w, conv_b, wa_bd, b_a, wx_bd, b_x, lam, b, s, tt):
    n = b * s
    nt = s // tt
    W = BRANCH_WIDTH
    full = lambda shape: pl.BlockSpec(shape, lambda bi, ti: (0, 0))
    zb = lambda off: pl.BlockSpec((tt, W), lambda bi, ti: (bi * nt + ti, off // W))
    return pl.pallas_call(
        functools.partial(_lru_kernel, tt=tt),
        grid=(b, nt),
        in_specs=[zb(OFF_XB), zb(OFF_GB), full((CONV_WIDTH, W)), full((1, W)), full((W, W)),
                  full((1, W)), full((W, W)), full((1, W)), full((1, W))],
        out_specs=pl.BlockSpec((tt, W), lambda bi, ti: (bi * nt + ti, 0)),
        out_shape=jax.ShapeDtypeStruct((n, W), BF16),
        scratch_shapes=[pltpu.VMEM((tt + 8, W), F32), pltpu.VMEM((1, W), F32)],
        compiler_params=pltpu.CompilerParams(dimension_semantics=("parallel", "arbitrary")),
        name="lru",
    )(z, z, conv_w, conv_b, wa_bd, b_a, wx_bd, b_x, lam)


def _gelu(x):
    return 0.5 * x * (1.0 + lax.erf(x * (1.0 / math.sqrt(2.0))))


def _sgu_kernel(u_ref, v_ref, g_ref, lng_ref, wm_ref, bs_ref, o_ref, *, tr):
    u = _gelu(u_ref[...].astype(F32))
    v = _gelu(v_ref[...].astype(F32))
    mu = jnp.mean(v, axis=-1, keepdims=True)
    vc = v - mu
    vn = (vc * lax.rsqrt(jnp.mean(vc * vc, axis=-1, keepdims=True) + NORM_EPS) * lng_ref[...]).astype(BF16)
    lane = lax.broadcasted_iota(jnp.int32, (SG_CHUNK, LANES), 1)
    chunks = []
    for c in range(tr // SG_CHUNK):
        vch = vn[c * SG_CHUNK:(c + 1) * SG_CHUNK, :]
        pairs = []
        for p in range(SG_GROUPS // 2):
            vp = vch[:, p * LANES:(p + 1) * LANES]
            pairs.append(jnp.where(lane < HEAD_DIM, _mm(wm_ref[2 * p], vp), _mm(wm_ref[2 * p + 1], vp)))
        chunks.append(jnp.concatenate(pairs, axis=1) + bs_ref[...])
    mixed = jnp.concatenate(chunks, axis=0)
    o_ref[...] = (u * mixed * _silu(g_ref[...].astype(F32))).astype(BF16)


def _sgu(z, ln_g, wm, bs_full, n, tr):
    W = BRANCH_WIDTH
    zb = lambda off: pl.BlockSpec((tr, W), lambda i: (i, off // W))
    return pl.pallas_call(
        functools.partial(_sgu_kernel, tr=tr),
        grid=(n // tr,),
        in_specs=[zb(OFF_UD), zb(OFF_VD), zb(OFF_GD),
                  pl.BlockSpec((1, W), lambda i: (0, 0)),
                  pl.BlockSpec((SG_GROUPS, SG_CHUNK, SG_CHUNK), lambda i: (0, 0, 0)),
                  pl.BlockSpec((SG_CHUNK, W), lambda i: (0, 0))],
        out_specs=pl.BlockSpec((tr, W), lambda i: (i, 0)),
        out_shape=jax.ShapeDtypeStruct((n, W), BF16),
        compiler_params=pltpu.CompilerParams(dimension_semantics=("parallel",)),
        name="sgu",
    )(z, z, z, ln_g, wm, bs_full)


def _compress_kernel(t16_ref, ptop_ref, pbot_ref, w1t_ref, w1b_ref, w2_ref, bd_ref, gain_ref,
                     kc_ref, vct_ref, *, nc):
    t16 = t16_ref[...].astype(F32)
    a = _mm((t16 + ptop_ref[...]).astype(BF16), w1t_ref[...])
    bm = _mm((t16 + pbot_ref[...]).astype(BF16), w1b_ref[...])
    h = a + pltpu.roll(bm, nc - 1, axis=0)
    out = _mm(_silu(h).astype(BF16), w2_ref[...])
    ms = _mm((out * out).astype(BF16), bd_ref[...])
    kn = out * lax.rsqrt(ms + NORM_EPS) * gain_ref[...]
    out_t = out.T
    for g in range(NSA_KV_HEADS):
        kc_ref[g] = kn[:, g * LANES:(g + 1) * LANES].astype(BF16)
        vct_ref[g] = out_t[(2 + g) * LANES:(2 + g) * LANES + HEAD_DIM, :].astype(BF16)


def _compress(t16, ptop, pbot, w1t, w1b, w2, bd, gain, b, nc):
    kw = 16 * 4 * HEAD_DIM
    full = lambda shape: pl.BlockSpec(shape, lambda bi: tuple(0 for _ in shape))
    return pl.pallas_call(
        functools.partial(_compress_kernel, nc=nc),
        grid=(b,),
        in_specs=[pl.BlockSpec((None, nc, kw), lambda bi: (bi, 0, 0)),
                  full((1, kw)), full((1, kw)), full((kw, 4 * CMP_HIDDEN)), full((kw, 4 * CMP_HIDDEN)),
                  full((4 * CMP_HIDDEN, ZT)), full((ZT, ZT)), full((1, ZT))],
        out_specs=[pl.BlockSpec((None, NSA_KV_HEADS, nc, LANES), lambda bi: (bi, 0, 0, 0)),
                   pl.BlockSpec((None, NSA_KV_HEADS, HEAD_DIM, nc), lambda bi: (bi, 0, 0, 0))],
        out_shape=[jax.ShapeDtypeStruct((b, NSA_KV_HEADS, nc, LANES), BF16),
                   jax.ShapeDtypeStruct((b, NSA_KV_HEADS, HEAD_DIM, nc), BF16)],
        compiler_params=pltpu.CompilerParams(
            dimension_semantics=("parallel",), vmem_limit_bytes=V7X_VMEM_LIMIT),
        name="compress",
    )(t16, ptop, pbot, w1t, w1b, w2, bd, gain)


def _softmax_cols(s, mask):
    sm = jnp.where(mask, s, NEG_INF)
    p = jnp.where(mask, jnp.exp(sm - jnp.max(sm, axis=0, keepdims=True)), 0.0)
    l = jnp.sum(p, axis=0, keepdims=True)
    return p * jnp.where(l > 0.0, 1.0 / l, 0.0)


def _nsa_kernel(q_ref, gc_ref, ks_ref, kw_ref, vs_ref, vw_ref, kc_ref, vct_ref, gt_ref, e_ref, ovt_ref,
                o_ref, sc_ref, m_ref, l_ref, acc_ref, *, s_len, nc):
    i = pl.program_id(2)
    qb, cols = Q_BLOCK, NSA_GROUP * Q_BLOCK
    ns = s_len // SLC_BLOCK
    top_k = min(SLC_TOPK, ns)
    kt_w = 2 * LANES
    span = min(WINDOW + Q_BLOCK, s_len)
    lane = lax.broadcasted_iota(jnp.int32, (qb, LANES), 1)

    parts = []
    for r in range(NSA_GROUP):
        blk = q_ref[:, (r // 2) * LANES:(r // 2 + 1) * LANES].astype(F32)
        parts.append(jnp.where(lane // HEAD_DIM == r % 2, blk, 0.0).astype(BF16))
    qs = jnp.concatenate(parts, axis=0)
    tcol = i * qb + lax.broadcasted_iota(jnp.int32, (1, cols), 1) % qb

    s_c = _nt(kc_ref[...], qs)
    cend = lax.broadcasted_iota(jnp.int32, s_c.shape, 0) * CMP_STRIDE + (CMP_BLOCK - 1)
    p_c = _softmax_cols(s_c, cend <= tcol)
    o_c = _mm(vct_ref[...], p_c.astype(BF16))

    ps = p_c[:, 0:qb] + p_c[:, qb:2 * qb] + p_c[:, 2 * qb:3 * qb] + p_c[:, 3 * qb:4 * qb]
    ps_hi = ps.astype(BF16)
    ps_lo = (ps - ps_hi.astype(F32)).astype(BF16)
    imp = _mm(ovt_ref[...], ps_hi) + _mm(ovt_ref[...], ps_lo)

    jrow = lax.broadcasted_iota(jnp.int32, (LANES, qb), 0)
    jt = (i * qb + lax.broadcasted_iota(jnp.int32, (LANES, qb), 1)) // SLC_BLOCK
    valid = (jrow <= jt) & (jrow < ns)
    forced = (jrow == 0) | (valid & (jrow > jt - SLC_LOCAL))
    score = jnp.where(forced, SLC_FORCE_SCORE, jnp.where(valid, imp, -1.0))
    sc_ref[...] = jnp.where(jrow < ns, score, -2.0)
    nsp = -(-ns // 8) * 8
    sc_t = sc_ref[0:nsp, :]
    jsub = jrow[0:nsp, :]
    cnt = jnp.zeros((nsp, qb), jnp.int32)
    for jp in range(ns):
        other = sc_ref[jp:jp + 1, :]
        beats = (other > sc_t) | ((other == sc_t) & (jsub > jp))
        cnt = cnt + beats.astype(jnp.int32)
    sel_t = (cnt < top_k) & (sc_t >= 0.0)
    selb_t = jnp.where(sel_t, 0.0, SEL_NEG)
    if nsp < LANES:
        selb_t = jnp.concatenate([selb_t, jnp.full((LANES - nsp, qb), SEL_NEG, F32)], axis=0)
    selb = selb_t.T.astype(BF16)
    lhs = jnp.concatenate([qs, jnp.concatenate([selb] * NSA_GROUP, axis=0)], axis=1)

    m_ref[...] = jnp.full_like(m_ref, NEG_INF)
    l_ref[...] = jnp.zeros_like(l_ref)
    acc_ref[...] = jnp.zeros_like(acc_ref)

    def sel_tile(kt, diag):
        k0 = pl.multiple_of(kt * kt_w, kt_w)
        rhs = jnp.concatenate([ks_ref[pl.ds(k0, kt_w), :], e_ref[pl.ds(k0, kt_w), :]], axis=1)
        s = _nt(rhs, lhs)
        if diag:
            pos = k0 + lax.broadcasted_iota(jnp.int32, s.shape, 0)
            s = jnp.where(pos <= tcol, s, NEG_INF)
        m_prev = m_ref[...]
        m_new = jnp.maximum(m_prev, jnp.max(s, axis=0, keepdims=True))
        alpha = jnp.exp(m_prev - m_new)
        pr = jnp.exp(s - m_new)
        l_ref[...] = alpha * l_ref[...] + jnp.sum(pr, axis=0, keepdims=True)
        acc_ref[...] = alpha * acc_ref[...] + _mm(vs_ref[0:HEAD_DIM, pl.ds(k0, kt_w)], pr.astype(BF16))
        m_ref[...] = m_new

    n_tiles = (i + 2) // 2

    def body(kt, carry):
        sel_tile(kt, False)
        return carry

    lax.fori_loop(0, n_tiles - 1, body, 0)
    sel_tile(n_tiles - 1, True)
    o_s = acc_ref[...] / l_ref[...]

    start = pl.multiple_of(jnp.maximum(i - WINDOW // qb, 0) * qb, qb)
    s_w = _nt(kw_ref[pl.ds(start, span), :], qs)
    pos = start + lax.broadcasted_iota(jnp.int32, s_w.shape, 0)
    p_w = _softmax_cols(s_w, (pos <= tcol) & (pos > tcol - WINDOW))
    o_w = _mm(vw_ref[0:HEAD_DIM, pl.ds(start, span)], p_w.astype(BF16))

    gts = _sigmoid(gt_ref[...]).T
    heads = []
    for r in range(NSA_GROUP):
        cs = slice(r * qb, (r + 1) * qb)
        heads.append(gts[r:r + 1, :] * o_c[:, cs] + gts[4 + r:5 + r, :] * o_s[:, cs]
                     + gts[8 + r:9 + r, :] * o_w[:, cs])
    y = jnp.concatenate(heads, axis=0).T
    o_ref[...] = (y * _silu(gc_ref[...].astype(F32))).astype(BF16)


def _nsa(z, vt, small, kc, vct, e128, ovt, b, s):
    n = b * s
    nqb = s // Q_BLOCK
    nc = s // CMP_STRIDE
    gw = NSA_GROUP * HEAD_DIM
    cols = NSA_GROUP * Q_BLOCK
    qspec = lambda off: pl.BlockSpec((Q_BLOCK, gw), lambda bi, g, i: (bi * nqb + i, off // gw + g))
    seq = lambda off: pl.BlockSpec((s, LANES), lambda bi, g, i: (bi, off // LANES + g))
    vseq = lambda row: pl.BlockSpec((LANES, s), lambda bi, g, i: (row // LANES + g, bi))
    return pl.pallas_call(
        functools.partial(_nsa_kernel, s_len=s, nc=nc),
        grid=(b, NSA_KV_HEADS, nqb),
        in_specs=[qspec(OFF_QC), qspec(OFF_GC), seq(OFF_KSD), seq(OFF_KWD),
                  vseq(ZT), vseq(ZT + 256),
                  pl.BlockSpec((None, None, nc, LANES), lambda bi, g, i: (bi, g, 0, 0)),
                  pl.BlockSpec((None, None, HEAD_DIM, nc), lambda bi, g, i: (bi, g, 0, 0)),
                  pl.BlockSpec((Q_BLOCK, LANES), lambda bi, g, i: (bi * nqb + i, 1 + g)),
                  pl.BlockSpec((s, LANES), lambda bi, g, i: (0, 0)),
                  pl.BlockSpec((LANES, nc), lambda bi, g, i: (0, 0))],
        out_specs=pl.BlockSpec((Q_BLOCK, gw), lambda bi, g, i: (bi * nqb + i, g)),
        out_shape=jax.ShapeDtypeStruct((n, BRANCH_WIDTH), BF16),
        scratch_shapes=[pltpu.VMEM((LANES, Q_BLOCK), F32), pltpu.VMEM((1, cols), F32),
                        pltpu.VMEM((1, cols), F32), pltpu.VMEM((HEAD_DIM, cols), F32)],
        compiler_params=pltpu.CompilerParams(
            dimension_semantics=("parallel", "parallel", "arbitrary"),
            vmem_limit_bytes=V7X_VMEM_LIMIT),
        name="nsa",
    )(z, z, z, z, vt, vt, kc, vct, small, e128, ovt)


def _merge_kernel(ya_ref, yb_ref, yc_ref, yd_ref, mg_ref, x_ref, wb_ref, wo_ref, o_ref):
    merged = None
    for nb, y_ref in enumerate((ya_ref, yb_ref, yc_ref, yd_ref)):
        gate = _sigmoid(mg_ref[:, nb * D_MODEL:(nb + 1) * D_MODEL].astype(F32))
        term = gate * _mm(y_ref[...], wb_ref[nb])
        merged = term if merged is None else merged + term
    o_ref[...] = x_ref[...] + _mm(merged.astype(BF16), wo_ref[...])


def _merge(ys, z, x2, wb, wo, tm):
    n = x2.shape[0]
    W = BRANCH_WIDTH
    yspec = pl.BlockSpec((tm, W), lambda i: (i, 0))
    return pl.pallas_call(
        _merge_kernel,
        grid=(n // tm,),
        in_specs=[yspec, yspec, yspec, yspec,
                  pl.BlockSpec((tm, N_BRANCHES * D_MODEL), lambda i: (i, 0)),
                  pl.BlockSpec((tm, D_MODEL), lambda i: (i, 0)),
                  pl.BlockSpec((N_BRANCHES, W, D_MODEL), lambda i: (0, 0, 0)),
                  pl.BlockSpec((D_MODEL, D_MODEL), lambda i: (0, 0))],
        out_specs=pl.BlockSpec((tm, D_MODEL), lambda i: (i, 0)),
        out_shape=jax.ShapeDtypeStruct((n, D_MODEL), F32),
        compiler_params=pltpu.CompilerParams(
            dimension_semantics=("parallel",), vmem_limit_bytes=V7X_VMEM_LIMIT),
        name="merge",
    )(*ys, z, x2, wb, wo)


def _block_diag(blocks):
    nb, a, b_ = blocks.shape
    eye = jnp.eye(nb, dtype=blocks.dtype)
    return jnp.einsum('nab,nm->namb', blocks, eye).reshape(nb * a, nb * b_)


def _mean_matrix(width):
    idx = np.arange(width) // HEAD_DIM
    return jnp.asarray((idx[:, None] == idx[None, :]).astype(np.float32) / HEAD_DIM, BF16)


def _layout_w_in(w_in, qn_a, kn_a, qn_c, kn_c):
    sizes = _split_sizes()
    offs = np.concatenate([[0], np.cumsum(sizes)])
    seg = lambda k: w_in[:, int(offs[k]):int(offs[k + 1])]
    (qa, ka, va, fa, ga, xb, gb, qc, kcc, vcc, ksc, vsc, kwc, vwc, gate_c, gc, ud, vd, gd, mg) = [
        seg(k) for k in range(len(sizes))]
    dup = lambda w: jnp.concatenate([w[:, :HEAD_DIM], w[:, :HEAD_DIM], w[:, HEAD_DIM:], w[:, HEAD_DIM:]], axis=1)
    pad = jnp.zeros((D_MODEL, 256), w_in.dtype)
    w_z = jnp.concatenate([mg, qa, ka, va, ga, xb, gb, qc, gc, ud, vd, gd,
                           dup(ksc), dup(kwc), dup(vsc), dup(vwc), kcc, vcc, pad], axis=1).astype(BF16)
    scale = 1.0 / math.sqrt(HEAD_DIM)
    gain = jnp.ones((ZW,), F32)
    gain = gain.at[OFF_QA:OFF_QA + ZT].set(jnp.tile(qn_a * scale, FOX_HEADS))
    gain = gain.at[OFF_KA:OFF_KA + ZT].set(jnp.tile(kn_a, FOX_HEADS))
    gain = gain.at[OFF_QC:OFF_QC + ZT].set(jnp.tile(qn_c * scale, NSA_HEADS))
    gain = gain.at[OFF_KSD:OFF_KSD + ZT].set(jnp.tile(kn_c, ZT // HEAD_DIM))
    gate3 = gate_c.reshape(D_MODEL, 3, NSA_KV_HEADS, NSA_GROUP)
    zpad = lambda w: jnp.pad(w, ((0, 0), (0, LANES - w.shape[1])))
    w_small = jnp.concatenate(
        [zpad(fa)] + [zpad(gate3[:, :, g, :].reshape(D_MODEL, 3 * NSA_GROUP)) for g in range(NSA_KV_HEADS)],
        axis=1).astype(BF16)
    return w_z, gain.reshape(1, ZW), w_small


def _layout_compress(cmp_pos, wk1, wk2, wv1, wv2, kn_c):
    half = CMP_BLOCK // 2
    eye4 = jnp.eye(4, dtype=F32)

    def w1_half(lo):
        k = wk1.reshape(CMP_BLOCK, HEAD_DIM, CMP_HIDDEN)[lo:lo + half]
        v = wv1.reshape(CMP_BLOCK, HEAD_DIM, CMP_HIDDEN)[lo:lo + half]
        stack = jnp.stack([k, k, v, v])
        return jnp.einsum('spdn,st->psdtn', stack, eye4).reshape(half * 4 * HEAD_DIM, 4 * CMP_HIDDEN).astype(BF16)

    def pos_half(lo):
        p = cmp_pos[lo:lo + half]
        return jnp.broadcast_to(p[:, None, :], (half, 4, HEAD_DIM)).reshape(1, half * 4 * HEAD_DIM)

    w2 = jnp.zeros((4, CMP_HIDDEN, 4, 2, HEAD_DIM), F32)
    for slot, w in enumerate((wk2, wk2, wv2, wv2)):
        w2 = w2.at[slot, :, slot, :, :].set(jnp.broadcast_to(w[:, None, :], (CMP_HIDDEN, 2, HEAD_DIM)))
    w2 = w2.reshape(4 * CMP_HIDDEN, ZT).astype(BF16)
    gain = jnp.tile(kn_c, ZT // HEAD_DIM).reshape(1, ZT)
    return pos_half(0), pos_half(half), w1_half(0), w1_half(half), w2, gain


@functools.lru_cache(maxsize=None)
def _nsa_consts(s):
    nc = s // CMP_STRIDE
    ns = s // SLC_BLOCK
    e = np.zeros((s, LANES), np.float32)
    e[np.arange(s), np.arange(s) // SLC_BLOCK] = 1.0
    c0 = np.arange(nc) * CMP_STRIDE
    s0 = np.arange(ns) * SLC_BLOCK
    ovl = np.minimum(c0[:, None] + CMP_BLOCK, s0[None, :] + SLC_BLOCK) - np.maximum(c0[:, None], s0[None, :])
    ovt = np.zeros((LANES, nc), np.float32)
    ovt[:ns, :] = (np.clip(ovl, 0, None) / CMP_BLOCK).T
    ovt[:, nc - 1] = 0.0
    return e, ovt


def _tile(s, pref):
    t = pref
    while s % t:
        t //= 2
    return t


def _branches(x2, b, s, norm_g, w_in, b_forget, qn_a, kn_a, conv_w, conv_b, w_rg_a, b_rg_a, w_rg_x,
              b_rg_x, lru_lambda, qn_c, kn_c, cmp_pos, wk1, wk2, wv1, wv2, ln_v_g, w_spatial,
              b_spatial):
    n = b * s
    W = BRANCH_WIDTH
    w_z, gain_row, w_small = _layout_w_in(w_in, qn_a, kn_a, qn_c, kn_c)
    bd = _mean_matrix(ZT)
    z, small = _inproj(x2, norm_g.reshape(1, D_MODEL), w_z, gain_row, bd, w_small, _tile(n, 1024))
    vt = _vtrans(z, _tile(s, 512))

    bf_row = jnp.pad(b_forget, (0, LANES - FOX_HEADS)).reshape(1, LANES)
    caq, cak = _cprep(small, bf_row, b, s, _tile(s, 512))
    ys_a = _fox(z, vt, caq, cak, b, s, _tile(s, 512))

    ys_b = _lru(z, conv_w, conv_b.reshape(1, W), _block_diag(w_rg_a).astype(BF16), b_rg_a.reshape(1, W),
                _block_diag(w_rg_x).astype(BF16), b_rg_x.reshape(1, W), lru_lambda.reshape(1, W),
                b, s, _tile(s, 256))

    nc = s // CMP_STRIDE
    t16 = z[:, OFF_CMP:OFF_CMP + 4 * HEAD_DIM].reshape(b, nc, CMP_STRIDE * 4 * HEAD_DIM)
    ptop, pbot, w1t, w1b, w2c, cgain = _layout_compress(cmp_pos, wk1, wk2, wv1, wv2, kn_c)
    kc, vct = _compress(t16, ptop, pbot, w1t, w1b, w2c, bd, cgain, b, nc)
    e_np, ovt_np = _nsa_consts(s)
    ys_c = _nsa(z, vt, small, kc, vct, jnp.asarray(e_np, BF16), jnp.asarray(ovt_np, BF16), b, s)

    causal = jnp.tril(jnp.ones((SG_CHUNK, SG_CHUNK), F32))
    wm = (w_spatial * causal).astype(BF16)
    bs_full = jnp.broadcast_to(b_spatial.T[:, :, None], (SG_CHUNK, SG_GROUPS, HEAD_DIM)).reshape(SG_CHUNK, W)
    ys_d = _sgu(z, ln_v_g.reshape(1, W), wm, bs_full, n, _tile(s, 512))
    return (ys_a, ys_b, ys_c, ys_d), z


def _layer(x2, b, s, *params):
    ys, z = _branches(x2, b, s, *params[:-2])
    w_branch, w_out = params[-2:]
    return _merge(ys, z, x2, w_branch.astype(BF16), w_out.astype(BF16), _tile(b * s, 512))


def kernel(x, norm_g, w_in, b_forget, qn_a, kn_a, conv_w, conv_b, w_rg_a, b_rg_a, w_rg_x, b_rg_x,
           lru_lambda, qn_c, kn_c, cmp_pos, cmp_k_w1, cmp_k_w2, cmp_v_w1, cmp_v_w2, ln_v_g,
           w_spatial, b_spatial, w_branch, w_out):
    b, s, d = x.shape
    x2 = x.reshape(b * s, d)
    params = (norm_g, w_in, b_forget, qn_a, kn_a, conv_w, conv_b, w_rg_a, b_rg_a, w_rg_x, b_rg_x,
              lru_lambda, qn_c, kn_c, cmp_pos, cmp_k_w1, cmp_k_w2, cmp_v_w1, cmp_v_w2, ln_v_g,
              w_spatial, b_spatial, w_branch, w_out)
    for layer in range(norm_g.shape[0]):
        x2 = _layer(x2, b, s, *[p[layer] for p in params])
    return x2.reshape(b, s, d)
```

```python
import functools
import math

import numpy as np
import jax
import jax.numpy as jnp
from jax import lax
from jax.experimental import pallas as pl
from jax.experimental.pallas import tpu as pltpu

F32 = jnp.float32
BF16 = jnp.bfloat16

D_MODEL = 1024
HEAD_DIM = 64
BRANCH_WIDTH = 512
N_BRANCHES = 4
FOX_HEADS = 8
LRU_BLOCKS = 8
CONV_WIDTH = 4
LRU_C = 8.0
NSA_HEADS = 8
NSA_KV_HEADS = 2
NSA_GROUP = 4
CMP_BLOCK = 32
CMP_STRIDE = 16
CMP_HIDDEN = 128
SLC_BLOCK = 64
SLC_TOPK = 16
SLC_LOCAL = 2
SLC_FORCE_SCORE = 1e6
WINDOW = 512
Q_BLOCK = 128
SG_CHUNK = 128
SG_GROUPS = 8
NORM_EPS = 1e-6
NEG_INF = -1e30
SEL_NEG = -32768.0

LANES = 128
V7X_VMEM_LIMIT = 56 * 1024 * 1024
SEL_TILE = 512
LOG2E = 1.4426950408889634

ZT = 512
OFF_MG = 0
OFF_QA = 8 * ZT
OFF_KA = 9 * ZT
OFF_QC = 10 * ZT
OFF_KSD = 11 * ZT
OFF_KWD = 11 * ZT + 256
OFF_VA = 12 * ZT
OFF_VSD = 13 * ZT
OFF_VWD = 13 * ZT + 256
OFF_GA = 14 * ZT
OFF_XB = 15 * ZT
OFF_GB = 16 * ZT
OFF_GC = 17 * ZT
OFF_UD = 18 * ZT
OFF_VD = 19 * ZT
OFF_GD = 20 * ZT
OFF_CMP = 21 * ZT
ZW = 22 * ZT
IN_TN = 2 * ZT
NORM_TILES = (OFF_QA // IN_TN, OFF_QC // IN_TN)
SMALL_W = 384
VT_TILES = (OFF_VA // ZT, OFF_VSD // ZT)


def _nt(a, b):
    return lax.dot_general(a, b, (((1,), (1,)), ((), ())), preferred_element_type=F32)


def _mm(a, b):
    return jnp.dot(a, b, preferred_element_type=F32)


def _sigmoid(x):
    return 1.0 / (1.0 + jnp.exp(-x))


def _silu(x):
    return x * _sigmoid(x)


def _split_sizes():
    W, kv = BRANCH_WIDTH, NSA_KV_HEADS * HEAD_DIM
    return (W, W, W, FOX_HEADS, W, W, W, W, kv, kv, kv, kv, kv, kv, 3 * NSA_HEADS, W,
            W, W, W, N_BRANCHES * D_MODEL)


def _inproj_kernel(x_ref, g_ref, w_ref, gain_ref, bd_ref, ws_ref, z_ref, small_ref, xn_ref):
    j = pl.program_id(1)

    @pl.when(j == 0)
    def _():
        x = x_ref[...]
        ms = jnp.mean(x * x, axis=-1, keepdims=True)
        xn = ((x * lax.rsqrt(ms + NORM_EPS)) * g_ref[...]).astype(BF16)
        xn_ref[...] = xn
        small_ref[...] = _mm(xn, ws_ref[...])

    acc = _mm(xn_ref[...], w_ref[...])
    is_norm = functools.reduce(jnp.logical_or, [j == t for t in NORM_TILES])

    @pl.when(is_norm)
    def _():
        sq = (acc * acc).astype(BF16)
        ms = jnp.concatenate([_mm(sq[:, c:c + ZT], bd_ref[...]) for c in range(0, IN_TN, ZT)], axis=1)
        z_ref[...] = (acc * lax.rsqrt(ms + NORM_EPS) * gain_ref[...]).astype(BF16)

    @pl.when(jnp.logical_not(is_norm))
    def _():
        z_ref[...] = acc.astype(BF16)


def _inproj(x2, norm_g, w_z, gain_row, bd, w_small, tm):
    n = x2.shape[0]
    grid = (n // tm, ZW // IN_TN)
    return pl.pallas_call(
        _inproj_kernel,
        grid=grid,
        in_specs=[
            pl.BlockSpec((tm, D_MODEL), lambda i, j: (i, 0)),
            pl.BlockSpec((1, D_MODEL), lambda i, j: (0, 0)),
            pl.BlockSpec((D_MODEL, IN_TN), lambda i, j: (0, j)),
            pl.BlockSpec((1, IN_TN), lambda i, j: (0, j)),
            pl.BlockSpec((ZT, ZT), lambda i, j: (0, 0)),
            pl.BlockSpec((D_MODEL, SMALL_W), lambda i, j: (0, 0)),
        ],
        out_specs=[
            pl.BlockSpec((tm, IN_TN), lambda i, j: (i, j)),
            pl.BlockSpec((tm, SMALL_W), lambda i, j: (i, 0)),
        ],
        out_shape=[
            jax.ShapeDtypeStruct((n, ZW), BF16),
            jax.ShapeDtypeStruct((n, SMALL_W), F32),
        ],
        scratch_shapes=[pltpu.VMEM((tm, D_MODEL), BF16)],
        compiler_params=pltpu.CompilerParams(
            dimension_semantics=("parallel", "arbitrary"), vmem_limit_bytes=V7X_VMEM_LIMIT),
        name="inproj",
    )(x2, norm_g, w_z, gain_row, bd, w_small)


def _vtrans_kernel(z_ref, o_ref):
    o_ref[...] = z_ref[...].astype(F32).T.astype(BF16)


def _vtrans(z, tm):
    n = z.shape[0]
    step = VT_TILES[1] - VT_TILES[0]
    return pl.pallas_call(
        _vtrans_kernel,
        grid=(n // tm, len(VT_TILES)),
        in_specs=[pl.BlockSpec((tm, ZT), lambda i, j: (i, VT_TILES[0] + step * j))],
        out_specs=pl.BlockSpec((ZT, tm), lambda i, j: (j, i)),
        out_shape=jax.ShapeDtypeStruct((len(VT_TILES) * ZT, n), BF16),
        compiler_params=pltpu.CompilerParams(dimension_semantics=("parallel", "parallel")),
        name="vtrans",
    )(z)


def _cprep_kernel(fa_ref, bf_ref, pq_ref, pk_ref, oq_ref, ok_ref, caq_ref, cak_ref, carry_ref,
                  *, tt):
    @pl.when(pl.program_id(1) == 0)
    def _():
        carry_ref[...] = jnp.zeros_like(carry_ref)

    x = fa_ref[...] + bf_ref[...]
    c = jnp.minimum(x, 0.0) - jnp.log(1.0 + jnp.exp(-jnp.abs(x)))
    row = lax.broadcasted_iota(jnp.int32, c.shape, 0)
    d = 1
    while d < tt:
        c = c + jnp.where(row >= d, pltpu.roll(c, d, axis=0), 0.0)
        d *= 2
    c = c + carry_ref[...]
    carry_ref[...] = c[tt - 1:tt, :]
    c = c * LOG2E
    hi = c.astype(BF16)
    r1 = c - hi.astype(F32)
    mid = r1.astype(BF16)
    lo = (r1 - mid.astype(F32)).astype(BF16)
    parts = jnp.concatenate([hi, mid, lo], axis=1)
    caq_ref[...] = (_mm(parts, pq_ref[...]) + oq_ref[...]).astype(BF16)
    cak_ref[...] = (_mm(parts, pk_ref[...]) + ok_ref[...]).astype(BF16)


def _aug_lane(h):
    return (h // 2) * LANES + (h % 2) * 8


@functools.lru_cache(maxsize=None)
def _cprep_consts():
    pq = np.zeros((3 * LANES, 4 * LANES), np.float32)
    pk = np.zeros((3 * LANES, 4 * LANES), np.float32)
    oq = np.zeros((1, 4 * LANES), np.float32)
    ok = np.zeros((1, 4 * LANES), np.float32)
    for h in range(FOX_HEADS):
        base = _aug_lane(h)
        for part in range(3):
            pq[part * LANES + h, base + part] = 1.0
            pk[part * LANES + h, base + 3 + part] = -1.0
            oq[0, base + 3 + part] = 1.0
            ok[0, base + part] = 1.0
    return pq, pk, oq, ok


def _cprep(small, bf_row, b, s, tt):
    n = b * s
    nt = s // tt
    pq, pk, oq, ok = _cprep_consts()
    full = lambda shape: pl.BlockSpec(shape, lambda bi, ti: (0, 0))
    return pl.pallas_call(
        functools.partial(_cprep_kernel, tt=tt),
        grid=(b, nt),
        in_specs=[
            pl.BlockSpec((tt, LANES), lambda bi, ti: (bi * nt + ti, 0)),
            full((1, LANES)),
            full((3 * LANES, 4 * LANES)),
            full((3 * LANES, 4 * LANES)),
            full((1, 4 * LANES)),
            full((1, 4 * LANES)),
        ],
        out_specs=[
            pl.BlockSpec((tt, 4 * LANES), lambda bi, ti: (bi * nt + ti, 0)),
            pl.BlockSpec((tt, 4 * LANES), lambda bi, ti: (bi * nt + ti, 0)),
        ],
        out_shape=[jax.ShapeDtypeStruct((n, 4 * LANES), BF16)] * 2,
        scratch_shapes=[pltpu.VMEM((1, LANES), F32)],
        compiler_params=pltpu.CompilerParams(dimension_semantics=("parallel", "arbitrary")),
        name="cprep",
    )(small, bf_row, jnp.asarray(pq, BF16), jnp.asarray(pk, BF16), jnp.asarray(oq), jnp.asarray(ok))


def _fox_kernel(qi_ref, ki_ref, q_ref, caq_ref, k_ref, cak_ref, vt_ref, ga_ref, o_ref,
                lhs_ref, rhs_ref, m_ref, l_ref, acc_ref, *, t):
    pair = pl.program_id(1)
    qi = qi_ref[pair]
    ki = ki_ref[pair]
    n_pairs = FOX_HEADS // 2

    @pl.when(ki == 0)
    def _():
        lane = lax.broadcasted_iota(jnp.int32, (t, 2 * LANES), 1)
        head_of_lane = jnp.where(lane < LANES, lane // HEAD_DIM, (lane - LANES) // 8)
        for hp in range(n_pairs):
            cols = slice(hp * LANES, (hp + 1) * LANES)
            lhs = jnp.concatenate([q_ref[:, cols], caq_ref[:, cols]], axis=1).astype(F32)
            for hh in range(2):
                lhs_ref[2 * hp + hh] = jnp.where(head_of_lane == hh, lhs, 0.0).astype(BF16)
        m_ref[...] = jnp.full_like(m_ref, NEG_INF)
        l_ref[...] = jnp.zeros_like(l_ref)
        acc_ref[...] = jnp.zeros_like(acc_ref)

    def step(diag):
        for hp in range(n_pairs):
            cols = slice(hp * LANES, (hp + 1) * LANES)
            rhs_ref[hp] = jnp.concatenate([k_ref[:, cols], cak_ref[:, cols]], axis=1)

        s_next = _nt(rhs_ref[0], lhs_ref[0])
        for h in range(FOX_HEADS):
            s = s_next
            if h + 1 < FOX_HEADS:
                s_next = _nt(rhs_ref[(h + 1) // 2], lhs_ref[h + 1])
            if diag:
                key = lax.broadcasted_iota(jnp.int32, s.shape, 0)
                qry = lax.broadcasted_iota(jnp.int32, s.shape, 1)
                s = jnp.where(key <= qry, s, NEG_INF)
            m_prev = m_ref[h]
            m_new = jnp.maximum(m_prev, jnp.max(s, axis=0, keepdims=True))
            alpha = jnp.exp2(m_prev - m_new)
            p = jnp.exp2(s - m_new)
            l_ref[h] = alpha * l_ref[h] + jnp.sum(p, axis=0, keepdims=True)
            vt = vt_ref[h * HEAD_DIM:(h + 1) * HEAD_DIM, :]
            acc_ref[h] = alpha * acc_ref[h] + _mm(vt, p.astype(BF16))
            m_ref[h] = m_new

    @pl.when(ki < qi)
    def _():
        step(False)

    @pl.when(ki == qi)
    def _():
        step(True)
        for hp in range(n_pairs):
            cols = slice(hp * LANES, (hp + 1) * LANES)
            o_t = jnp.concatenate([acc_ref[2 * hp] / l_ref[2 * hp],
                                   acc_ref[2 * hp + 1] / l_ref[2 * hp + 1]], axis=0)
            o_ref[:, cols] = (o_t.T * _silu(ga_ref[:, cols].astype(F32))).astype(BF16)


def _fox(z, vt, caq, cak, b, s, t):
    n = b * s
    nq = s // t
    W = BRANCH_WIDTH
    qi_of = np.concatenate([np.full(q + 1, q) for q in range(nq)]).astype(np.int32)
    ki_of = np.concatenate([np.arange(q + 1) for q in range(nq)]).astype(np.int32)
    qmap = lambda off: (lambda bi, p, qi, ki: (bi * nq + qi[p], off // W))
    kmap = lambda off: (lambda bi, p, qi, ki: (bi * nq + ki[p], off // W))
    grid_spec = pltpu.PrefetchScalarGridSpec(
        num_scalar_prefetch=2,
        grid=(b, len(qi_of)),
        in_specs=[
            pl.BlockSpec((t, W), qmap(OFF_QA)),
            pl.BlockSpec((t, W), qmap(0)),
            pl.BlockSpec((t, W), kmap(OFF_KA)),
            pl.BlockSpec((t, W), kmap(0)),
            pl.BlockSpec((W, t), lambda bi, p, qi, ki: (0, bi * nq + ki[p])),
            pl.BlockSpec((t, W), qmap(OFF_GA)),
        ],
        out_specs=pl.BlockSpec((t, W), qmap(0)),
        scratch_shapes=[
            pltpu.VMEM((FOX_HEADS, t, 2 * LANES), BF16),
            pltpu.VMEM((FOX_HEADS // 2, t, 2 * LANES), BF16),
            pltpu.VMEM((FOX_HEADS, 1, t), F32),
            pltpu.VMEM((FOX_HEADS, 1, t), F32),
            pltpu.VMEM((FOX_HEADS, HEAD_DIM, t), F32),
        ],
    )
    return pl.pallas_call(
        functools.partial(_fox_kernel, t=t),
        grid_spec=grid_spec,
        out_shape=jax.ShapeDtypeStruct((n, W), BF16),
        compiler_params=pltpu.CompilerParams(
            dimension_semantics=("parallel", "arbitrary"), vmem_limit_bytes=V7X_VMEM_LIMIT),
        name="fox",
    )(jnp.asarray(qi_of), jnp.asarray(ki_of), z, caq, z, cak, vt, z)


def _lru_kernel(xb_ref, gb_ref, cw_ref, cb_ref, wa_ref, ba_ref, wx_ref, bx_ref, lam_ref, o_ref,
                xpad_ref, h_ref, *, tt):
    @pl.when(pl.program_id(1) == 0)
    def _():
        xpad_ref[0:8, :] = jnp.zeros((8, BRANCH_WIDTH), F32)
        h_ref[...] = jnp.zeros_like(h_ref)

    x = xb_ref[...].astype(F32)
    xpad_ref[8:tt + 8, :] = x
    xc = cb_ref[...] + cw_ref[3:4, :] * x
    for j in range(CONV_WIDTH - 1):
        sh = CONV_WIDTH - 1 - j
        xc = xc + cw_ref[j:j + 1, :] * xpad_ref[8 - sh:8 - sh + tt, :]
    xpad_ref[0:8, :] = x[tt - 8:tt, :]

    xcb = xc.astype(BF16)
    r = _sigmoid(_mm(xcb, wa_ref[...]) + ba_ref[...])
    ig = _sigmoid(_mm(xcb, wx_ref[...]) + bx_ref[...])
    nl = -lam_ref[...]
    softplus = jnp.maximum(nl, 0.0) + jnp.log(1.0 + jnp.exp(-jnp.abs(nl)))
    log_a = (-LRU_C) * r * softplus
    a = jnp.exp(log_a)
    bb = jnp.sqrt(1.0 - jnp.exp(2.0 * log_a)) * (ig * xc)

    row = lax.broadcasted_iota(jnp.int32, a.shape, 0)
    d = 1
    while d < tt:
        keep = row >= d
        bb = bb + a * jnp.where(keep, pltpu.roll(bb, d, axis=0), 0.0)
        a = a * jnp.where(keep, pltpu.roll(a, d, axis=0), 1.0)
        d *= 2
    h = bb + a * h_ref[...]
    h_ref[...] = h[tt - 1:tt, :]
    o_ref[...] = (h * _silu(gb_ref[...].astype(F32))).astype(BF16)


def _lru(z, conv_w, conv_b, wa_bd, b_a, wx_bd, b_x, lam, b, s, tt):
    n = b * s
    nt = s // tt
    W = BRANCH_WIDTH
    full = lambda shape: pl.BlockSpec(shape, lambda bi, ti: (0, 0))
    zb = lambda off: pl.BlockSpec((tt, W), lambda bi, ti: (bi * nt + ti, off // W))
    return pl.pallas_call(
        functools.partial(_lru_kernel, tt=tt),
        grid=(b, nt),
        in_specs=[zb(OFF_XB), zb(OFF_GB), full((CONV_WIDTH, W)), full((1, W)), full((W, W)),
                  full((1, W)), full((W, W)), full((1, W)), full((1, W))],
        out_specs=pl.BlockSpec((tt, W), lambda bi, ti: (bi * nt + ti, 0)),
        out_shape=jax.ShapeDtypeStruct((n, W), BF16),
        scratch_shapes=[pltpu.VMEM((tt + 8, W), F32), pltpu.VMEM((1, W), F32)],
        compiler_params=pltpu.CompilerParams(dimension_semantics=("parallel", "arbitrary")),
        name="lru",
    )(z, z, conv_w, conv_b, wa_bd, b_a, wx_bd, b_x, lam)


def _gelu(x):
    return 0.5 * x * (1.0 + lax.erf(x * (1.0 / math.sqrt(2.0))))


def _sgu_kernel(u_ref, v_ref, g_ref, lng_ref, wm_ref, bs_ref, o_ref, *, tr):
    u = _gelu(u_ref[...].astype(F32))
    v = _gelu(v_ref[...].astype(F32))
    mu = jnp.mean(v, axis=-1, keepdims=True)
    vc = v - mu
    vn = (vc * lax.rsqrt(jnp.mean(vc * vc, axis=-1, keepdims=True) + NORM_EPS) * lng_ref[...]).astype(BF16)
    lane = lax.broadcasted_iota(jnp.int32, (SG_CHUNK, LANES), 1)
    chunks = []
    for c in range(tr // SG_CHUNK):
        vch = vn[c * SG_CHUNK:(c + 1) * SG_CHUNK, :]
        pairs = []
        for p in range(SG_GROUPS // 2):
            vp = vch[:, p * LANES:(p + 1) * LANES]
            pairs.append(jnp.where(lane < HEAD_DIM, _mm(wm_ref[2 * p], vp), _mm(wm_ref[2 * p + 1], vp)))
        chunks.append(jnp.concatenate(pairs, axis=1) + bs_ref[...])
    mixed = jnp.concatenate(chunks, axis=0)
    o_ref[...] = (u * mixed * _silu(g_ref[...].astype(F32))).astype(BF16)


def _sgu(z, ln_g, wm, bs_full, n, tr):
    W = BRANCH_WIDTH
    zb = lambda off: pl.BlockSpec((tr, W), lambda i: (i, off // W))
    return pl.pallas_call(
        functools.partial(_sgu_kernel, tr=tr),
        grid=(n // tr,),
        in_specs=[zb(OFF_UD), zb(OFF_VD), zb(OFF_GD),
                  pl.BlockSpec((1, W), lambda i: (0, 0)),
                  pl.BlockSpec((SG_GROUPS, SG_CHUNK, SG_CHUNK), lambda i: (0, 0, 0)),
                  pl.BlockSpec((SG_CHUNK, W), lambda i: (0, 0))],
        out_specs=pl.BlockSpec((tr, W), lambda i: (i, 0)),
        out_shape=jax.ShapeDtypeStruct((n, W), BF16),
        compiler_params=pltpu.CompilerParams(dimension_semantics=("parallel",)),
        name="sgu",
    )(z, z, z, ln_g, wm, bs_full)


def _compress_kernel(t16_ref, ptop_ref, pbot_ref, w1t_ref, w1b_ref, w2_ref, bd_ref, gain_ref,
                     kc_ref, vct_ref, *, nc):
    t16 = t16_ref[...].astype(F32)
    a = _mm((t16 + ptop_ref[...]).astype(BF16), w1t_ref[...])
    bm = _mm((t16 + pbot_ref[...]).astype(BF16), w1b_ref[...])
    h = a + pltpu.roll(bm, nc - 1, axis=0)
    out = _mm(_silu(h).astype(BF16), w2_ref[...])
    ms = _mm((out * out).astype(BF16), bd_ref[...])
    kn = out * lax.rsqrt(ms + NORM_EPS) * gain_ref[...]
    out_t = out.T
    for g in range(NSA_KV_HEADS):
        kc_ref[g] = kn[:, g * LANES:(g + 1) * LANES].astype(BF16)
        vct_ref[g] = out_t[(2 + g) * LANES:(2 + g) * LANES + HEAD_DIM, :].astype(BF16)


def _compress(t16, ptop, pbot, w1t, w1b, w2, bd, gain, b, nc):
    kw = 16 * 4 * HEAD_DIM
    full = lambda shape: pl.BlockSpec(shape, lambda bi: tuple(0 for _ in shape))
    return pl.pallas_call(
        functools.partial(_compress_kernel, nc=nc),
        grid=(b,),
        in_specs=[pl.BlockSpec((None, nc, kw), lambda bi: (bi, 0, 0)),
                  full((1, kw)), full((1, kw)), full((kw, 4 * CMP_HIDDEN)), full((kw, 4 * CMP_HIDDEN)),
                  full((4 * CMP_HIDDEN, ZT)), full((ZT, ZT)), full((1, ZT))],
        out_specs=[pl.BlockSpec((None, NSA_KV_HEADS, nc, LANES), lambda bi: (bi, 0, 0, 0)),
                   pl.BlockSpec((None, NSA_KV_HEADS, HEAD_DIM, nc), lambda bi: (bi, 0, 0, 0))],
        out_shape=[jax.ShapeDtypeStruct((b, NSA_KV_HEADS, nc, LANES), BF16),
                   jax.ShapeDtypeStruct((b, NSA_KV_HEADS, HEAD_DIM, nc), BF16)],
        compiler_params=pltpu.CompilerParams(
            dimension_semantics=("parallel",), vmem_limit_bytes=V7X_VMEM_LIMIT),
        name="compress",
    )(t16, ptop, pbot, w1t, w1b, w2, bd, gain)


def _softmax_cols(s, mask):
    sm = jnp.where(mask, s, NEG_INF)
    m = jnp.max(sm, axis=0, keepdims=True)
    p = jnp.exp2(sm - jnp.where(m > 0.5 * NEG_INF, m, 0.0))
    l = jnp.sum(p, axis=0, keepdims=True)
    return p, jnp.where(l > 0.0, 1.0 / l, 0.0)


def _nsa_kernel(q_ref, gc_ref, ks_ref, kw_ref, vs_ref, vw_ref, kc_ref, vct_ref, gt_ref, e_ref, ovt_ref,
                o_ref, kse_ref, sc_ref, m_ref, l_ref, acc_ref, *, s_len, nc):
    i = pl.program_id(2)
    qb, cols = Q_BLOCK, NSA_GROUP * Q_BLOCK
    ns = s_len // SLC_BLOCK
    top_k = min(SLC_TOPK, ns)
    kt_w = min(SEL_TILE, s_len)
    span = min(WINDOW + Q_BLOCK, s_len)
    lane = lax.broadcasted_iota(jnp.int32, (qb, LANES), 1)

    @pl.when(i == 0)
    def _():
        kse_ref[:, 0:LANES] = ks_ref[...]
        kse_ref[:, LANES:2 * LANES] = e_ref[...]

    parts = []
    for r in range(NSA_GROUP):
        blk = q_ref[:, (r // 2) * LANES:(r // 2 + 1) * LANES].astype(F32)
        parts.append(jnp.where(lane // HEAD_DIM == r % 2, blk, 0.0).astype(BF16))
    qs = jnp.concatenate(parts, axis=0)
    tcol = i * qb + lax.broadcasted_iota(jnp.int32, (1, cols), 1) % qb

    s_c = _nt(kc_ref[...], qs)
    start = pl.multiple_of(jnp.maximum(i - WINDOW // qb, 0) * qb, qb)
    s_w = _nt(kw_ref[pl.ds(start, span), :], qs)

    cend = lax.broadcasted_iota(jnp.int32, s_c.shape, 0) * CMP_STRIDE + (CMP_BLOCK - 1)
    p_c, inv_c = _softmax_cols(s_c, cend <= tcol)
    p_c = p_c * inv_c
    o_c = _mm(vct_ref[...], p_c.astype(BF16))

    ps = p_c[:, 0:qb] + p_c[:, qb:2 * qb] + p_c[:, 2 * qb:3 * qb] + p_c[:, 3 * qb:4 * qb]
    ps_hi = ps.astype(BF16)
    ps_lo = (ps - ps_hi.astype(F32)).astype(BF16)
    imp = _mm(ovt_ref[...], ps_hi) + _mm(ovt_ref[...], ps_lo)

    jrow = lax.broadcasted_iota(jnp.int32, (LANES, qb), 0)
    jt = (i * qb + lax.broadcasted_iota(jnp.int32, (LANES, qb), 1)) // SLC_BLOCK
    valid = (jrow <= jt) & (jrow < ns)
    forced = (jrow == 0) | (valid & (jrow > jt - SLC_LOCAL))
    score = jnp.where(forced, SLC_FORCE_SCORE, jnp.where(valid, imp, -1.0))
    sc_ref[...] = jnp.where(jrow < ns, score, -2.0)
    slabs = []
    for k in range(-(-ns // 8)):
        mine = sc_ref[8 * k:8 * k + 8, :]
        sub = 8 * k + lax.broadcasted_iota(jnp.int32, (8, qb), 0)
        cnt = jnp.zeros((8, qb), F32)
        for jp in range(ns):
            other = sc_ref[jp:jp + 1, :]
            if jp < 8 * k:
                beats = other >= mine
            elif jp >= 8 * k + 8:
                beats = other > mine
            else:
                beats = (other > mine) | ((other == mine) & (sub > jp))
            cnt = cnt + jnp.where(beats, 1.0, 0.0)
        slabs.append(jnp.where((cnt < top_k) & (mine >= 0.0), 0.0, SEL_NEG))
    if len(slabs) * 8 < LANES:
        slabs.append(jnp.full((LANES - len(slabs) * 8, qb), SEL_NEG, F32))
    selb = jnp.concatenate(slabs, axis=0).T.astype(BF16)
    lhs = jnp.concatenate([qs, jnp.concatenate([selb] * NSA_GROUP, axis=0)], axis=1)

    def sel_scores(kt):
        k0 = pl.multiple_of(kt * kt_w, kt_w)
        return _nt(kse_ref[pl.ds(k0, kt_w), :], lhs)

    def sel_update(kt, s):
        k0 = pl.multiple_of(kt * kt_w, kt_w)
        m_prev = m_ref[...]
        m_new = jnp.maximum(m_prev, jnp.max(s, axis=0, keepdims=True))
        alpha = jnp.exp2(m_prev - m_new)
        pr = jnp.exp2(s - m_new)
        l_ref[...] = alpha * l_ref[...] + jnp.sum(pr, axis=0, keepdims=True)
        acc_ref[...] = alpha * acc_ref[...] + _mm(vs_ref[0:HEAD_DIM, pl.ds(k0, kt_w)], pr.astype(BF16))
        m_ref[...] = m_new

    m_ref[...] = jnp.full_like(m_ref, NEG_INF)
    l_ref[...] = jnp.zeros_like(l_ref)
    acc_ref[...] = jnp.zeros_like(acc_ref)
    n_full = (i * qb) // kt_w
    s_first = sel_scores(0)

    pos = start + lax.broadcasted_iota(jnp.int32, s_w.shape, 0)
    p_w, inv_w = _softmax_cols(s_w, (pos <= tcol) & (pos > tcol - WINDOW))
    o_w = _mm(vw_ref[0:HEAD_DIM, pl.ds(start, span)], p_w.astype(BF16)) * inv_w

    def body(kt, s_cur):
        s_nxt = sel_scores(kt + 1)
        sel_update(kt, s_cur)
        return s_nxt

    s_last = lax.fori_loop(0, n_full, body, s_first)
    pos = n_full * kt_w + lax.broadcasted_iota(jnp.int32, s_last.shape, 0)
    sel_update(n_full, jnp.where(pos <= tcol, s_last, NEG_INF))
    o_s = acc_ref[...] * (1.0 / l_ref[...])

    gts = _sigmoid(gt_ref[...]).T
    heads = []
    for r in range(NSA_GROUP):
        cs = slice(r * qb, (r + 1) * qb)
        heads.append(gts[r:r + 1, :] * o_c[:, cs] + gts[4 + r:5 + r, :] * o_s[:, cs]
                     + gts[8 + r:9 + r, :] * o_w[:, cs])
    y = jnp.concatenate(heads, axis=0).T
    o_ref[...] = (y * _silu(gc_ref[...].astype(F32))).astype(BF16)


def _nsa(z, vt, small, kc, vct, e128, ovt, b, s):
    n = b * s
    nqb = s // Q_BLOCK
    nc = s // CMP_STRIDE
    gw = NSA_GROUP * HEAD_DIM
    cols = NSA_GROUP * Q_BLOCK
    qspec = lambda off: pl.BlockSpec((Q_BLOCK, gw), lambda bi, g, i: (bi * nqb + i, off // gw + g))
    seq = lambda off: pl.BlockSpec((s, LANES), lambda bi, g, i: (bi, off // LANES + g))
    vseq = lambda row: pl.BlockSpec((LANES, s), lambda bi, g, i: (row // LANES + g, bi))
    return pl.pallas_call(
        functools.partial(_nsa_kernel, s_len=s, nc=nc),
        grid=(b, NSA_KV_HEADS, nqb),
        in_specs=[qspec(OFF_QC), qspec(OFF_GC), seq(OFF_KSD), seq(OFF_KWD),
                  vseq(ZT), vseq(ZT + 256),
                  pl.BlockSpec((None, None, nc, LANES), lambda bi, g, i: (bi, g, 0, 0)),
                  pl.BlockSpec((None, None, HEAD_DIM, nc), lambda bi, g, i: (bi, g, 0, 0)),
                  pl.BlockSpec((Q_BLOCK, LANES), lambda bi, g, i: (bi * nqb + i, 1 + g)),
                  pl.BlockSpec((s, LANES), lambda bi, g, i: (0, 0)),
                  pl.BlockSpec((LANES, nc), lambda bi, g, i: (0, 0))],
        out_specs=pl.BlockSpec((Q_BLOCK, gw), lambda bi, g, i: (bi * nqb + i, g)),
        out_shape=jax.ShapeDtypeStruct((n, BRANCH_WIDTH), BF16),
        scratch_shapes=[pltpu.VMEM((s, 2 * LANES), BF16),
                        pltpu.VMEM((LANES, Q_BLOCK), F32), pltpu.VMEM((1, cols), F32),
                        pltpu.VMEM((1, cols), F32), pltpu.VMEM((HEAD_DIM, cols), F32)],
        compiler_params=pltpu.CompilerParams(
            dimension_semantics=("parallel", "parallel", "arbitrary"),
            vmem_limit_bytes=V7X_VMEM_LIMIT),
        name="nsa",
    )(z, z, z, z, vt, vt, kc, vct, small, e128, ovt)


def _merge_kernel(ya_ref, yb_ref, yc_ref, yd_ref, mg_ref, x_ref, wb_ref, wo_ref, o_ref):
    merged = None
    for nb, y_ref in enumerate((ya_ref, yb_ref, yc_ref, yd_ref)):
        gate = _sigmoid(mg_ref[:, nb * D_MODEL:(nb + 1) * D_MODEL].astype(F32))
        term = gate * _mm(y_ref[...], wb_ref[nb])
        merged = term if merged is None else merged + term
    o_ref[...] = x_ref[...] + _mm(merged.astype(BF16), wo_ref[...])


def _merge(ys, z, x2, wb, wo, tm):
    n = x2.shape[0]
    W = BRANCH_WIDTH
    yspec = pl.BlockSpec((tm, W), lambda i: (i, 0))
    return pl.pallas_call(
        _merge_kernel,
        grid=(n // tm,),
        in_specs=[yspec, yspec, yspec, yspec,
                  pl.BlockSpec((tm, N_BRANCHES * D_MODEL), lambda i: (i, 0)),
                  pl.BlockSpec((tm, D_MODEL), lambda i: (i, 0)),
                  pl.BlockSpec((N_BRANCHES, W, D_MODEL), lambda i: (0, 0, 0)),
                  pl.BlockSpec((D_MODEL, D_MODEL), lambda i: (0, 0))],
        out_specs=pl.BlockSpec((tm, D_MODEL), lambda i: (i, 0)),
        out_shape=jax.ShapeDtypeStruct((n, D_MODEL), F32),
        compiler_params=pltpu.CompilerParams(
            dimension_semantics=("parallel",), vmem_limit_bytes=V7X_VMEM_LIMIT),
        name="merge",
    )(*ys, z, x2, wb, wo)


def _block_diag(blocks):
    nb, a, b_ = blocks.shape
    eye = jnp.eye(nb, dtype=blocks.dtype)
    return jnp.einsum('nab,nm->namb', blocks, eye).reshape(nb * a, nb * b_)


def _mean_matrix(width):
    idx = np.arange(width) // HEAD_DIM
    return jnp.asarray((idx[:, None] == idx[None, :]).astype(np.float32) / HEAD_DIM, BF16)


def _layout_w_in(w_in, qn_a, kn_a, qn_c, kn_c):
    sizes = _split_sizes()
    offs = np.concatenate([[0], np.cumsum(sizes)])
    seg = lambda k: w_in[:, int(offs[k]):int(offs[k + 1])]
    (qa, ka, va, fa, ga, xb, gb, qc, kcc, vcc, ksc, vsc, kwc, vwc, gate_c, gc, ud, vd, gd, mg) = [
        seg(k) for k in range(len(sizes))]
    dup = lambda w: jnp.concatenate([w[:, :HEAD_DIM], w[:, :HEAD_DIM], w[:, HEAD_DIM:], w[:, HEAD_DIM:]], axis=1)
    pad = jnp.zeros((D_MODEL, 256), w_in.dtype)
    w_z = jnp.concatenate([mg, qa, ka, qc, dup(ksc), dup(kwc), va, dup(vsc), dup(vwc),
                           ga, xb, gb, gc, ud, vd, gd, kcc, vcc, pad], axis=1).astype(BF16)
    scale = LOG2E / math.sqrt(HEAD_DIM)
    gain = jnp.ones((ZW,), F32)
    gain = gain.at[OFF_QA:OFF_QA + ZT].set(jnp.tile(qn_a * scale, FOX_HEADS))
    gain = gain.at[OFF_KA:OFF_KA + ZT].set(jnp.tile(kn_a, FOX_HEADS))
    gain = gain.at[OFF_QC:OFF_QC + ZT].set(jnp.tile(qn_c * scale, NSA_HEADS))
    gain = gain.at[OFF_KSD:OFF_KSD + ZT].set(jnp.tile(kn_c, ZT // HEAD_DIM))
    gate3 = gate_c.reshape(D_MODEL, 3, NSA_KV_HEADS, NSA_GROUP)
    zpad = lambda w: jnp.pad(w, ((0, 0), (0, LANES - w.shape[1])))
    w_small = jnp.concatenate(
        [zpad(fa)] + [zpad(gate3[:, :, g, :].reshape(D_MODEL, 3 * NSA_GROUP)) for g in range(NSA_KV_HEADS)],
        axis=1).astype(BF16)
    return w_z, gain.reshape(1, ZW), w_small


def _layout_compress(cmp_pos, wk1, wk2, wv1, wv2, kn_c):
    half = CMP_BLOCK // 2
    eye4 = jnp.eye(4, dtype=F32)

    def w1_half(lo):
        k = wk1.reshape(CMP_BLOCK, HEAD_DIM, CMP_HIDDEN)[lo:lo + half]
        v = wv1.reshape(CMP_BLOCK, HEAD_DIM, CMP_HIDDEN)[lo:lo + half]
        stack = jnp.stack([k, k, v, v])
        return jnp.einsum('spdn,st->psdtn', stack, eye4).reshape(half * 4 * HEAD_DIM, 4 * CMP_HIDDEN).astype(BF16)

    def pos_half(lo):
        p = cmp_pos[lo:lo + half]
        return jnp.broadcast_to(p[:, None, :], (half, 4, HEAD_DIM)).reshape(1, half * 4 * HEAD_DIM)

    w2 = jnp.zeros((4, CMP_HIDDEN, 4, 2, HEAD_DIM), F32)
    for slot, w in enumerate((wk2, wk2, wv2, wv2)):
        w2 = w2.at[slot, :, slot, :, :].set(jnp.broadcast_to(w[:, None, :], (CMP_HIDDEN, 2, HEAD_DIM)))
    w2 = w2.reshape(4 * CMP_HIDDEN, ZT).astype(BF16)
    gain = jnp.tile(kn_c, ZT // HEAD_DIM).reshape(1, ZT)
    return pos_half(0), pos_half(half), w1_half(0), w1_half(half), w2, gain


@functools.lru_cache(maxsize=None)
def _nsa_consts(s):
    nc = s // CMP_STRIDE
    ns = s // SLC_BLOCK
    e = np.zeros((s, LANES), np.float32)
    e[np.arange(s), np.arange(s) // SLC_BLOCK] = 1.0
    c0 = np.arange(nc) * CMP_STRIDE
    s0 = np.arange(ns) * SLC_BLOCK
    ovl = np.minimum(c0[:, None] + CMP_BLOCK, s0[None, :] + SLC_BLOCK) - np.maximum(c0[:, None], s0[None, :])
    ovt = np.zeros((LANES, nc), np.float32)
    ovt[:ns, :] = (np.clip(ovl, 0, None) / CMP_BLOCK).T
    ovt[:, nc - 1] = 0.0
    return e, ovt


def _tile(s, pref):
    t = pref
    while s % t:
        t //= 2
    return t


def _branches(x2, b, s, norm_g, w_in, b_forget, qn_a, kn_a, conv_w, conv_b, w_rg_a, b_rg_a, w_rg_x,
              b_rg_x, lru_lambda, qn_c, kn_c, cmp_pos, wk1, wk2, wv1, wv2, ln_v_g, w_spatial,
              b_spatial):
    n = b * s
    W = BRANCH_WIDTH
    w_z, gain_row, w_small = _layout_w_in(w_in, qn_a, kn_a, qn_c, kn_c)
    bd = _mean_matrix(ZT)
    z, small = _inproj(x2, norm_g.reshape(1, D_MODEL), w_z, gain_row, bd, w_small, _tile(n, 1024))
    vt = _vtrans(z, _tile(s, 512))

    bf_row = jnp.pad(b_forget, (0, LANES - FOX_HEADS)).reshape(1, LANES)
    caq, cak = _cprep(small, bf_row, b, s, _tile(s, 512))
    ys_a = _fox(z, vt, caq, cak, b, s, _tile(s, 512))

    ys_b = _lru(z, conv_w, conv_b.reshape(1, W), _block_diag(w_rg_a).astype(BF16), b_rg_a.reshape(1, W),
                _block_diag(w_rg_x).astype(BF16), b_rg_x.reshape(1, W), lru_lambda.reshape(1, W),
                b, s, _tile(s, 256))

    nc = s // CMP_STRIDE
    t16 = z[:, OFF_CMP:OFF_CMP + 4 * HEAD_DIM].reshape(b, nc, CMP_STRIDE * 4 * HEAD_DIM)
    ptop, pbot, w1t, w1b, w2c, cgain = _layout_compress(cmp_pos, wk1, wk2, wv1, wv2, kn_c)
    kc, vct = _compress(t16, ptop, pbot, w1t, w1b, w2c, bd, cgain, b, nc)
    e_np, ovt_np = _nsa_consts(s)
    ys_c = _nsa(z, vt, small, kc, vct, jnp.asarray(e_np, BF16), jnp.asarray(ovt_np, BF16), b, s)

    causal = jnp.tril(jnp.ones((SG_CHUNK, SG_CHUNK), F32))
    wm = (w_spatial * causal).astype(BF16)
    bs_full = jnp.broadcast_to(b_spatial.T[:, :, None], (SG_CHUNK, SG_GROUPS, HEAD_DIM)).reshape(SG_CHUNK, W)
    ys_d = _sgu(z, ln_v_g.reshape(1, W), wm, bs_full, n, _tile(s, 512))
    return (ys_a, ys_b, ys_c, ys_d), z


def _layer(x2, b, s, *params):
    ys, z = _branches(x2, b, s, *params[:-2])
    w_branch, w_out = params[-2:]
    return _merge(ys, z, x2, w_branch.astype(BF16), w_out.astype(BF16), _tile(b * s, 512))


def kernel(x, norm_g, w_in, b_forget, qn_a, kn_a, conv_w, conv_b, w_rg_a, b_rg_a, w_rg_x, b_rg_x,
           lru_lambda, qn_c, kn_c, cmp_pos, cmp_k_w1, cmp_k_w2, cmp_v_w1, cmp_v_w2, ln_v_g,
           w_spatial, b_spatial, w_branch, w_out):
    b, s, d = x.shape
    x2 = x.reshape(b * s, d)
    params = (norm_g, w_in, b_forget, qn_a, kn_a, conv_w, conv_b, w_rg_a, b_rg_a, w_rg_x, b_rg_x,
              lru_lambda, qn_c, kn_c, cmp_pos, cmp_k_w1, cmp_k_w2, cmp_v_w1, cmp_v_w2, ln_v_g,
              w_spatial, b_spatial, w_branch, w_out)
    for layer in range(norm_g.shape[0]):
        x2 = _layer(x2, b, s, *[p[layer] for p in params])
    return x2.reshape(b, s, d)
```

```python
import functools
import math

import numpy as np
import jax
import jax.numpy as jnp
from jax import lax
from jax.experimental import pallas as pl
from jax.experimental.pallas import tpu as pltpu

F32 = jnp.float32
BF16 = jnp.bfloat16

D_MODEL = 1024
HEAD_DIM = 64
BRANCH_WIDTH = 512
N_BRANCHES = 4
FOX_HEADS = 8
LRU_BLOCKS = 8
CONV_WIDTH = 4
LRU_C = 8.0
NSA_HEADS = 8
NSA_KV_HEADS = 2
NSA_GROUP = 4
CMP_BLOCK = 32
CMP_STRIDE = 16
CMP_HIDDEN = 128
SLC_BLOCK = 64
SLC_TOPK = 16
SLC_LOCAL = 2
SLC_FORCE_SCORE = 1e6
WINDOW = 512
Q_BLOCK = 128
SG_CHUNK = 128
SG_GROUPS = 8
NORM_EPS = 1e-6
NEG_INF = -1e30
SEL_NEG = -32768.0

LANES = 128
V7X_VMEM_LIMIT = 56 * 1024 * 1024
SEL_TILE = 512
LOG2E = 1.4426950408889634

ZT = 512
OFF_MG = 0
OFF_QA = 8 * ZT
OFF_KA = 9 * ZT
OFF_QC = 10 * ZT
OFF_KSD = 11 * ZT
OFF_KWD = 11 * ZT + 256
OFF_VA = 12 * ZT
OFF_VSD = 13 * ZT
OFF_VWD = 13 * ZT + 256
OFF_GA = 14 * ZT
OFF_XB = 15 * ZT
OFF_GB = 16 * ZT
OFF_GC = 17 * ZT
OFF_UD = 18 * ZT
OFF_VD = 19 * ZT
OFF_GD = 20 * ZT
OFF_CMP = 21 * ZT
ZW = 22 * ZT
IN_TN = 2 * ZT
NORM_TILES = (OFF_QA // IN_TN, OFF_QC // IN_TN)
SMALL_W = 384
VT_TILES = (OFF_VA // ZT, OFF_VSD // ZT)


def _nt(a, b):
    return lax.dot_general(a, b, (((1,), (1,)), ((), ())), preferred_element_type=F32)


def _mm(a, b):
    return jnp.dot(a, b, preferred_element_type=F32)


def _sigmoid(x):
    return 0.5 * jnp.tanh(0.5 * x) + 0.5


def _silu(x):
    return x * _sigmoid(x)


def _split_sizes():
    W, kv = BRANCH_WIDTH, NSA_KV_HEADS * HEAD_DIM
    return (W, W, W, FOX_HEADS, W, W, W, W, kv, kv, kv, kv, kv, kv, 3 * NSA_HEADS, W,
            W, W, W, N_BRANCHES * D_MODEL)


def _inproj_kernel(x_ref, g_ref, w_ref, gain_ref, bd_ref, ws_ref, z_ref, small_ref, xn_ref):
    j = pl.program_id(1)

    @pl.when(j == 0)
    def _():
        x = x_ref[...]
        ms = jnp.mean(x * x, axis=-1, keepdims=True)
        xn = ((x * lax.rsqrt(ms + NORM_EPS)) * g_ref[...]).astype(BF16)
        xn_ref[...] = xn
        small_ref[...] = _mm(xn, ws_ref[...])

    acc = _mm(xn_ref[...], w_ref[...])
    is_norm = functools.reduce(jnp.logical_or, [j == t for t in NORM_TILES])

    @pl.when(is_norm)
    def _():
        sq = (acc * acc).astype(BF16)
        ms = jnp.concatenate([_mm(sq[:, c:c + ZT], bd_ref[...]) for c in range(0, IN_TN, ZT)], axis=1)
        z_ref[...] = (acc * lax.rsqrt(ms + NORM_EPS) * gain_ref[...]).astype(BF16)

    @pl.when(jnp.logical_not(is_norm))
    def _():
        z_ref[...] = acc.astype(BF16)


def _inproj(x2, norm_g, w_z, gain_row, bd, w_small, tm):
    n = x2.shape[0]
    grid = (n // tm, ZW // IN_TN)
    return pl.pallas_call(
        _inproj_kernel,
        grid=grid,
        in_specs=[
            pl.BlockSpec((tm, D_MODEL), lambda i, j: (i, 0)),
            pl.BlockSpec((1, D_MODEL), lambda i, j: (0, 0)),
            pl.BlockSpec((D_MODEL, IN_TN), lambda i, j: (0, j)),
            pl.BlockSpec((1, IN_TN), lambda i, j: (0, j)),
            pl.BlockSpec((ZT, ZT), lambda i, j: (0, 0)),
            pl.BlockSpec((D_MODEL, SMALL_W), lambda i, j: (0, 0)),
        ],
        out_specs=[
            pl.BlockSpec((tm, IN_TN), lambda i, j: (i, j)),
            pl.BlockSpec((tm, SMALL_W), lambda i, j: (i, 0)),
        ],
        out_shape=[
            jax.ShapeDtypeStruct((n, ZW), BF16),
            jax.ShapeDtypeStruct((n, SMALL_W), F32),
        ],
        scratch_shapes=[pltpu.VMEM((tm, D_MODEL), BF16)],
        compiler_params=pltpu.CompilerParams(
            dimension_semantics=("parallel", "arbitrary"), vmem_limit_bytes=V7X_VMEM_LIMIT),
        name="inproj",
    )(x2, norm_g, w_z, gain_row, bd, w_small)


def _vtrans_kernel(z_ref, o_ref):
    o_ref[...] = z_ref[...].astype(F32).T.astype(BF16)


def _vtrans(z, tm):
    n = z.shape[0]
    step = VT_TILES[1] - VT_TILES[0]
    return pl.pallas_call(
        _vtrans_kernel,
        grid=(n // tm, len(VT_TILES)),
        in_specs=[pl.BlockSpec((tm, ZT), lambda i, j: (i, VT_TILES[0] + step * j))],
        out_specs=pl.BlockSpec((ZT, tm), lambda i, j: (j, i)),
        out_shape=jax.ShapeDtypeStruct((len(VT_TILES) * ZT, n), BF16),
        compiler_params=pltpu.CompilerParams(dimension_semantics=("parallel", "parallel")),
        name="vtrans",
    )(z)


def _cprep_kernel(fa_ref, bf_ref, pq_ref, pk_ref, oq_ref, ok_ref, caq_ref, cak_ref, carry_ref,
                  *, tt):
    @pl.when(pl.program_id(1) == 0)
    def _():
        carry_ref[...] = jnp.zeros_like(carry_ref)

    x = fa_ref[...] + bf_ref[...]
    c = jnp.minimum(x, 0.0) - jnp.log(1.0 + jnp.exp(-jnp.abs(x)))
    row = lax.broadcasted_iota(jnp.int32, c.shape, 0)
    d = 1
    while d < tt:
        c = c + jnp.where(row >= d, pltpu.roll(c, d, axis=0), 0.0)
        d *= 2
    c = c + carry_ref[...]
    carry_ref[...] = c[tt - 1:tt, :]
    c = c * LOG2E
    hi = c.astype(BF16)
    r1 = c - hi.astype(F32)
    mid = r1.astype(BF16)
    lo = (r1 - mid.astype(F32)).astype(BF16)
    parts = jnp.concatenate([hi, mid, lo], axis=1)
    caq_ref[...] = (_mm(parts, pq_ref[...]) + oq_ref[...]).astype(BF16)
    cak_ref[...] = (_mm(parts, pk_ref[...]) + ok_ref[...]).astype(BF16)


def _aug_lane(h):
    return (h // 2) * LANES + (h % 2) * 8


@functools.lru_cache(maxsize=None)
def _cprep_consts():
    pq = np.zeros((3 * LANES, 4 * LANES), np.float32)
    pk = np.zeros((3 * LANES, 4 * LANES), np.float32)
    oq = np.zeros((1, 4 * LANES), np.float32)
    ok = np.zeros((1, 4 * LANES), np.float32)
    for h in range(FOX_HEADS):
        base = _aug_lane(h)
        for part in range(3):
            pq[part * LANES + h, base + part] = 1.0
            pk[part * LANES + h, base + 3 + part] = -1.0
            oq[0, base + 3 + part] = 1.0
            ok[0, base + part] = 1.0
    return pq, pk, oq, ok


def _cprep(small, bf_row, b, s, tt):
    n = b * s
    nt = s // tt
    pq, pk, oq, ok = _cprep_consts()
    full = lambda shape: pl.BlockSpec(shape, lambda bi, ti: (0, 0))
    return pl.pallas_call(
        functools.partial(_cprep_kernel, tt=tt),
        grid=(b, nt),
        in_specs=[
            pl.BlockSpec((tt, LANES), lambda bi, ti: (bi * nt + ti, 0)),
            full((1, LANES)),
            full((3 * LANES, 4 * LANES)),
            full((3 * LANES, 4 * LANES)),
            full((1, 4 * LANES)),
            full((1, 4 * LANES)),
        ],
        out_specs=[
            pl.BlockSpec((tt, 4 * LANES), lambda bi, ti: (bi * nt + ti, 0)),
            pl.BlockSpec((tt, 4 * LANES), lambda bi, ti: (bi * nt + ti, 0)),
        ],
        out_shape=[jax.ShapeDtypeStruct((n, 4 * LANES), BF16)] * 2,
        scratch_shapes=[pltpu.VMEM((1, LANES), F32)],
        compiler_params=pltpu.CompilerParams(dimension_semantics=("parallel", "arbitrary")),
        name="cprep",
    )(small, bf_row, jnp.asarray(pq, BF16), jnp.asarray(pk, BF16), jnp.asarray(oq), jnp.asarray(ok))


def _fox_kernel(qi_ref, ki_ref, q_ref, caq_ref, k_ref, cak_ref, vt_ref, ga_ref, o_ref,
                lhs_ref, rhs_ref, m_ref, l_ref, acc_ref, *, t):
    pair = pl.program_id(1)
    qi = qi_ref[pair]
    ki = ki_ref[pair]
    n_pairs = FOX_HEADS // 2

    @pl.when(ki == 0)
    def _():
        lane = lax.broadcasted_iota(jnp.int32, (t, 2 * LANES), 1)
        head_of_lane = jnp.where(lane < LANES, lane // HEAD_DIM, (lane - LANES) // 8)
        for hp in range(n_pairs):
            cols = slice(hp * LANES, (hp + 1) * LANES)
            lhs = jnp.concatenate([q_ref[:, cols], caq_ref[:, cols]], axis=1).astype(F32)
            for hh in range(2):
                lhs_ref[2 * hp + hh] = jnp.where(head_of_lane == hh, lhs, 0.0).astype(BF16)
        m_ref[...] = jnp.full_like(m_ref, NEG_INF)
        l_ref[...] = jnp.zeros_like(l_ref)
        acc_ref[...] = jnp.zeros_like(acc_ref)

    def step(diag):
        for hp in range(n_pairs):
            cols = slice(hp * LANES, (hp + 1) * LANES)
            rhs_ref[hp] = jnp.concatenate([k_ref[:, cols], cak_ref[:, cols]], axis=1)

        s_next = _nt(rhs_ref[0], lhs_ref[0])
        for h in range(FOX_HEADS):
            s = s_next
            if h + 1 < FOX_HEADS:
                s_next = _nt(rhs_ref[(h + 1) // 2], lhs_ref[h + 1])
            if diag:
                key = lax.broadcasted_iota(jnp.int32, s.shape, 0)
                qry = lax.broadcasted_iota(jnp.int32, s.shape, 1)
                s = jnp.where(key <= qry, s, NEG_INF)
            m_prev = m_ref[h]
            m_new = jnp.maximum(m_prev, jnp.max(s, axis=0, keepdims=True))
            alpha = jnp.exp2(m_prev - m_new)
            p = jnp.exp2(s - m_new)
            l_ref[h] = alpha * l_ref[h] + jnp.sum(p, axis=0, keepdims=True)
            vt = vt_ref[h * HEAD_DIM:(h + 1) * HEAD_DIM, :]
            acc_ref[h] = alpha * acc_ref[h] + _mm(vt, p.astype(BF16))
            m_ref[h] = m_new

    @pl.when(ki < qi)
    def _():
        step(False)

    @pl.when(ki == qi)
    def _():
        step(True)
        for hp in range(n_pairs):
            cols = slice(hp * LANES, (hp + 1) * LANES)
            o_t = jnp.concatenate([acc_ref[2 * hp] / l_ref[2 * hp],
                                   acc_ref[2 * hp + 1] / l_ref[2 * hp + 1]], axis=0)
            o_ref[:, cols] = (o_t.T * _silu(ga_ref[:, cols].astype(F32))).astype(BF16)


def _fox(z, vt, caq, cak, b, s, t):
    n = b * s
    nq = s // t
    W = BRANCH_WIDTH
    qi_of = np.concatenate([np.full(q + 1, q) for q in range(nq)]).astype(np.int32)
    ki_of = np.concatenate([np.arange(q + 1) for q in range(nq)]).astype(np.int32)
    qmap = lambda off: (lambda bi, p, qi, ki: (bi * nq + qi[p], off // W))
    kmap = lambda off: (lambda bi, p, qi, ki: (bi * nq + ki[p], off // W))
    grid_spec = pltpu.PrefetchScalarGridSpec(
        num_scalar_prefetch=2,
        grid=(b, len(qi_of)),
        in_specs=[
            pl.BlockSpec((t, W), qmap(OFF_QA)),
            pl.BlockSpec((t, W), qmap(0)),
            pl.BlockSpec((t, W), kmap(OFF_KA)),
            pl.BlockSpec((t, W), kmap(0)),
            pl.BlockSpec((W, t), lambda bi, p, qi, ki: (0, bi * nq + ki[p])),
            pl.BlockSpec((t, W), qmap(OFF_GA)),
        ],
        out_specs=pl.BlockSpec((t, W), qmap(0)),
        scratch_shapes=[
            pltpu.VMEM((FOX_HEADS, t, 2 * LANES), BF16),
            pltpu.VMEM((FOX_HEADS // 2, t, 2 * LANES), BF16),
            pltpu.VMEM((FOX_HEADS, 1, t), F32),
            pltpu.VMEM((FOX_HEADS, 1, t), F32),
            pltpu.VMEM((FOX_HEADS, HEAD_DIM, t), F32),
        ],
    )
    return pl.pallas_call(
        functools.partial(_fox_kernel, t=t),
        grid_spec=grid_spec,
        out_shape=jax.ShapeDtypeStruct((n, W), BF16),
        compiler_params=pltpu.CompilerParams(
            dimension_semantics=("parallel", "arbitrary"), vmem_limit_bytes=V7X_VMEM_LIMIT),
        name="fox",
    )(jnp.asarray(qi_of), jnp.asarray(ki_of), z, caq, z, cak, vt, z)


def _lru_kernel(xb_ref, gb_ref, cw_ref, cb_ref, wa_ref, ba_ref, wx_ref, bx_ref, lam_ref, o_ref,
                xpad_ref, h_ref, *, tt):
    @pl.when(pl.program_id(1) == 0)
    def _():
        xpad_ref[0:8, :] = jnp.zeros((8, BRANCH_WIDTH), F32)
        h_ref[...] = jnp.zeros_like(h_ref)

    x = xb_ref[...].astype(F32)
    xpad_ref[8:tt + 8, :] = x
    xc = cb_ref[...] + cw_ref[3:4, :] * x
    for j in range(CONV_WIDTH - 1):
        sh = CONV_WIDTH - 1 - j
        xc = xc + cw_ref[j:j + 1, :] * xpad_ref[8 - sh:8 - sh + tt, :]
    xpad_ref[0:8, :] = x[tt - 8:tt, :]

    xcb = xc.astype(BF16)
    r = _sigmoid(_mm(xcb, wa_ref[...]) + ba_ref[...])
    ig = _sigmoid(_mm(xcb, wx_ref[...]) + bx_ref[...])
    nl = -lam_ref[...]
    softplus = jnp.maximum(nl, 0.0) + jnp.log(1.0 + jnp.exp(-jnp.abs(nl)))
    log_a = (-LRU_C) * r * softplus
    a = jnp.exp(log_a)
    bb = jnp.sqrt(1.0 - jnp.exp(2.0 * log_a)) * (ig * xc)

    row = lax.broadcasted_iota(jnp.int32, a.shape, 0)
    d = 1
    while d < tt:
        keep = row >= d
        bb = bb + a * jnp.where(keep, pltpu.roll(bb, d, axis=0), 0.0)
        a = a * jnp.where(keep, pltpu.roll(a, d, axis=0), 1.0)
        d *= 2
    h = bb + a * h_ref[...]
    h_ref[...] = h[tt - 1:tt, :]
    o_ref[...] = (h * _silu(gb_ref[...].astype(F32))).astype(BF16)


def _lru(z, conv_w, conv_b, wa_bd, b_a, wx_bd, b_x, lam, b, s, tt):
    n = b * s
    nt = s // tt
    W = BRANCH_WIDTH
    full = lambda shape: pl.BlockSpec(shape, lambda bi, ti: (0, 0))
    zb = lambda off: pl.BlockSpec((tt, W), lambda bi, ti: (bi * nt + ti, off // W))
    return pl.pallas_call(
        functools.partial(_lru_kernel, tt=tt),
        grid=(b, nt),
        in_specs=[zb(OFF_XB), zb(OFF_GB), full((CONV_WIDTH, W)), full((1, W)), full((W, W)),
                  full((1, W)), full((W, W)), full((1, W)), full((1, W))],
        out_specs=pl.BlockSpec((tt, W), lambda bi, ti: (bi * nt + ti, 0)),
        out_shape=jax.ShapeDtypeStruct((n, W), BF16),
        scratch_shapes=[pltpu.VMEM((tt + 8, W), F32), pltpu.VMEM((1, W), F32)],
        compiler_params=pltpu.CompilerParams(dimension_semantics=("parallel", "arbitrary")),
        name="lru",
    )(z, z, conv_w, conv_b, wa_bd, b_a, wx_bd, b_x, lam)


def _gelu(x):
    return 0.5 * x * (1.0 + lax.erf(x * (1.0 / math.sqrt(2.0))))


def _sgu_kernel(u_ref, v_ref, g_ref, lng_ref, wm_ref, bs_ref, o_ref, *, tr):
    u = _gelu(u_ref[...].astype(F32))
    v = _gelu(v_ref[...].astype(F32))
    mu = jnp.mean(v, axis=-1, keepdims=True)
    vc = v - mu
    vn = (vc * lax.rsqrt(jnp.mean(vc * vc, axis=-1, keepdims=True) + NORM_EPS) * lng_ref[...]).astype(BF16)
    lane = lax.broadcasted_iota(jnp.int32, (SG_CHUNK, LANES), 1)
    chunks = []
    for c in range(tr // SG_CHUNK):
        vch = vn[c * SG_CHUNK:(c + 1) * SG_CHUNK, :]
        pairs = []
        for p in range(SG_GROUPS // 2):
            vp = vch[:, p * LANES:(p + 1) * LANES]
            pairs.append(jnp.where(lane < HEAD_DIM, _mm(wm_ref[2 * p], vp), _mm(wm_ref[2 * p + 1], vp)))
        chunks.append(jnp.concatenate(pairs, axis=1) + bs_ref[...])
    mixed = jnp.concatenate(chunks, axis=0)
    o_ref[...] = (u * mixed * _silu(g_ref[...].astype(F32))).astype(BF16)


def _sgu(z, ln_g, wm, bs_full, n, tr):
    W = BRANCH_WIDTH
    zb = lambda off: pl.BlockSpec((tr, W), lambda i: (i, off // W))
    return pl.pallas_call(
        functools.partial(_sgu_kernel, tr=tr),
        grid=(n // tr,),
        in_specs=[zb(OFF_UD), zb(OFF_VD), zb(OFF_GD),
                  pl.BlockSpec((1, W), lambda i: (0, 0)),
                  pl.BlockSpec((SG_GROUPS, SG_CHUNK, SG_CHUNK), lambda i: (0, 0, 0)),
                  pl.BlockSpec((SG_CHUNK, W), lambda i: (0, 0))],
        out_specs=pl.BlockSpec((tr, W), lambda i: (i, 0)),
        out_shape=jax.ShapeDtypeStruct((n, W), BF16),
        compiler_params=pltpu.CompilerParams(dimension_semantics=("parallel",)),
        name="sgu",
    )(z, z, z, ln_g, wm, bs_full)


def _compress_kernel(t16_ref, ptop_ref, pbot_ref, w1t_ref, w1b_ref, w2_ref, bd_ref, gain_ref,
                     kc_ref, vct_ref, *, nc):
    t16 = t16_ref[...].astype(F32)
    a = _mm((t16 + ptop_ref[...]).astype(BF16), w1t_ref[...])
    bm = _mm((t16 + pbot_ref[...]).astype(BF16), w1b_ref[...])
    h = a + pltpu.roll(bm, nc - 1, axis=0)
    out = _mm(_silu(h).astype(BF16), w2_ref[...])
    ms = _mm((out * out).astype(BF16), bd_ref[...])
    kn = out * lax.rsqrt(ms + NORM_EPS) * gain_ref[...]
    out_t = out.T
    for g in range(NSA_KV_HEADS):
        kc_ref[g] = kn[:, g * LANES:(g + 1) * LANES].astype(BF16)
        vct_ref[g] = out_t[(2 + g) * LANES:(2 + g) * LANES + HEAD_DIM, :].astype(BF16)


def _compress(t16, ptop, pbot, w1t, w1b, w2, bd, gain, b, nc):
    kw = 16 * 4 * HEAD_DIM
    full = lambda shape: pl.BlockSpec(shape, lambda bi: tuple(0 for _ in shape))
    return pl.pallas_call(
        functools.partial(_compress_kernel, nc=nc),
        grid=(b,),
        in_specs=[pl.BlockSpec((None, nc, kw), lambda bi: (bi, 0, 0)),
                  full((1, kw)), full((1, kw)), full((kw, 4 * CMP_HIDDEN)), full((kw, 4 * CMP_HIDDEN)),
                  full((4 * CMP_HIDDEN, ZT)), full((ZT, ZT)), full((1, ZT))],
        out_specs=[pl.BlockSpec((None, NSA_KV_HEADS, nc, LANES), lambda bi: (bi, 0, 0, 0)),
                   pl.BlockSpec((None, NSA_KV_HEADS, HEAD_DIM, nc), lambda bi: (bi, 0, 0, 0))],
        out_shape=[jax.ShapeDtypeStruct((b, NSA_KV_HEADS, nc, LANES), BF16),
                   jax.ShapeDtypeStruct((b, NSA_KV_HEADS, HEAD_DIM, nc), BF16)],
        compiler_params=pltpu.CompilerParams(
            dimension_semantics=("parallel",), vmem_limit_bytes=V7X_VMEM_LIMIT),
        name="compress",
    )(t16, ptop, pbot, w1t, w1b, w2, bd, gain)


def _softmax_cols(s, mask):
    sm = jnp.where(mask, s, NEG_INF)
    m = jnp.max(sm, axis=0, keepdims=True)
    p = jnp.exp2(sm - jnp.where(m > 0.5 * NEG_INF, m, 0.0))
    l = jnp.sum(p, axis=0, keepdims=True)
    return p, jnp.where(l > 0.0, 1.0 / l, 0.0)


def _nsa_kernel(q_ref, gc_ref, ks_ref, kw_ref, vs_ref, vw_ref, kc_ref, vct_ref, gt_ref, e_ref, ovt_ref,
                o_ref, kse_ref, sa_ref, sb_ref, sc_ref, m_ref, l_ref, acc_ref, *, s_len, nc):
    i = pl.program_id(2)
    qb, cols = Q_BLOCK, NSA_GROUP * Q_BLOCK
    ns = s_len // SLC_BLOCK
    top_k = min(SLC_TOPK, ns)
    kt_w = min(SEL_TILE, s_len)
    span = min(WINDOW + Q_BLOCK, s_len)
    lane = lax.broadcasted_iota(jnp.int32, (qb, LANES), 1)

    @pl.when(i == 0)
    def _():
        kse_ref[:, 0:LANES] = ks_ref[...]
        kse_ref[:, LANES:2 * LANES] = e_ref[...]

    parts = []
    for r in range(NSA_GROUP):
        blk = q_ref[:, (r // 2) * LANES:(r // 2 + 1) * LANES].astype(F32)
        parts.append(jnp.where(lane // HEAD_DIM == r % 2, blk, 0.0).astype(BF16))
    qs = jnp.concatenate(parts, axis=0)
    tcol = i * qb + lax.broadcasted_iota(jnp.int32, (1, cols), 1) % qb

    s_c = _nt(kc_ref[...], qs)
    start = pl.multiple_of(jnp.maximum(i - WINDOW // qb, 0) * qb, qb)
    s_w = _nt(kw_ref[pl.ds(start, span), :], qs)

    cend = lax.broadcasted_iota(jnp.int32, s_c.shape, 0) * CMP_STRIDE + (CMP_BLOCK - 1)
    p_c, inv_c = _softmax_cols(s_c, cend <= tcol)
    p_c = p_c * inv_c
    o_c = _mm(vct_ref[...], p_c.astype(BF16))

    ps = p_c[:, 0:qb] + p_c[:, qb:2 * qb] + p_c[:, 2 * qb:3 * qb] + p_c[:, 3 * qb:4 * qb]
    ps_hi = ps.astype(BF16)
    ps_lo = (ps - ps_hi.astype(F32)).astype(BF16)
    imp = _mm(ovt_ref[...], ps_hi) + _mm(ovt_ref[...], ps_lo)

    jrow = lax.broadcasted_iota(jnp.int32, (LANES, qb), 0)
    jt = (i * qb + lax.broadcasted_iota(jnp.int32, (LANES, qb), 1)) // SLC_BLOCK
    valid = (jrow <= jt) & (jrow < ns)
    forced = (jrow == 0) | (valid & (jrow > jt - SLC_LOCAL))
    score = jnp.where(forced, SLC_FORCE_SCORE, jnp.where(valid, imp, -1.0))
    sc_ref[...] = jnp.where(jrow < ns, score, -2.0)
    slabs = []
    for k in range(-(-ns // 8)):
        mine = sc_ref[8 * k:8 * k + 8, :]
        sub = 8 * k + lax.broadcasted_iota(jnp.int32, (8, qb), 0)
        cnt = jnp.zeros((8, qb), F32)
        for jp in range(ns):
            other = sc_ref[jp:jp + 1, :]
            if jp < 8 * k:
                beats = other >= mine
            elif jp >= 8 * k + 8:
                beats = other > mine
            else:
                beats = (other > mine) | ((other == mine) & (sub > jp))
            cnt = cnt + jnp.where(beats, 1.0, 0.0)
        slabs.append(jnp.where((cnt < top_k) & (mine >= 0.0), 0.0, SEL_NEG))
    if len(slabs) * 8 < LANES:
        slabs.append(jnp.full((LANES - len(slabs) * 8, qb), SEL_NEG, F32))
    selb = jnp.concatenate(slabs, axis=0).T.astype(BF16)
    lhs = jnp.concatenate([qs, jnp.concatenate([selb] * NSA_GROUP, axis=0)], axis=1)

    def sel_scores(kt):
        k0 = pl.multiple_of(kt * kt_w, kt_w)
        return _nt(kse_ref[pl.ds(k0, kt_w), :], lhs)

    def sel_update(kt, s):
        k0 = pl.multiple_of(kt * kt_w, kt_w)
        m_prev = m_ref[...]
        m_new = jnp.maximum(m_prev, jnp.max(s, axis=0, keepdims=True))
        alpha = jnp.exp2(m_prev - m_new)
        pr = jnp.exp2(s - m_new)
        l_ref[...] = alpha * l_ref[...] + jnp.sum(pr, axis=0, keepdims=True)
        acc_ref[...] = alpha * acc_ref[...] + _mm(vs_ref[0:HEAD_DIM, pl.ds(k0, kt_w)], pr.astype(BF16))
        m_ref[...] = m_new

    m_ref[...] = jnp.full_like(m_ref, NEG_INF)
    l_ref[...] = jnp.zeros_like(l_ref)
    acc_ref[...] = jnp.zeros_like(acc_ref)
    n_full = (i * qb) // kt_w
    sa_ref[...] = sel_scores(0)

    pos = start + lax.broadcasted_iota(jnp.int32, s_w.shape, 0)
    p_w, inv_w = _softmax_cols(s_w, (pos <= tcol) & (pos > tcol - WINDOW))
    o_w = _mm(vw_ref[0:HEAD_DIM, pl.ds(start, span)], p_w.astype(BF16)) * inv_w

    def body(k2, carry):
        sb_ref[...] = sel_scores(2 * k2 + 1)
        sel_update(2 * k2, sa_ref[...])
        sa_ref[...] = sel_scores(2 * k2 + 2)
        sel_update(2 * k2 + 1, sb_ref[...])
        return carry

    lax.fori_loop(0, n_full // 2, body, 0)

    def last_update(s_last):
        pos_k = n_full * kt_w + lax.broadcasted_iota(jnp.int32, s_last.shape, 0)
        sel_update(n_full, jnp.where(pos_k <= tcol, s_last, NEG_INF))

    @pl.when(n_full % 2 == 1)
    def _():
        sb_ref[...] = sel_scores(n_full)
        sel_update(n_full - 1, sa_ref[...])
        last_update(sb_ref[...])

    @pl.when(n_full % 2 == 0)
    def _():
        last_update(sa_ref[...])

    o_s = acc_ref[...] * (1.0 / l_ref[...])

    gts = _sigmoid(gt_ref[...]).T
    heads = []
    for r in range(NSA_GROUP):
        cs = slice(r * qb, (r + 1) * qb)
        heads.append(gts[r:r + 1, :] * o_c[:, cs] + gts[4 + r:5 + r, :] * o_s[:, cs]
                     + gts[8 + r:9 + r, :] * o_w[:, cs])
    y = jnp.concatenate(heads, axis=0).T
    o_ref[...] = (y * _silu(gc_ref[...].astype(F32))).astype(BF16)


def _nsa(z, vt, small, kc, vct, e128, ovt, b, s):
    n = b * s
    nqb = s // Q_BLOCK
    nc = s // CMP_STRIDE
    gw = NSA_GROUP * HEAD_DIM
    cols = NSA_GROUP * Q_BLOCK
    qspec = lambda off: pl.BlockSpec((Q_BLOCK, gw), lambda bi, g, i: (bi * nqb + i, off // gw + g))
    seq = lambda off: pl.BlockSpec((s, LANES), lambda bi, g, i: (bi, off // LANES + g))
    vseq = lambda row: pl.BlockSpec((LANES, s), lambda bi, g, i: (row // LANES + g, bi))
    return pl.pallas_call(
        functools.partial(_nsa_kernel, s_len=s, nc=nc),
        grid=(b, NSA_KV_HEADS, nqb),
        in_specs=[qspec(OFF_QC), qspec(OFF_GC), seq(OFF_KSD), seq(OFF_KWD),
                  vseq(ZT), vseq(ZT + 256),
                  pl.BlockSpec((None, None, nc, LANES), lambda bi, g, i: (bi, g, 0, 0)),
                  pl.BlockSpec((None, None, HEAD_DIM, nc), lambda bi, g, i: (bi, g, 0, 0)),
                  pl.BlockSpec((Q_BLOCK, LANES), lambda bi, g, i: (bi * nqb + i, 1 + g)),
                  pl.BlockSpec((s, LANES), lambda bi, g, i: (0, 0)),
                  pl.BlockSpec((LANES, nc), lambda bi, g, i: (0, 0))],
        out_specs=pl.BlockSpec((Q_BLOCK, gw), lambda bi, g, i: (bi * nqb + i, g)),
        out_shape=jax.ShapeDtypeStruct((n, BRANCH_WIDTH), BF16),
        scratch_shapes=[pltpu.VMEM((s, 2 * LANES), BF16),
                        pltpu.VMEM((min(SEL_TILE, s), cols), F32),
                        pltpu.VMEM((min(SEL_TILE, s), cols), F32),
                        pltpu.VMEM((LANES, Q_BLOCK), F32), pltpu.VMEM((1, cols), F32),
                        pltpu.VMEM((1, cols), F32), pltpu.VMEM((HEAD_DIM, cols), F32)],
        compiler_params=pltpu.CompilerParams(
            dimension_semantics=("parallel", "parallel", "arbitrary"),
            vmem_limit_bytes=V7X_VMEM_LIMIT),
        name="nsa",
    )(z, z, z, z, vt, vt, kc, vct, small, e128, ovt)


def _merge_kernel(ya_ref, yb_ref, yc_ref, yd_ref, mg_ref, x_ref, wb_ref, wo_ref, o_ref):
    merged = None
    for nb, y_ref in enumerate((ya_ref, yb_ref, yc_ref, yd_ref)):
        gate = _sigmoid(mg_ref[:, nb * D_MODEL:(nb + 1) * D_MODEL].astype(F32))
        term = gate * _mm(y_ref[...], wb_ref[nb])
        merged = term if merged is None else merged + term
    o_ref[...] = x_ref[...] + _mm(merged.astype(BF16), wo_ref[...])


def _merge(ys, z, x2, wb, wo, tm):
    n = x2.shape[0]
    W = BRANCH_WIDTH
    yspec = pl.BlockSpec((tm, W), lambda i: (i, 0))
    return pl.pallas_call(
        _merge_kernel,
        grid=(n // tm,),
        in_specs=[yspec, yspec, yspec, yspec,
                  pl.BlockSpec((tm, N_BRANCHES * D_MODEL), lambda i: (i, 0)),
                  pl.BlockSpec((tm, D_MODEL), lambda i: (i, 0)),
                  pl.BlockSpec((N_BRANCHES, W, D_MODEL), lambda i: (0, 0, 0)),
                  pl.BlockSpec((D_MODEL, D_MODEL), lambda i: (0, 0))],
        out_specs=pl.BlockSpec((tm, D_MODEL), lambda i: (i, 0)),
        out_shape=jax.ShapeDtypeStruct((n, D_MODEL), F32),
        compiler_params=pltpu.CompilerParams(
            dimension_semantics=("parallel",), vmem_limit_bytes=V7X_VMEM_LIMIT),
        name="merge",
    )(*ys, z, x2, wb, wo)


def _block_diag(blocks):
    nb, a, b_ = blocks.shape
    eye = jnp.eye(nb, dtype=blocks.dtype)
    return jnp.einsum('nab,nm->namb', blocks, eye).reshape(nb * a, nb * b_)


def _mean_matrix(width):
    idx = np.arange(width) // HEAD_DIM
    return jnp.asarray((idx[:, None] == idx[None, :]).astype(np.float32) / HEAD_DIM, BF16)


def _layout_w_in(w_in, qn_a, kn_a, qn_c, kn_c):
    sizes = _split_sizes()
    offs = np.concatenate([[0], np.cumsum(sizes)])
    seg = lambda k: w_in[:, int(offs[k]):int(offs[k + 1])]
    (qa, ka, va, fa, ga, xb, gb, qc, kcc, vcc, ksc, vsc, kwc, vwc, gate_c, gc, ud, vd, gd, mg) = [
        seg(k) for k in range(len(sizes))]
    dup = lambda w: jnp.concatenate([w[:, :HEAD_DIM], w[:, :HEAD_DIM], w[:, HEAD_DIM:], w[:, HEAD_DIM:]], axis=1)
    pad = jnp.zeros((D_MODEL, 256), w_in.dtype)
    w_z = jnp.concatenate([mg, qa, ka, qc, dup(ksc), dup(kwc), va, dup(vsc), dup(vwc),
                           ga, xb, gb, gc, ud, vd, gd, kcc, vcc, pad], axis=1).astype(BF16)
    scale = LOG2E / math.sqrt(HEAD_DIM)
    gain = jnp.ones((ZW,), F32)
    gain = gain.at[OFF_QA:OFF_QA + ZT].set(jnp.tile(qn_a * scale, FOX_HEADS))
    gain = gain.at[OFF_KA:OFF_KA + ZT].set(jnp.tile(kn_a, FOX_HEADS))
    gain = gain.at[OFF_QC:OFF_QC + ZT].set(jnp.tile(qn_c * scale, NSA_HEADS))
    gain = gain.at[OFF_KSD:OFF_KSD + ZT].set(jnp.tile(kn_c, ZT // HEAD_DIM))
    gate3 = gate_c.reshape(D_MODEL, 3, NSA_KV_HEADS, NSA_GROUP)
    zpad = lambda w: jnp.pad(w, ((0, 0), (0, LANES - w.shape[1])))
    w_small = jnp.concatenate(
        [zpad(fa)] + [zpad(gate3[:, :, g, :].reshape(D_MODEL, 3 * NSA_GROUP)) for g in range(NSA_KV_HEADS)],
        axis=1).astype(BF16)
    return w_z, gain.reshape(1, ZW), w_small


def _layout_compress(cmp_pos, wk1, wk2, wv1, wv2, kn_c):
    half = CMP_BLOCK // 2
    eye4 = jnp.eye(4, dtype=F32)

    def w1_half(lo):
        k = wk1.reshape(CMP_BLOCK, HEAD_DIM, CMP_HIDDEN)[lo:lo + half]
        v = wv1.reshape(CMP_BLOCK, HEAD_DIM, CMP_HIDDEN)[lo:lo + half]
        stack = jnp.stack([k, k, v, v])
        return jnp.einsum('spdn,st->psdtn', stack, eye4).reshape(half * 4 * HEAD_DIM, 4 * CMP_HIDDEN).astype(BF16)

    def pos_half(lo):
        p = cmp_pos[lo:lo + half]
        return jnp.broadcast_to(p[:, None, :], (half, 4, HEAD_DIM)).reshape(1, half * 4 * HEAD_DIM)

    w2 = jnp.zeros((4, CMP_HIDDEN, 4, 2, HEAD_DIM), F32)
    for slot, w in enumerate((wk2, wk2, wv2, wv2)):
        w2 = w2.at[slot, :, slot, :, :].set(jnp.broadcast_to(w[:, None, :], (CMP_HIDDEN, 2, HEAD_DIM)))
    w2 = w2.reshape(4 * CMP_HIDDEN, ZT).astype(BF16)
    gain = jnp.tile(kn_c, ZT // HEAD_DIM).reshape(1, ZT)
    return pos_half(0), pos_half(half), w1_half(0), w1_half(half), w2, gain


@functools.lru_cache(maxsize=None)
def _nsa_consts(s):
    nc = s // CMP_STRIDE
    ns = s // SLC_BLOCK
    e = np.zeros((s, LANES), np.float32)
    e[np.arange(s), np.arange(s) // SLC_BLOCK] = 1.0
    c0 = np.arange(nc) * CMP_STRIDE
    s0 = np.arange(ns) * SLC_BLOCK
    ovl = np.minimum(c0[:, None] + CMP_BLOCK, s0[None, :] + SLC_BLOCK) - np.maximum(c0[:, None], s0[None, :])
    ovt = np.zeros((LANES, nc), np.float32)
    ovt[:ns, :] = (np.clip(ovl, 0, None) / CMP_BLOCK).T
    ovt[:, nc - 1] = 0.0
    return e, ovt


def _tile(s, pref):
    t = pref
    while s % t:
        t //= 2
    return t


def _branches(x2, b, s, norm_g, w_in, b_forget, qn_a, kn_a, conv_w, conv_b, w_rg_a, b_rg_a, w_rg_x,
              b_rg_x, lru_lambda, qn_c, kn_c, cmp_pos, wk1, wk2, wv1, wv2, ln_v_g, w_spatial,
              b_spatial):
    n = b * s
    W = BRANCH_WIDTH
    w_z, gain_row, w_small = _layout_w_in(w_in, qn_a, kn_a, qn_c, kn_c)
    bd = _mean_matrix(ZT)
    z, small = _inproj(x2, norm_g.reshape(1, D_MODEL), w_z, gain_row, bd, w_small, _tile(n, 1024))
    vt = _vtrans(z, _tile(s, 512))

    bf_row = jnp.pad(b_forget, (0, LANES - FOX_HEADS)).reshape(1, LANES)
    caq, cak = _cprep(small, bf_row, b, s, _tile(s, 512))
    ys_a = _fox(z, vt, caq, cak, b, s, _tile(s, 512))

    ys_b = _lru(z, conv_w, conv_b.reshape(1, W), _block_diag(w_rg_a).astype(BF16), b_rg_a.reshape(1, W),
                _block_diag(w_rg_x).astype(BF16), b_rg_x.reshape(1, W), lru_lambda.reshape(1, W),
                b, s, _tile(s, 256))

    nc = s // CMP_STRIDE
    t16 = z[:, OFF_CMP:OFF_CMP + 4 * HEAD_DIM].reshape(b, nc, CMP_STRIDE * 4 * HEAD_DIM)
    ptop, pbot, w1t, w1b, w2c, cgain = _layout_compress(cmp_pos, wk1, wk2, wv1, wv2, kn_c)
    kc, vct = _compress(t16, ptop, pbot, w1t, w1b, w2c, bd, cgain, b, nc)
    e_np, ovt_np = _nsa_consts(s)
    ys_c = _nsa(z, vt, small, kc, vct, jnp.asarray(e_np, BF16), jnp.asarray(ovt_np, BF16), b, s)

    causal = jnp.tril(jnp.ones((SG_CHUNK, SG_CHUNK), F32))
    wm = (w_spatial * causal).astype(BF16)
    bs_full = jnp.broadcast_to(b_spatial.T[:, :, None], (SG_CHUNK, SG_GROUPS, HEAD_DIM)).reshape(SG_CHUNK, W)
    ys_d = _sgu(z, ln_v_g.reshape(1, W), wm, bs_full, n, _tile(s, 512))
    return (ys_a, ys_b, ys_c, ys_d), z


def _layer(x2, b, s, *params):
    ys, z = _branches(x2, b, s, *params[:-2])
    w_branch, w_out = params[-2:]
    return _merge(ys, z, x2, w_branch.astype(BF16), w_out.astype(BF16), _tile(b * s, 512))


def kernel(x, norm_g, w_in, b_forget, qn_a, kn_a, conv_w, conv_b, w_rg_a, b_rg_a, w_rg_x, b_rg_x,
           lru_lambda, qn_c, kn_c, cmp_pos, cmp_k_w1, cmp_k_w2, cmp_v_w1, cmp_v_w2, ln_v_g,
           w_spatial, b_spatial, w_branch, w_out):
    b, s, d = x.shape
    x2 = x.reshape(b * s, d)
    params = (norm_g, w_in, b_forget, qn_a, kn_a, conv_w, conv_b, w_rg_a, b_rg_a, w_rg_x, b_rg_x,
              lru_lambda, qn_c, kn_c, cmp_pos, cmp_k_w1, cmp_k_w2, cmp_v_w1, cmp_v_w2, ln_v_g,
              w_spatial, b_spatial, w_branch, w_out)
    for layer in range(norm_g.shape[0]):
        x2 = _layer(x2, b, s, *[p[layer] for p in params])
    return x2.reshape(b, s, d)
```

```python
import functools
import math

import numpy as np
import jax
import jax.numpy as jnp
from jax import lax
from jax.experimental import pallas as pl
from jax.experimental.pallas import tpu as pltpu

F32 = jnp.float32
BF16 = jnp.bfloat16

D_MODEL = 1024
HEAD_DIM = 64
BRANCH_WIDTH = 512
N_BRANCHES = 4
FOX_HEADS = 8
LRU_BLOCKS = 8
CONV_WIDTH = 4
LRU_C = 8.0
NSA_HEADS = 8
NSA_KV_HEADS = 2
NSA_GROUP = 4
CMP_BLOCK = 32
CMP_STRIDE = 16
CMP_HIDDEN = 128
SLC_BLOCK = 64
SLC_TOPK = 16
SLC_LOCAL = 2
SLC_FORCE_SCORE = 1e6
WINDOW = 512
Q_BLOCK = 128
SG_CHUNK = 128
SG_GROUPS = 8
NORM_EPS = 1e-6
NEG_INF = -1e30
SEL_NEG = -32768.0

LANES = 128
V7X_VMEM_LIMIT = 56 * 1024 * 1024
SEL_TILE = 512
LOG2E = 1.4426950408889634

ZT = 512
OFF_MG = 0
OFF_QA = 8 * ZT
OFF_KA = 9 * ZT
OFF_QC = 10 * ZT
OFF_KSD = 11 * ZT
OFF_KWD = 11 * ZT + 256
OFF_VA = 12 * ZT
OFF_VSD = 13 * ZT
OFF_VWD = 13 * ZT + 256
OFF_GA = 14 * ZT
OFF_XB = 15 * ZT
OFF_GB = 16 * ZT
OFF_GC = 17 * ZT
OFF_UD = 18 * ZT
OFF_VD = 19 * ZT
OFF_GD = 20 * ZT
OFF_CMP = 21 * ZT
ZW = 22 * ZT
IN_TN = 2 * ZT
NORM_TILES = (OFF_QA // IN_TN, OFF_QC // IN_TN)
SMALL_W = 384
VT_TILES = (OFF_VA // ZT, OFF_VSD // ZT)


def _nt(a, b):
    return lax.dot_general(a, b, (((1,), (1,)), ((), ())), preferred_element_type=F32)


def _mm(a, b):
    return jnp.dot(a, b, preferred_element_type=F32)


def _sigmoid(x):
    return 0.5 * jnp.tanh(0.5 * x) + 0.5


def _silu(x):
    return x * _sigmoid(x)


def _split_sizes():
    W, kv = BRANCH_WIDTH, NSA_KV_HEADS * HEAD_DIM
    return (W, W, W, FOX_HEADS, W, W, W, W, kv, kv, kv, kv, kv, kv, 3 * NSA_HEADS, W,
            W, W, W, N_BRANCHES * D_MODEL)


def _inproj_kernel(x_ref, g_ref, w_ref, gain_ref, bd_ref, ws_ref, z_ref, small_ref, xn_ref):
    j = pl.program_id(1)

    @pl.when(j == 0)
    def _():
        x = x_ref[...]
        ms = jnp.mean(x * x, axis=-1, keepdims=True)
        xn = ((x * lax.rsqrt(ms + NORM_EPS)) * g_ref[...]).astype(BF16)
        xn_ref[...] = xn
        small_ref[...] = _mm(xn, ws_ref[...])

    acc = _mm(xn_ref[...], w_ref[...])
    is_norm = functools.reduce(jnp.logical_or, [j == t for t in NORM_TILES])

    @pl.when(is_norm)
    def _():
        sq = (acc * acc).astype(BF16)
        ms = jnp.concatenate([_mm(sq[:, c:c + ZT], bd_ref[...]) for c in range(0, IN_TN, ZT)], axis=1)
        z_ref[...] = (acc * lax.rsqrt(ms + NORM_EPS) * gain_ref[...]).astype(BF16)

    @pl.when(jnp.logical_not(is_norm))
    def _():
        z_ref[...] = acc.astype(BF16)


def _inproj(x2, norm_g, w_z, gain_row, bd, w_small, tm):
    n = x2.shape[0]
    grid = (n // tm, ZW // IN_TN)
    return pl.pallas_call(
        _inproj_kernel,
        grid=grid,
        in_specs=[
            pl.BlockSpec((tm, D_MODEL), lambda i, j: (i, 0)),
            pl.BlockSpec((1, D_MODEL), lambda i, j: (0, 0)),
            pl.BlockSpec((D_MODEL, IN_TN), lambda i, j: (0, j)),
            pl.BlockSpec((1, IN_TN), lambda i, j: (0, j)),
            pl.BlockSpec((ZT, ZT), lambda i, j: (0, 0)),
            pl.BlockSpec((D_MODEL, SMALL_W), lambda i, j: (0, 0)),
        ],
        out_specs=[
            pl.BlockSpec((tm, IN_TN), lambda i, j: (i, j)),
            pl.BlockSpec((tm, SMALL_W), lambda i, j: (i, 0)),
        ],
        out_shape=[
            jax.ShapeDtypeStruct((n, ZW), BF16),
            jax.ShapeDtypeStruct((n, SMALL_W), F32),
        ],
        scratch_shapes=[pltpu.VMEM((tm, D_MODEL), BF16)],
        compiler_params=pltpu.CompilerParams(
            dimension_semantics=("parallel", "arbitrary"), vmem_limit_bytes=V7X_VMEM_LIMIT),
        name="inproj",
    )(x2, norm_g, w_z, gain_row, bd, w_small)


def _vtrans_kernel(z_ref, o_ref):
    o_ref[...] = z_ref[...].astype(F32).T.astype(BF16)


def _vtrans(z, tm):
    n = z.shape[0]
    step = VT_TILES[1] - VT_TILES[0]
    return pl.pallas_call(
        _vtrans_kernel,
        grid=(n // tm, len(VT_TILES)),
        in_specs=[pl.BlockSpec((tm, ZT), lambda i, j: (i, VT_TILES[0] + step * j))],
        out_specs=pl.BlockSpec((ZT, tm), lambda i, j: (j, i)),
        out_shape=jax.ShapeDtypeStruct((len(VT_TILES) * ZT, n), BF16),
        compiler_params=pltpu.CompilerParams(dimension_semantics=("parallel", "parallel")),
        name="vtrans",
    )(z)


def _cprep_kernel(fa_ref, bf_ref, pq_ref, pk_ref, oq_ref, ok_ref, caq_ref, cak_ref, carry_ref,
                  *, tt):
    @pl.when(pl.program_id(1) == 0)
    def _():
        carry_ref[...] = jnp.zeros_like(carry_ref)

    x = fa_ref[...] + bf_ref[...]
    c = jnp.minimum(x, 0.0) - jnp.log(1.0 + jnp.exp(-jnp.abs(x)))
    row = lax.broadcasted_iota(jnp.int32, c.shape, 0)
    d = 1
    while d < tt:
        c = c + jnp.where(row >= d, pltpu.roll(c, d, axis=0), 0.0)
        d *= 2
    c = c + carry_ref[...]
    carry_ref[...] = c[tt - 1:tt, :]
    c = c * LOG2E
    hi = c.astype(BF16)
    r1 = c - hi.astype(F32)
    mid = r1.astype(BF16)
    lo = (r1 - mid.astype(F32)).astype(BF16)
    parts = jnp.concatenate([hi, mid, lo], axis=1)
    caq_ref[...] = (_mm(parts, pq_ref[...]) + oq_ref[...]).astype(BF16)
    cak_ref[...] = (_mm(parts, pk_ref[...]) + ok_ref[...]).astype(BF16)


def _aug_lane(h):
    return (h // 2) * LANES + (h % 2) * 8


@functools.lru_cache(maxsize=None)
def _cprep_consts():
    pq = np.zeros((3 * LANES, 4 * LANES), np.float32)
    pk = np.zeros((3 * LANES, 4 * LANES), np.float32)
    oq = np.zeros((1, 4 * LANES), np.float32)
    ok = np.zeros((1, 4 * LANES), np.float32)
    for h in range(FOX_HEADS):
        base = _aug_lane(h)
        for part in range(3):
            pq[part * LANES + h, base + part] = 1.0
            pk[part * LANES + h, base + 3 + part] = -1.0
            oq[0, base + 3 + part] = 1.0
            ok[0, base + part] = 1.0
    return pq, pk, oq, ok


def _cprep(small, bf_row, b, s, tt):
    n = b * s
    nt = s // tt
    pq, pk, oq, ok = _cprep_consts()
    full = lambda shape: pl.BlockSpec(shape, lambda bi, ti: (0, 0))
    return pl.pallas_call(
        functools.partial(_cprep_kernel, tt=tt),
        grid=(b, nt),
        in_specs=[
            pl.BlockSpec((tt, LANES), lambda bi, ti: (bi * nt + ti, 0)),
            full((1, LANES)),
            full((3 * LANES, 4 * LANES)),
            full((3 * LANES, 4 * LANES)),
            full((1, 4 * LANES)),
            full((1, 4 * LANES)),
        ],
        out_specs=[
            pl.BlockSpec((tt, 4 * LANES), lambda bi, ti: (bi * nt + ti, 0)),
            pl.BlockSpec((tt, 4 * LANES), lambda bi, ti: (bi * nt + ti, 0)),
        ],
        out_shape=[jax.ShapeDtypeStruct((n, 4 * LANES), BF16)] * 2,
        scratch_shapes=[pltpu.VMEM((1, LANES), F32)],
        compiler_params=pltpu.CompilerParams(dimension_semantics=("parallel", "arbitrary")),
        name="cprep",
    )(small, bf_row, jnp.asarray(pq, BF16), jnp.asarray(pk, BF16), jnp.asarray(oq), jnp.asarray(ok))


def _fox_kernel(qi_ref, ki_ref, q_ref, caq_ref, k_ref, cak_ref, vt_ref, ga_ref, o_ref,
                lhs_ref, rhs_ref, m_ref, l_ref, acc_ref, *, t):
    pair = pl.program_id(1)
    qi = qi_ref[pair]
    ki = ki_ref[pair]
    n_pairs = FOX_HEADS // 2

    @pl.when(ki == 0)
    def _():
        lane = lax.broadcasted_iota(jnp.int32, (t, 2 * LANES), 1)
        head_of_lane = jnp.where(lane < LANES, lane // HEAD_DIM, (lane - LANES) // 8)
        for hp in range(n_pairs):
            cols = slice(hp * LANES, (hp + 1) * LANES)
            lhs = jnp.concatenate([q_ref[:, cols], caq_ref[:, cols]], axis=1).astype(F32)
            for hh in range(2):
                lhs_ref[2 * hp + hh] = jnp.where(head_of_lane == hh, lhs, 0.0).astype(BF16)
        m_ref[...] = jnp.full_like(m_ref, NEG_INF)
        l_ref[...] = jnp.zeros_like(l_ref)
        acc_ref[...] = jnp.zeros_like(acc_ref)

    def step(diag):
        for hp in range(n_pairs):
            cols = slice(hp * LANES, (hp + 1) * LANES)
            rhs_ref[hp] = jnp.concatenate([k_ref[:, cols], cak_ref[:, cols]], axis=1)

        s_next = _nt(rhs_ref[0], lhs_ref[0])
        for h in range(FOX_HEADS):
            s = s_next
            if h + 1 < FOX_HEADS:
                s_next = _nt(rhs_ref[(h + 1) // 2], lhs_ref[h + 1])
            if diag:
                key = lax.broadcasted_iota(jnp.int32, s.shape, 0)
                qry = lax.broadcasted_iota(jnp.int32, s.shape, 1)
                s = jnp.where(key <= qry, s, NEG_INF)
            m_prev = m_ref[h]
            m_new = jnp.maximum(m_prev, jnp.max(s, axis=0, keepdims=True))
            alpha = jnp.exp2(m_prev - m_new)
            p = jnp.exp2(s - m_new)
            l_ref[h] = alpha * l_ref[h] + jnp.sum(p, axis=0, keepdims=True)
            vt = vt_ref[h * HEAD_DIM:(h + 1) * HEAD_DIM, :]
            acc_ref[h] = alpha * acc_ref[h] + _mm(vt, p.astype(BF16))
            m_ref[h] = m_new

    @pl.when(ki < qi)
    def _():
        step(False)

    @pl.when(ki == qi)
    def _():
        step(True)
        for hp in range(n_pairs):
            cols = slice(hp * LANES, (hp + 1) * LANES)
            o_t = jnp.concatenate([acc_ref[2 * hp] / l_ref[2 * hp],
                                   acc_ref[2 * hp + 1] / l_ref[2 * hp + 1]], axis=0)
            o_ref[:, cols] = (o_t.T * _silu(ga_ref[:, cols].astype(F32))).astype(BF16)


def _fox(z, vt, caq, cak, b, s, t):
    n = b * s
    nq = s // t
    W = BRANCH_WIDTH
    qi_of = np.concatenate([np.full(q + 1, q) for q in range(nq)]).astype(np.int32)
    ki_of = np.concatenate([np.arange(q + 1) for q in range(nq)]).astype(np.int32)
    qmap = lambda off: (lambda bi, p, qi, ki: (bi * nq + qi[p], off // W))
    kmap = lambda off: (lambda bi, p, qi, ki: (bi * nq + ki[p], off // W))
    grid_spec = pltpu.PrefetchScalarGridSpec(
        num_scalar_prefetch=2,
        grid=(b, len(qi_of)),
        in_specs=[
            pl.BlockSpec((t, W), qmap(OFF_QA)),
            pl.BlockSpec((t, W), qmap(0)),
            pl.BlockSpec((t, W), kmap(OFF_KA)),
            pl.BlockSpec((t, W), kmap(0)),
            pl.BlockSpec((W, t), lambda bi, p, qi, ki: (0, bi * nq + ki[p])),
            pl.BlockSpec((t, W), qmap(OFF_GA)),
        ],
        out_specs=pl.BlockSpec((t, W), qmap(0)),
        scratch_shapes=[
            pltpu.VMEM((FOX_HEADS, t, 2 * LANES), BF16),
            pltpu.VMEM((FOX_HEADS // 2, t, 2 * LANES), BF16),
            pltpu.VMEM((FOX_HEADS, 1, t), F32),
            pltpu.VMEM((FOX_HEADS, 1, t), F32),
            pltpu.VMEM((FOX_HEADS, HEAD_DIM, t), F32),
        ],
    )
    return pl.pallas_call(
        functools.partial(_fox_kernel, t=t),
        grid_spec=grid_spec,
        out_shape=jax.ShapeDtypeStruct((n, W), BF16),
        compiler_params=pltpu.CompilerParams(
            dimension_semantics=("parallel", "arbitrary"), vmem_limit_bytes=V7X_VMEM_LIMIT),
        name="fox",
    )(jnp.asarray(qi_of), jnp.asarray(ki_of), z, caq, z, cak, vt, z)


def _lru_kernel(xb_ref, gb_ref, cw_ref, cb_ref, wa_ref, ba_ref, wx_ref, bx_ref, lam_ref, o_ref,
                xpad_ref, h_ref, *, tt):
    @pl.when(pl.program_id(1) == 0)
    def _():
        xpad_ref[0:8, :] = jnp.zeros((8, BRANCH_WIDTH), F32)
        h_ref[...] = jnp.zeros_like(h_ref)

    x = xb_ref[...].astype(F32)
    xpad_ref[8:tt + 8, :] = x
    xc = cb_ref[...] + cw_ref[3:4, :] * x
    for j in range(CONV_WIDTH - 1):
        sh = CONV_WIDTH - 1 - j
        xc = xc + cw_ref[j:j + 1, :] * xpad_ref[8 - sh:8 - sh + tt, :]
    xpad_ref[0:8, :] = x[tt - 8:tt, :]

    xcb = xc.astype(BF16)
    r = _sigmoid(_mm(xcb, wa_ref[...]) + ba_ref[...])
    ig = _sigmoid(_mm(xcb, wx_ref[...]) + bx_ref[...])
    nl = -lam_ref[...]
    softplus = jnp.maximum(nl, 0.0) + jnp.log(1.0 + jnp.exp(-jnp.abs(nl)))
    log_a = (-LRU_C) * r * softplus
    a = jnp.exp(log_a)
    bb = jnp.sqrt(1.0 - jnp.exp(2.0 * log_a)) * (ig * xc)

    row = lax.broadcasted_iota(jnp.int32, a.shape, 0)
    d = 1
    while d < tt:
        keep = row >= d
        bb = bb + a * jnp.where(keep, pltpu.roll(bb, d, axis=0), 0.0)
        a = a * jnp.where(keep, pltpu.roll(a, d, axis=0), 1.0)
        d *= 2
    h = bb + a * h_ref[...]
    h_ref[...] = h[tt - 1:tt, :]
    o_ref[...] = (h * _silu(gb_ref[...].astype(F32))).astype(BF16)


def _lru(z, conv_w, conv_b, wa_bd, b_a, wx_bd, b_x, lam, b, s, tt):
    n = b * s
    nt = s // tt
    W = BRANCH_WIDTH
    full = lambda shape: pl.BlockSpec(shape, lambda bi, ti: (0, 0))
    zb = lambda off: pl.BlockSpec((tt, W), lambda bi, ti: (bi * nt + ti, off // W))
    return pl.pallas_call(
        functools.partial(_lru_kernel, tt=tt),
        grid=(b, nt),
        in_specs=[zb(OFF_XB), zb(OFF_GB), full((CONV_WIDTH, W)), full((1, W)), full((W, W)),
                  full((1, W)), full((W, W)), full((1, W)), full((1, W))],
        out_specs=pl.BlockSpec((tt, W), lambda bi, ti: (bi * nt + ti, 0)),
        out_shape=jax.ShapeDtypeStruct((n, W), BF16),
        scratch_shapes=[pltpu.VMEM((tt + 8, W), F32), pltpu.VMEM((1, W), F32)],
        compiler_params=pltpu.CompilerParams(dimension_semantics=("parallel", "arbitrary")),
        name="lru",
    )(z, z, conv_w, conv_b, wa_bd, b_a, wx_bd, b_x, lam)


def _gelu(x):
    return 0.5 * x * (1.0 + lax.erf(x * (1.0 / math.sqrt(2.0))))


def _sgu_kernel(u_ref, v_ref, g_ref, lng_ref, wm_ref, bs_ref, o_ref, *, tr):
    u = _gelu(u_ref[...].astype(F32))
    v = _gelu(v_ref[...].astype(F32))
    mu = jnp.mean(v, axis=-1, keepdims=True)
    vc = v - mu
    vn = (vc * lax.rsqrt(jnp.mean(vc * vc, axis=-1, keepdims=True) + NORM_EPS) * lng_ref[...]).astype(BF16)
    lane = lax.broadcasted_iota(jnp.int32, (SG_CHUNK, LANES), 1)
    chunks = []
    for c in range(tr // SG_CHUNK):
        vch = vn[c * SG_CHUNK:(c + 1) * SG_CHUNK, :]
        pairs = []
        for p in range(SG_GROUPS // 2):
            vp = vch[:, p * LANES:(p + 1) * LANES]
            pairs.append(jnp.where(lane < HEAD_DIM, _mm(wm_ref[2 * p], vp), _mm(wm_ref[2 * p + 1], vp)))
        chunks.append(jnp.concatenate(pairs, axis=1) + bs_ref[...])
    mixed = jnp.concatenate(chunks, axis=0)
    o_ref[...] = (u * mixed * _silu(g_ref[...].astype(F32))).astype(BF16)


def _sgu(z, ln_g, wm, bs_full, n, tr):
    W = BRANCH_WIDTH
    zb = lambda off: pl.BlockSpec((tr, W), lambda i: (i, off // W))
    return pl.pallas_call(
        functools.partial(_sgu_kernel, tr=tr),
        grid=(n // tr,),
        in_specs=[zb(OFF_UD), zb(OFF_VD), zb(OFF_GD),
                  pl.BlockSpec((1, W), lambda i: (0, 0)),
                  pl.BlockSpec((SG_GROUPS, SG_CHUNK, SG_CHUNK), lambda i: (0, 0, 0)),
                  pl.BlockSpec((SG_CHUNK, W), lambda i: (0, 0))],
        out_specs=pl.BlockSpec((tr, W), lambda i: (i, 0)),
        out_shape=jax.ShapeDtypeStruct((n, W), BF16),
        compiler_params=pltpu.CompilerParams(dimension_semantics=("parallel",)),
        name="sgu",
    )(z, z, z, ln_g, wm, bs_full)


def _compress_kernel(t16_ref, ptop_ref, pbot_ref, w1t_ref, w1b_ref, w2_ref, bd_ref, gain_ref,
                     kc_ref, vct_ref, *, nc):
    t16 = t16_ref[...].astype(F32)
    a = _mm((t16 + ptop_ref[...]).astype(BF16), w1t_ref[...])
    bm = _mm((t16 + pbot_ref[...]).astype(BF16), w1b_ref[...])
    h = a + pltpu.roll(bm, nc - 1, axis=0)
    out = _mm(_silu(h).astype(BF16), w2_ref[...])
    ms = _mm((out * out).astype(BF16), bd_ref[...])
    kn = out * lax.rsqrt(ms + NORM_EPS) * gain_ref[...]
    out_t = out.T
    for g in range(NSA_KV_HEADS):
        kc_ref[g] = kn[:, g * LANES:(g + 1) * LANES].astype(BF16)
        vct_ref[g] = out_t[(2 + g) * LANES:(2 + g) * LANES + HEAD_DIM, :].astype(BF16)


def _compress(t16, ptop, pbot, w1t, w1b, w2, bd, gain, b, nc):
    kw = 16 * 4 * HEAD_DIM
    full = lambda shape: pl.BlockSpec(shape, lambda bi: tuple(0 for _ in shape))
    return pl.pallas_call(
        functools.partial(_compress_kernel, nc=nc),
        grid=(b,),
        in_specs=[pl.BlockSpec((None, nc, kw), lambda bi: (bi, 0, 0)),
                  full((1, kw)), full((1, kw)), full((kw, 4 * CMP_HIDDEN)), full((kw, 4 * CMP_HIDDEN)),
                  full((4 * CMP_HIDDEN, ZT)), full((ZT, ZT)), full((1, ZT))],
        out_specs=[pl.BlockSpec((None, NSA_KV_HEADS, nc, LANES), lambda bi: (bi, 0, 0, 0)),
                   pl.BlockSpec((None, NSA_KV_HEADS, HEAD_DIM, nc), lambda bi: (bi, 0, 0, 0))],
        out_shape=[jax.ShapeDtypeStruct((b, NSA_KV_HEADS, nc, LANES), BF16),
                   jax.ShapeDtypeStruct((b, NSA_KV_HEADS, HEAD_DIM, nc), BF16)],
        compiler_params=pltpu.CompilerParams(
            dimension_semantics=("parallel",), vmem_limit_bytes=V7X_VMEM_LIMIT),
        name="compress",
    )(t16, ptop, pbot, w1t, w1b, w2, bd, gain)


def _softmax_cols(s, mask):
    sm = jnp.where(mask, s, NEG_INF)
    m = jnp.max(sm, axis=0, keepdims=True)
    p = jnp.exp2(sm - jnp.where(m > 0.5 * NEG_INF, m, 0.0))
    l = jnp.sum(p, axis=0, keepdims=True)
    return p, jnp.where(l > 0.0, 1.0 / l, 0.0)


def _nsa_kernel(q_ref, gc_ref, ks_ref, kw_ref, vs_ref, vw_ref, kc_ref, vct_ref, gt_ref, e_ref, ovt_ref,
                o_ref, kse_ref, sa_ref, sb_ref, sc_ref, selt_ref, m_ref, l_ref, acc_ref, *, s_len, nc):
    i = pl.program_id(2)
    qb, cols = Q_BLOCK, NSA_GROUP * Q_BLOCK
    ns = s_len // SLC_BLOCK
    top_k = min(SLC_TOPK, ns)
    kt_w = min(SEL_TILE, s_len)
    span = min(WINDOW + Q_BLOCK, s_len)
    lane = lax.broadcasted_iota(jnp.int32, (qb, LANES), 1)

    @pl.when(i == 0)
    def _():
        kse_ref[:, 0:LANES] = ks_ref[...]
        kse_ref[:, LANES:2 * LANES] = e_ref[...]

    parts = []
    for r in range(NSA_GROUP):
        blk = q_ref[:, (r // 2) * LANES:(r // 2 + 1) * LANES].astype(F32)
        parts.append(jnp.where(lane // HEAD_DIM == r % 2, blk, 0.0).astype(BF16))
    qs = jnp.concatenate(parts, axis=0)
    tcol = i * qb + lax.broadcasted_iota(jnp.int32, (1, cols), 1) % qb

    s_c = _nt(kc_ref[...], qs)
    start = pl.multiple_of(jnp.maximum(i - WINDOW // qb, 0) * qb, qb)
    s_w = _nt(kw_ref[pl.ds(start, span), :], qs)

    cend = lax.broadcasted_iota(jnp.int32, s_c.shape, 0) * CMP_STRIDE + (CMP_BLOCK - 1)
    p_c, inv_c = _softmax_cols(s_c, cend <= tcol)
    p_c = p_c * inv_c
    o_c = _mm(vct_ref[...], p_c.astype(BF16))

    ps = p_c[:, 0:qb] + p_c[:, qb:2 * qb] + p_c[:, 2 * qb:3 * qb] + p_c[:, 3 * qb:4 * qb]
    ps_hi = ps.astype(BF16)
    ps_lo = (ps - ps_hi.astype(F32)).astype(BF16)
    imp = _mm(ovt_ref[...], ps_hi) + _mm(ovt_ref[...], ps_lo)

    jrow = lax.broadcasted_iota(jnp.int32, (LANES, qb), 0)
    jt = (i * qb + lax.broadcasted_iota(jnp.int32, (LANES, qb), 1)) // SLC_BLOCK
    valid = (jrow <= jt) & (jrow < ns)
    forced = (jrow == 0) | (valid & (jrow > jt - SLC_LOCAL))
    score = jnp.where(forced, SLC_FORCE_SCORE, jnp.where(valid, imp, -1.0))
    sc_ref[...] = jnp.where(jrow < ns, score, -2.0)

    def rank_blocks(n_cand):
        for k in range(n_cand // 8):
            mine = sc_ref[8 * k:8 * k + 8, :]
            sub = 8 * k + lax.broadcasted_iota(jnp.int32, (8, qb), 0)
            cnt = jnp.zeros((8, qb), F32)
            for jp in range(n_cand):
                other = sc_ref[jp:jp + 1, :]
                if jp < 8 * k:
                    beats = other >= mine
                elif jp >= 8 * k + 8:
                    beats = other > mine
                else:
                    beats = (other > mine) | ((other == mine) & (sub > jp))
                cnt = cnt + jnp.where(beats, 1.0, 0.0)
            selt_ref[8 * k:8 * k + 8, :] = jnp.where((cnt < top_k) & (mine >= 0.0), 0.0, SEL_NEG)
        if n_cand < LANES:
            selt_ref[n_cand:LANES, :] = jnp.full((LANES - n_cand, qb), SEL_NEG, F32)

    cand_step = 16
    blocks_per_qb = qb // SLC_BLOCK
    classes = list(range(cand_step, ns, cand_step)) + [ns]
    lo_i = 0
    for n_cand in classes:
        hi_i = n_cand // blocks_per_qb if n_cand < ns else s_len // qb
        if n_cand <= top_k:
            @pl.when((i >= lo_i) & (i < hi_i))
            def _():
                selt_ref[...] = jnp.where(valid, 0.0, SEL_NEG)
        else:
            pl.when((i >= lo_i) & (i < hi_i))(functools.partial(rank_blocks, n_cand))
        lo_i = hi_i
    selb = selt_ref[...].T.astype(BF16)
    lhs = jnp.concatenate([qs, jnp.concatenate([selb] * NSA_GROUP, axis=0)], axis=1)

    def sel_scores(kt):
        k0 = pl.multiple_of(kt * kt_w, kt_w)
        return _nt(kse_ref[pl.ds(k0, kt_w), :], lhs)

    def sel_update(kt, s):
        k0 = pl.multiple_of(kt * kt_w, kt_w)
        m_prev = m_ref[...]
        m_new = jnp.maximum(m_prev, jnp.max(s, axis=0, keepdims=True))
        alpha = jnp.exp2(m_prev - m_new)
        pr = jnp.exp2(s - m_new)
        l_ref[...] = alpha * l_ref[...] + jnp.sum(pr, axis=0, keepdims=True)
        acc_ref[...] = alpha * acc_ref[...] + _mm(vs_ref[0:HEAD_DIM, pl.ds(k0, kt_w)], pr.astype(BF16))
        m_ref[...] = m_new

    m_ref[...] = jnp.full_like(m_ref, NEG_INF)
    l_ref[...] = jnp.zeros_like(l_ref)
    acc_ref[...] = jnp.zeros_like(acc_ref)
    n_full = (i * qb) // kt_w
    sa_ref[...] = sel_scores(0)

    pos = start + lax.broadcasted_iota(jnp.int32, s_w.shape, 0)
    p_w, inv_w = _softmax_cols(s_w, (pos <= tcol) & (pos > tcol - WINDOW))
    o_w = _mm(vw_ref[0:HEAD_DIM, pl.ds(start, span)], p_w.astype(BF16)) * inv_w

    def body(k2, carry):
        sb_ref[...] = sel_scores(2 * k2 + 1)
        sel_update(2 * k2, sa_ref[...])
        sa_ref[...] = sel_scores(2 * k2 + 2)
        sel_update(2 * k2 + 1, sb_ref[...])
        return carry

    lax.fori_loop(0, n_full // 2, body, 0)

    def last_update(s_last):
        pos_k = n_full * kt_w + lax.broadcasted_iota(jnp.int32, s_last.shape, 0)
        sel_update(n_full, jnp.where(pos_k <= tcol, s_last, NEG_INF))

    @pl.when(n_full % 2 == 1)
    def _():
        sb_ref[...] = sel_scores(n_full)
        sel_update(n_full - 1, sa_ref[...])
        last_update(sb_ref[...])

    @pl.when(n_full % 2 == 0)
    def _():
        last_update(sa_ref[...])

    o_s = acc_ref[...] * (1.0 / l_ref[...])

    gts = _sigmoid(gt_ref[...]).T
    heads = []
    for r in range(NSA_GROUP):
        cs = slice(r * qb, (r + 1) * qb)
        heads.append(gts[r:r + 1, :] * o_c[:, cs] + gts[4 + r:5 + r, :] * o_s[:, cs]
                     + gts[8 + r:9 + r, :] * o_w[:, cs])
    y = jnp.concatenate(heads, axis=0).T
    o_ref[...] = (y * _silu(gc_ref[...].astype(F32))).astype(BF16)


def _nsa(z, vt, small, kc, vct, e128, ovt, b, s):
    n = b * s
    nqb = s // Q_BLOCK
    nc = s // CMP_STRIDE
    gw = NSA_GROUP * HEAD_DIM
    cols = NSA_GROUP * Q_BLOCK
    qspec = lambda off: pl.BlockSpec((Q_BLOCK, gw), lambda bi, g, i: (bi * nqb + i, off // gw + g))
    seq = lambda off: pl.BlockSpec((s, LANES), lambda bi, g, i: (bi, off // LANES + g))
    vseq = lambda row: pl.BlockSpec((LANES, s), lambda bi, g, i: (row // LANES + g, bi))
    return pl.pallas_call(
        functools.partial(_nsa_kernel, s_len=s, nc=nc),
        grid=(b, NSA_KV_HEADS, nqb),
        in_specs=[qspec(OFF_QC), qspec(OFF_GC), seq(OFF_KSD), seq(OFF_KWD),
                  vseq(ZT), vseq(ZT + 256),
                  pl.BlockSpec((None, None, nc, LANES), lambda bi, g, i: (bi, g, 0, 0)),
                  pl.BlockSpec((None, None, HEAD_DIM, nc), lambda bi, g, i: (bi, g, 0, 0)),
                  pl.BlockSpec((Q_BLOCK, LANES), lambda bi, g, i: (bi * nqb + i, 1 + g)),
                  pl.BlockSpec((s, LANES), lambda bi, g, i: (0, 0)),
                  pl.BlockSpec((LANES, nc), lambda bi, g, i: (0, 0))],
        out_specs=pl.BlockSpec((Q_BLOCK, gw), lambda bi, g, i: (bi * nqb + i, g)),
        out_shape=jax.ShapeDtypeStruct((n, BRANCH_WIDTH), BF16),
        scratch_shapes=[pltpu.VMEM((s, 2 * LANES), BF16),
                        pltpu.VMEM((min(SEL_TILE, s), cols), F32),
                        pltpu.VMEM((min(SEL_TILE, s), cols), F32),
                        pltpu.VMEM((LANES, Q_BLOCK), F32), pltpu.VMEM((LANES, Q_BLOCK), F32),
                        pltpu.VMEM((1, cols), F32),
                        pltpu.VMEM((1, cols), F32), pltpu.VMEM((HEAD_DIM, cols), F32)],
        compiler_params=pltpu.CompilerParams(
            dimension_semantics=("parallel", "parallel", "arbitrary"),
            vmem_limit_bytes=V7X_VMEM_LIMIT),
        name="nsa",
    )(z, z, z, z, vt, vt, kc, vct, small, e128, ovt)


def _merge_kernel(ya_ref, yb_ref, yc_ref, yd_ref, mg_ref, x_ref, wb_ref, wo_ref, o_ref):
    merged = None
    for nb, y_ref in enumerate((ya_ref, yb_ref, yc_ref, yd_ref)):
        gate = _sigmoid(mg_ref[:, nb * D_MODEL:(nb + 1) * D_MODEL].astype(F32))
        term = gate * _mm(y_ref[...], wb_ref[nb])
        merged = term if merged is None else merged + term
    o_ref[...] = x_ref[...] + _mm(merged.astype(BF16), wo_ref[...])


def _merge(ys, z, x2, wb, wo, tm):
    n = x2.shape[0]
    W = BRANCH_WIDTH
    yspec = pl.BlockSpec((tm, W), lambda i: (i, 0))
    return pl.pallas_call(
        _merge_kernel,
        grid=(n // tm,),
        in_specs=[yspec, yspec, yspec, yspec,
                  pl.BlockSpec((tm, N_BRANCHES * D_MODEL), lambda i: (i, 0)),
                  pl.BlockSpec((tm, D_MODEL), lambda i: (i, 0)),
                  pl.BlockSpec((N_BRANCHES, W, D_MODEL), lambda i: (0, 0, 0)),
                  pl.BlockSpec((D_MODEL, D_MODEL), lambda i: (0, 0))],
        out_specs=pl.BlockSpec((tm, D_MODEL), lambda i: (i, 0)),
        out_shape=jax.ShapeDtypeStruct((n, D_MODEL), F32),
        compiler_params=pltpu.CompilerParams(
            dimension_semantics=("parallel",), vmem_limit_bytes=V7X_VMEM_LIMIT),
        name="merge",
    )(*ys, z, x2, wb, wo)


def _block_diag(blocks):
    nb, a, b_ = blocks.shape
    eye = jnp.eye(nb, dtype=blocks.dtype)
    return jnp.einsum('nab,nm->namb', blocks, eye).reshape(nb * a, nb * b_)


def _mean_matrix(width):
    idx = np.arange(width) // HEAD_DIM
    return jnp.asarray((idx[:, None] == idx[None, :]).astype(np.float32) / HEAD_DIM, BF16)


def _layout_w_in(w_in, qn_a, kn_a, qn_c, kn_c):
    sizes = _split_sizes()
    offs = np.concatenate([[0], np.cumsum(sizes)])
    seg = lambda k: w_in[:, int(offs[k]):int(offs[k + 1])]
    (qa, ka, va, fa, ga, xb, gb, qc, kcc, vcc, ksc, vsc, kwc, vwc, gate_c, gc, ud, vd, gd, mg) = [
        seg(k) for k in range(len(sizes))]
    dup = lambda w: jnp.concatenate([w[:, :HEAD_DIM], w[:, :HEAD_DIM], w[:, HEAD_DIM:], w[:, HEAD_DIM:]], axis=1)
    pad = jnp.zeros((D_MODEL, 256), w_in.dtype)
    w_z = jnp.concatenate([mg, qa, ka, qc, dup(ksc), dup(kwc), va, dup(vsc), dup(vwc),
                           ga, xb, gb, gc, ud, vd, gd, kcc, vcc, pad], axis=1).astype(BF16)
    scale = LOG2E / math.sqrt(HEAD_DIM)
    gain = jnp.ones((ZW,), F32)
    gain = gain.at[OFF_QA:OFF_QA + ZT].set(jnp.tile(qn_a * scale, FOX_HEADS))
    gain = gain.at[OFF_KA:OFF_KA + ZT].set(jnp.tile(kn_a, FOX_HEADS))
    gain = gain.at[OFF_QC:OFF_QC + ZT].set(jnp.tile(qn_c * scale, NSA_HEADS))
    gain = gain.at[OFF_KSD:OFF_KSD + ZT].set(jnp.tile(kn_c, ZT // HEAD_DIM))
    gate3 = gate_c.reshape(D_MODEL, 3, NSA_KV_HEADS, NSA_GROUP)
    zpad = lambda w: jnp.pad(w, ((0, 0), (0, LANES - w.shape[1])))
    w_small = jnp.concatenate(
        [zpad(fa)] + [zpad(gate3[:, :, g, :].reshape(D_MODEL, 3 * NSA_GROUP)) for g in range(NSA_KV_HEADS)],
        axis=1).astype(BF16)
    return w_z, gain.reshape(1, ZW), w_small


def _layout_compress(cmp_pos, wk1, wk2, wv1, wv2, kn_c):
    half = CMP_BLOCK // 2
    eye4 = jnp.eye(4, dtype=F32)

    def w1_half(lo):
        k = wk1.reshape(CMP_BLOCK, HEAD_DIM, CMP_HIDDEN)[lo:lo + half]
        v = wv1.reshape(CMP_BLOCK, HEAD_DIM, CMP_HIDDEN)[lo:lo + half]
        stack = jnp.stack([k, k, v, v])
        return jnp.einsum('spdn,st->psdtn', stack, eye4).reshape(half * 4 * HEAD_DIM, 4 * CMP_HIDDEN).astype(BF16)

    def pos_half(lo):
        p = cmp_pos[lo:lo + half]
        return jnp.broadcast_to(p[:, None, :], (half, 4, HEAD_DIM)).reshape(1, half * 4 * HEAD_DIM)

    w2 = jnp.zeros((4, CMP_HIDDEN, 4, 2, HEAD_DIM), F32)
    for slot, w in enumerate((wk2, wk2, wv2, wv2)):
        w2 = w2.at[slot, :, slot, :, :].set(jnp.broadcast_to(w[:, None, :], (CMP_HIDDEN, 2, HEAD_DIM)))
    w2 = w2.reshape(4 * CMP_HIDDEN, ZT).astype(BF16)
    gain = jnp.tile(kn_c, ZT // HEAD_DIM).reshape(1, ZT)
    return pos_half(0), pos_half(half), w1_half(0), w1_half(half), w2, gain


@functools.lru_cache(maxsize=None)
def _nsa_consts(s):
    nc = s // CMP_STRIDE
    ns = s // SLC_BLOCK
    e = np.zeros((s, LANES), np.float32)
    e[np.arange(s), np.arange(s) // SLC_BLOCK] = 1.0
    c0 = np.arange(nc) * CMP_STRIDE
    s0 = np.arange(ns) * SLC_BLOCK
    ovl = np.minimum(c0[:, None] + CMP_BLOCK, s0[None, :] + SLC_BLOCK) - np.maximum(c0[:, None], s0[None, :])
    ovt = np.zeros((LANES, nc), np.float32)
    ovt[:ns, :] = (np.clip(ovl, 0, None) / CMP_BLOCK).T
    ovt[:, nc - 1] = 0.0
    return e, ovt


def _tile(s, pref):
    t = pref
    while s % t:
        t //= 2
    return t


def _branches(x2, b, s, norm_g, w_in, b_forget, qn_a, kn_a, conv_w, conv_b, w_rg_a, b_rg_a, w_rg_x,
              b_rg_x, lru_lambda, qn_c, kn_c, cmp_pos, wk1, wk2, wv1, wv2, ln_v_g, w_spatial,
              b_spatial):
    n = b * s
    W = BRANCH_WIDTH
    w_z, gain_row, w_small = _layout_w_in(w_in, qn_a, kn_a, qn_c, kn_c)
    bd = _mean_matrix(ZT)
    z, small = _inproj(x2, norm_g.reshape(1, D_MODEL), w_z, gain_row, bd, w_small, _tile(n, 1024))
    vt = _vtrans(z, _tile(s, 512))

    bf_row = jnp.pad(b_forget, (0, LANES - FOX_HEADS)).reshape(1, LANES)
    caq, cak = _cprep(small, bf_row, b, s, _tile(s, 512))
    ys_a = _fox(z, vt, caq, cak, b, s, _tile(s, 512))

    ys_b = _lru(z, conv_w, conv_b.reshape(1, W), _block_diag(w_rg_a).astype(BF16), b_rg_a.reshape(1, W),
                _block_diag(w_rg_x).astype(BF16), b_rg_x.reshape(1, W), lru_lambda.reshape(1, W),
                b, s, _tile(s, 256))

    nc = s // CMP_STRIDE
    t16 = z[:, OFF_CMP:OFF_CMP + 4 * HEAD_DIM].reshape(b, nc, CMP_STRIDE * 4 * HEAD_DIM)
    ptop, pbot, w1t, w1b, w2c, cgain = _layout_compress(cmp_pos, wk1, wk2, wv1, wv2, kn_c)
    kc, vct = _compress(t16, ptop, pbot, w1t, w1b, w2c, bd, cgain, b, nc)
    e_np, ovt_np = _nsa_consts(s)
    ys_c = _nsa(z, vt, small, kc, vct, jnp.asarray(e_np, BF16), jnp.asarray(ovt_np, BF16), b, s)

    causal = jnp.tril(jnp.ones((SG_CHUNK, SG_CHUNK), F32))
    wm = (w_spatial * causal).astype(BF16)
    bs_full = jnp.broadcast_to(b_spatial.T[:, :, None], (SG_CHUNK, SG_GROUPS, HEAD_DIM)).reshape(SG_CHUNK, W)
    ys_d = _sgu(z, ln_v_g.reshape(1, W), wm, bs_full, n, _tile(s, 512))
    return (ys_a, ys_b, ys_c, ys_d), z


def _layer(x2, b, s, *params):
    ys, z = _branches(x2, b, s, *params[:-2])
    w_branch, w_out = params[-2:]
    return _merge(ys, z, x2, w_branch.astype(BF16), w_out.astype(BF16), _tile(b * s, 512))


def kernel(x, norm_g, w_in, b_forget, qn_a, kn_a, conv_w, conv_b, w_rg_a, b_rg_a, w_rg_x, b_rg_x,
           lru_lambda, qn_c, kn_c, cmp_pos, cmp_k_w1, cmp_k_w2, cmp_v_w1, cmp_v_w2, ln_v_g,
           w_spatial, b_spatial, w_branch, w_out):
    b, s, d = x.shape
    x2 = x.reshape(b * s, d)
    params = (norm_g, w_in, b_forget, qn_a, kn_a, conv_w, conv_b, w_rg_a, b_rg_a, w_rg_x, b_rg_x,
              lru_lambda, qn_c, kn_c, cmp_pos, cmp_k_w1, cmp_k_w2, cmp_v_w1, cmp_v_w2, ln_v_g,
              w_spatial, b_spatial, w_branch, w_out)
    for layer in range(norm_g.shape[0]):
        x2 = _layer(x2, b, s, *[p[layer] for p in params])
    return x2.reshape(b, s, d)
```

```python
import functools
import math

import numpy as np
import jax
import jax.numpy as jnp
from jax import lax
from jax.experimental import pallas as pl
from jax.experimental.pallas import tpu as pltpu

F32 = jnp.float32
BF16 = jnp.bfloat16

D_MODEL = 1024
HEAD_DIM = 64
BRANCH_WIDTH = 512
N_BRANCHES = 4
FOX_HEADS = 8
LRU_BLOCKS = 8
CONV_WIDTH = 4
LRU_C = 8.0
NSA_HEADS = 8
NSA_KV_HEADS = 2
NSA_GROUP = 4
CMP_BLOCK = 32
CMP_STRIDE = 16
CMP_HIDDEN = 128
SLC_BLOCK = 64
SLC_TOPK = 16
SLC_LOCAL = 2
SLC_FORCE_SCORE = 1e6
WINDOW = 512
Q_BLOCK = 128
SG_CHUNK = 128
SG_GROUPS = 8
NORM_EPS = 1e-6
NEG_INF = -1e30
SEL_NEG = -32768.0

LANES = 128
V7X_VMEM_LIMIT = 56 * 1024 * 1024
SEL_TILE = 512
LOG2E = 1.4426950408889634

ZT = 512
OFF_MG = 0
OFF_QA = 8 * ZT
OFF_KA = 9 * ZT
OFF_QC = 10 * ZT
OFF_KSD = 11 * ZT
OFF_KWD = 11 * ZT + 256
OFF_VA = 12 * ZT
OFF_VSD = 13 * ZT
OFF_VWD = 13 * ZT + 256
OFF_GA = 14 * ZT
OFF_XB = 15 * ZT
OFF_GB = 16 * ZT
OFF_GC = 17 * ZT
OFF_UD = 18 * ZT
OFF_VD = 19 * ZT
OFF_GD = 20 * ZT
OFF_CMP = 21 * ZT
ZW = 22 * ZT
IN_TN = 2 * ZT
NORM_TILES = (OFF_QA // IN_TN, OFF_QC // IN_TN)
SMALL_W = 384
VT_TILES = (OFF_VA // ZT, OFF_VSD // ZT)


def _mm(a, b):
    return jnp.dot(a, b, preferred_element_type=F32)


def _sigmoid(x):
    return 0.5 * jnp.tanh(0.5 * x) + 0.5


def _silu(x):
    return x * _sigmoid(x)


def _split_sizes():
    W, kv = BRANCH_WIDTH, NSA_KV_HEADS * HEAD_DIM
    return (W, W, W, FOX_HEADS, W, W, W, W, kv, kv, kv, kv, kv, kv, 3 * NSA_HEADS, W,
            W, W, W, N_BRANCHES * D_MODEL)


def _inproj_kernel(x_ref, g_ref, w_ref, gain_ref, bd_ref, ws_ref, z_ref, small_ref, xn_ref):
    j = pl.program_id(1)

    @pl.when(j == 0)
    def _():
        x = x_ref[...]
        ms = jnp.mean(x * x, axis=-1, keepdims=True)
        xn = ((x * lax.rsqrt(ms + NORM_EPS)) * g_ref[...]).astype(BF16)
        xn_ref[...] = xn
        small_ref[...] = _mm(xn, ws_ref[...])

    acc = _mm(xn_ref[...], w_ref[...])
    is_norm = functools.reduce(jnp.logical_or, [j == t for t in NORM_TILES])

    @pl.when(is_norm)
    def _():
        sq = (acc * acc).astype(BF16)
        ms = jnp.concatenate([_mm(sq[:, c:c + ZT], bd_ref[...]) for c in range(0, IN_TN, ZT)], axis=1)
        z_ref[...] = (acc * lax.rsqrt(ms + NORM_EPS) * gain_ref[...]).astype(BF16)

    @pl.when(jnp.logical_not(is_norm))
    def _():
        z_ref[...] = acc.astype(BF16)


def _inproj(x2, norm_g, w_z, gain_row, bd, w_small, tm):
    n = x2.shape[0]
    grid = (n // tm, ZW // IN_TN)
    return pl.pallas_call(
        _inproj_kernel,
        grid=grid,
        in_specs=[
            pl.BlockSpec((tm, D_MODEL), lambda i, j: (i, 0)),
            pl.BlockSpec((1, D_MODEL), lambda i, j: (0, 0)),
            pl.BlockSpec((D_MODEL, IN_TN), lambda i, j: (0, j)),
            pl.BlockSpec((1, IN_TN), lambda i, j: (0, j)),
            pl.BlockSpec((ZT, ZT), lambda i, j: (0, 0)),
            pl.BlockSpec((D_MODEL, SMALL_W), lambda i, j: (0, 0)),
        ],
        out_specs=[
            pl.BlockSpec((tm, IN_TN), lambda i, j: (i, j)),
            pl.BlockSpec((tm, SMALL_W), lambda i, j: (i, 0)),
        ],
        out_shape=[
            jax.ShapeDtypeStruct((n, ZW), BF16),
            jax.ShapeDtypeStruct((n, SMALL_W), F32),
        ],
        scratch_shapes=[pltpu.VMEM((tm, D_MODEL), BF16)],
        compiler_params=pltpu.CompilerParams(
            dimension_semantics=("parallel", "arbitrary"), vmem_limit_bytes=V7X_VMEM_LIMIT),
        name="inproj",
    )(x2, norm_g, w_z, gain_row, bd, w_small)


def _vtrans_kernel(z_ref, o_ref):
    o_ref[...] = z_ref[...].astype(F32).T.astype(BF16)


def _vtrans(z, tm):
    n = z.shape[0]
    step = VT_TILES[1] - VT_TILES[0]
    return pl.pallas_call(
        _vtrans_kernel,
        grid=(n // tm, len(VT_TILES)),
        in_specs=[pl.BlockSpec((tm, ZT), lambda i, j: (i, VT_TILES[0] + step * j))],
        out_specs=pl.BlockSpec((ZT, tm), lambda i, j: (j, i)),
        out_shape=jax.ShapeDtypeStruct((len(VT_TILES) * ZT, n), BF16),
        compiler_params=pltpu.CompilerParams(dimension_semantics=("parallel", "parallel")),
        name="vtrans",
    )(z)


def _cprep_kernel(fa_ref, bf_ref, pq_ref, pk_ref, oq_ref, ok_ref, caq_ref, cak_ref, carry_ref,
                  *, tt):
    @pl.when(pl.program_id(1) == 0)
    def _():
        carry_ref[...] = jnp.zeros_like(carry_ref)

    x = fa_ref[...] + bf_ref[...]
    c = jnp.minimum(x, 0.0) - jnp.log(1.0 + jnp.exp(-jnp.abs(x)))
    row = lax.broadcasted_iota(jnp.int32, c.shape, 0)
    d = 1
    while d < tt:
        c = c + jnp.where(row >= d, pltpu.roll(c, d, axis=0), 0.0)
        d *= 2
    c = c + carry_ref[...]
    carry_ref[...] = c[tt - 1:tt, :]
    c = c * LOG2E
    hi = c.astype(BF16)
    r1 = c - hi.astype(F32)
    mid = r1.astype(BF16)
    lo = (r1 - mid.astype(F32)).astype(BF16)
    parts = jnp.concatenate([hi, mid, lo], axis=1)
    caq_ref[...] = (_mm(parts, pq_ref[...]) + oq_ref[...]).astype(BF16)
    cak_ref[...] = (_mm(parts, pk_ref[...]) + ok_ref[...]).astype(BF16)


def _aug_lane(h):
    return (h // 2) * LANES + (h % 2) * 8


@functools.lru_cache(maxsize=None)
def _cprep_consts():
    pq = np.zeros((3 * LANES, 4 * LANES), np.float32)
    pk = np.zeros((3 * LANES, 4 * LANES), np.float32)
    oq = np.zeros((1, 4 * LANES), np.float32)
    ok = np.zeros((1, 4 * LANES), np.float32)
    for h in range(FOX_HEADS):
        base = _aug_lane(h)
        for part in range(3):
            pq[part * LANES + h, base + part] = 1.0
            pk[part * LANES + h, base + 3 + part] = -1.0
            oq[0, base + 3 + part] = 1.0
            ok[0, base + part] = 1.0
    return pq, pk, oq, ok


def _cprep(small, bf_row, b, s, tt):
    n = b * s
    nt = s // tt
    pq, pk, oq, ok = _cprep_consts()
    full = lambda shape: pl.BlockSpec(shape, lambda bi, ti: (0, 0))
    return pl.pallas_call(
        functools.partial(_cprep_kernel, tt=tt),
        grid=(b, nt),
        in_specs=[
            pl.BlockSpec((tt, LANES), lambda bi, ti: (bi * nt + ti, 0)),
            full((1, LANES)),
            full((3 * LANES, 4 * LANES)),
            full((3 * LANES, 4 * LANES)),
            full((1, 4 * LANES)),
            full((1, 4 * LANES)),
        ],
        out_specs=[
            pl.BlockSpec((tt, 4 * LANES), lambda bi, ti: (bi * nt + ti, 0)),
            pl.BlockSpec((tt, 4 * LANES), lambda bi, ti: (bi * nt + ti, 0)),
        ],
        out_shape=[jax.ShapeDtypeStruct((n, 4 * LANES), BF16)] * 2,
        scratch_shapes=[pltpu.VMEM((1, LANES), F32)],
        compiler_params=pltpu.CompilerParams(dimension_semantics=("parallel", "arbitrary")),
        name="cprep",
    )(small, bf_row, jnp.asarray(pq, BF16), jnp.asarray(pk, BF16), jnp.asarray(oq), jnp.asarray(ok))


def _fox_kernel(qi_ref, ki_ref, q_ref, caq_ref, k_ref, cak_ref, vt_ref, ga_ref, o_ref,
                lhs_ref, rhs_ref, m_ref, l_ref, acc_ref, *, t):
    pair = pl.program_id(1)
    qi = qi_ref[pair]
    ki = ki_ref[pair]
    n_pairs = FOX_HEADS // 2

    @pl.when(ki == 0)
    def _():
        lane = lax.broadcasted_iota(jnp.int32, (t, 2 * LANES), 1)
        head_of_lane = jnp.where(lane < LANES, lane // HEAD_DIM, (lane - LANES) // 8)
        for hp in range(n_pairs):
            cols = slice(hp * LANES, (hp + 1) * LANES)
            lhs = jnp.concatenate([q_ref[:, cols], caq_ref[:, cols]], axis=1).astype(F32)
            for hh in range(2):
                lhs_ref[2 * hp + hh] = jnp.where(head_of_lane == hh, lhs, 0.0).T.astype(BF16)
        m_ref[...] = jnp.full_like(m_ref, NEG_INF)
        l_ref[...] = jnp.zeros_like(l_ref)
        acc_ref[...] = jnp.zeros_like(acc_ref)

    def step(diag):
        for hp in range(n_pairs):
            cols = slice(hp * LANES, (hp + 1) * LANES)
            rhs_ref[hp] = jnp.concatenate([k_ref[:, cols], cak_ref[:, cols]], axis=1)

        s_next = _mm(rhs_ref[0], lhs_ref[0])
        for h in range(FOX_HEADS):
            s = s_next
            if h + 1 < FOX_HEADS:
                s_next = _mm(rhs_ref[(h + 1) // 2], lhs_ref[h + 1])
            if diag:
                key = lax.broadcasted_iota(jnp.int32, s.shape, 0)
                qry = lax.broadcasted_iota(jnp.int32, s.shape, 1)
                s = jnp.where(key <= qry, s, NEG_INF)
            m_prev = m_ref[h]
            m_new = jnp.maximum(m_prev, jnp.max(s, axis=0, keepdims=True))
            alpha = jnp.exp2(m_prev - m_new)
            p = jnp.exp2(s - m_new)
            l_ref[h] = alpha * l_ref[h] + jnp.sum(p, axis=0, keepdims=True)
            vt = vt_ref[h * HEAD_DIM:(h + 1) * HEAD_DIM, :]
            acc_ref[h] = alpha * acc_ref[h] + _mm(vt, p.astype(BF16))
            m_ref[h] = m_new

    @pl.when(ki < qi)
    def _():
        step(False)

    @pl.when(ki == qi)
    def _():
        step(True)
        for hp in range(n_pairs):
            cols = slice(hp * LANES, (hp + 1) * LANES)
            o_t = jnp.concatenate([acc_ref[2 * hp] / l_ref[2 * hp],
                                   acc_ref[2 * hp + 1] / l_ref[2 * hp + 1]], axis=0)
            o_ref[:, cols] = (o_t.T * _silu(ga_ref[:, cols].astype(F32))).astype(BF16)


def _fox(z, vt, caq, cak, b, s, t):
    n = b * s
    nq = s // t
    W = BRANCH_WIDTH
    qi_of = np.concatenate([np.full(q + 1, q) for q in range(nq)]).astype(np.int32)
    ki_of = np.concatenate([np.arange(q + 1) for q in range(nq)]).astype(np.int32)
    qmap = lambda off: (lambda bi, p, qi, ki: (bi * nq + qi[p], off // W))
    kmap = lambda off: (lambda bi, p, qi, ki: (bi * nq + ki[p], off // W))
    grid_spec = pltpu.PrefetchScalarGridSpec(
        num_scalar_prefetch=2,
        grid=(b, len(qi_of)),
        in_specs=[
            pl.BlockSpec((t, W), qmap(OFF_QA)),
            pl.BlockSpec((t, W), qmap(0)),
            pl.BlockSpec((t, W), kmap(OFF_KA)),
            pl.BlockSpec((t, W), kmap(0)),
            pl.BlockSpec((W, t), lambda bi, p, qi, ki: (0, bi * nq + ki[p])),
            pl.BlockSpec((t, W), qmap(OFF_GA)),
        ],
        out_specs=pl.BlockSpec((t, W), qmap(0)),
        scratch_shapes=[
            pltpu.VMEM((FOX_HEADS, 2 * LANES, t), BF16),
            pltpu.VMEM((FOX_HEADS // 2, t, 2 * LANES), BF16),
            pltpu.VMEM((FOX_HEADS, 1, t), F32),
            pltpu.VMEM((FOX_HEADS, 1, t), F32),
            pltpu.VMEM((FOX_HEADS, HEAD_DIM, t), F32),
        ],
    )
    return pl.pallas_call(
        functools.partial(_fox_kernel, t=t),
        grid_spec=grid_spec,
        out_shape=jax.ShapeDtypeStruct((n, W), BF16),
        compiler_params=pltpu.CompilerParams(
            dimension_semantics=("parallel", "arbitrary"), vmem_limit_bytes=V7X_VMEM_LIMIT),
        name="fox",
    )(jnp.asarray(qi_of), jnp.asarray(ki_of), z, caq, z, cak, vt, z)


def _lru_kernel(xb_ref, gb_ref, cw_ref, cb_ref, wa_ref, ba_ref, wx_ref, bx_ref, lam_ref, o_ref,
                xpad_ref, h_ref, *, tt):
    @pl.when(pl.program_id(1) == 0)
    def _():
        xpad_ref[0:8, :] = jnp.zeros((8, BRANCH_WIDTH), F32)
        h_ref[...] = jnp.zeros_like(h_ref)

    x = xb_ref[...].astype(F32)
    xpad_ref[8:tt + 8, :] = x
    xc = cb_ref[...] + cw_ref[3:4, :] * x
    for j in range(CONV_WIDTH - 1):
        sh = CONV_WIDTH - 1 - j
        xc = xc + cw_ref[j:j + 1, :] * xpad_ref[8 - sh:8 - sh + tt, :]
    xpad_ref[0:8, :] = x[tt - 8:tt, :]

    xcb = xc.astype(BF16)
    r = _sigmoid(_mm(xcb, wa_ref[...]) + ba_ref[...])
    ig = _sigmoid(_mm(xcb, wx_ref[...]) + bx_ref[...])
    nl = -lam_ref[...]
    softplus = jnp.maximum(nl, 0.0) + jnp.log(1.0 + jnp.exp(-jnp.abs(nl)))
    log_a = (-LRU_C) * r * softplus
    a = jnp.exp(log_a)
    bb = jnp.sqrt(1.0 - jnp.exp(2.0 * log_a)) * (ig * xc)

    row = lax.broadcasted_iota(jnp.int32, a.shape, 0)
    d = 1
    while d < tt:
        keep = row >= d
        bb = bb + a * jnp.where(keep, pltpu.roll(bb, d, axis=0), 0.0)
        a = a * jnp.where(keep, pltpu.roll(a, d, axis=0), 1.0)
        d *= 2
    h = bb + a * h_ref[...]
    h_ref[...] = h[tt - 1:tt, :]
    o_ref[...] = (h * _silu(gb_ref[...].astype(F32))).astype(BF16)


def _lru(z, conv_w, conv_b, wa_bd, b_a, wx_bd, b_x, lam, b, s, tt):
    n = b * s
    nt = s // tt
    W = BRANCH_WIDTH
    full = lambda shape: pl.BlockSpec(shape, lambda bi, ti: (0, 0))
    zb = lambda off: pl.BlockSpec((tt, W), lambda bi, ti: (bi * nt + ti, off // W))
    return pl.pallas_call(
        functools.partial(_lru_kernel, tt=tt),
        grid=(b, nt),
        in_specs=[zb(OFF_XB), zb(OFF_GB), full((CONV_WIDTH, W)), full((1, W)), full((W, W)),
                  full((1, W)), full((W, W)), full((1, W)), full((1, W))],
        out_specs=pl.BlockSpec((tt, W), lambda bi, ti: (bi * nt + ti, 0)),
        out_shape=jax.ShapeDtypeStruct((n, W), BF16),
        scratch_shapes=[pltpu.VMEM((tt + 8, W), F32), pltpu.VMEM((1, W), F32)],
        compiler_params=pltpu.CompilerParams(dimension_semantics=("parallel", "arbitrary")),
        name="lru",
    )(z, z, conv_w, conv_b, wa_bd, b_a, wx_bd, b_x, lam)


def _gelu(x):
    return 0.5 * x * (1.0 + lax.erf(x * (1.0 / math.sqrt(2.0))))


def _sgu_kernel(u_ref, v_ref, g_ref, lng_ref, wm_ref, bs_ref, o_ref, *, tr):
    u = _gelu(u_ref[...].astype(F32))
    v = _gelu(v_ref[...].astype(F32))
    mu = jnp.mean(v, axis=-1, keepdims=True)
    vc = v - mu
    vn = (vc * lax.rsqrt(jnp.mean(vc * vc, axis=-1, keepdims=True) + NORM_EPS) * lng_ref[...]).astype(BF16)
    lane = lax.broadcasted_iota(jnp.int32, (SG_CHUNK, LANES), 1)
    chunks = []
    for c in range(tr // SG_CHUNK):
        vch = vn[c * SG_CHUNK:(c + 1) * SG_CHUNK, :]
        pairs = []
        for p in range(SG_GROUPS // 2):
            vp = vch[:, p * LANES:(p + 1) * LANES]
            pairs.append(jnp.where(lane < HEAD_DIM, _mm(wm_ref[2 * p], vp), _mm(wm_ref[2 * p + 1], vp)))
        chunks.append(jnp.concatenate(pairs, axis=1) + bs_ref[...])
    mixed = jnp.concatenate(chunks, axis=0)
    o_ref[...] = (u * mixed * _silu(g_ref[...].astype(F32))).astype(BF16)


def _sgu(z, ln_g, wm, bs_full, n, tr):
    W = BRANCH_WIDTH
    zb = lambda off: pl.BlockSpec((tr, W), lambda i: (i, off // W))
    return pl.pallas_call(
        functools.partial(_sgu_kernel, tr=tr),
        grid=(n // tr,),
        in_specs=[zb(OFF_UD), zb(OFF_VD), zb(OFF_GD),
                  pl.BlockSpec((1, W), lambda i: (0, 0)),
                  pl.BlockSpec((SG_GROUPS, SG_CHUNK, SG_CHUNK), lambda i: (0, 0, 0)),
                  pl.BlockSpec((SG_CHUNK, W), lambda i: (0, 0))],
        out_specs=pl.BlockSpec((tr, W), lambda i: (i, 0)),
        out_shape=jax.ShapeDtypeStruct((n, W), BF16),
        compiler_params=pltpu.CompilerParams(dimension_semantics=("parallel",)),
        name="sgu",
    )(z, z, z, ln_g, wm, bs_full)


def _compress_kernel(t16_ref, ptop_ref, pbot_ref, w1t_ref, w1b_ref, w2_ref, bd_ref, gain_ref,
                     kc_ref, vct_ref, *, nc):
    t16 = t16_ref[...].astype(F32)
    a = _mm((t16 + ptop_ref[...]).astype(BF16), w1t_ref[...])
    bm = _mm((t16 + pbot_ref[...]).astype(BF16), w1b_ref[...])
    h = a + pltpu.roll(bm, nc - 1, axis=0)
    out = _mm(_silu(h).astype(BF16), w2_ref[...])
    ms = _mm((out * out).astype(BF16), bd_ref[...])
    kn = out * lax.rsqrt(ms + NORM_EPS) * gain_ref[...]
    out_t = out.T
    for g in range(NSA_KV_HEADS):
        kc_ref[g] = kn[:, g * LANES:(g + 1) * LANES].astype(BF16)
        vct_ref[g] = out_t[(2 + g) * LANES:(2 + g) * LANES + HEAD_DIM, :].astype(BF16)


def _compress(t16, ptop, pbot, w1t, w1b, w2, bd, gain, b, nc):
    kw = 16 * 4 * HEAD_DIM
    full = lambda shape: pl.BlockSpec(shape, lambda bi: tuple(0 for _ in shape))
    return pl.pallas_call(
        functools.partial(_compress_kernel, nc=nc),
        grid=(b,),
        in_specs=[pl.BlockSpec((None, nc, kw), lambda bi: (bi, 0, 0)),
                  full((1, kw)), full((1, kw)), full((kw, 4 * CMP_HIDDEN)), full((kw, 4 * CMP_HIDDEN)),
                  full((4 * CMP_HIDDEN, ZT)), full((ZT, ZT)), full((1, ZT))],
        out_specs=[pl.BlockSpec((None, NSA_KV_HEADS, nc, LANES), lambda bi: (bi, 0, 0, 0)),
                   pl.BlockSpec((None, NSA_KV_HEADS, HEAD_DIM, nc), lambda bi: (bi, 0, 0, 0))],
        out_shape=[jax.ShapeDtypeStruct((b, NSA_KV_HEADS, nc, LANES), BF16),
                   jax.ShapeDtypeStruct((b, NSA_KV_HEADS, HEAD_DIM, nc), BF16)],
        compiler_params=pltpu.CompilerParams(
            dimension_semantics=("parallel",), vmem_limit_bytes=V7X_VMEM_LIMIT),
        name="compress",
    )(t16, ptop, pbot, w1t, w1b, w2, bd, gain)


def _softmax_cols(s, mask):
    sm = jnp.where(mask, s, NEG_INF)
    m = jnp.max(sm, axis=0, keepdims=True)
    p = jnp.exp2(sm - jnp.where(m > 0.5 * NEG_INF, m, 0.0))
    l = jnp.sum(p, axis=0, keepdims=True)
    return p, jnp.where(l > 0.0, 1.0 / l, 0.0)


def _nsa_kernel(q_ref, gc_ref, ks_ref, kw_ref, vs_ref, vw_ref, kc_ref, vct_ref, gt_ref, e_ref, ovt_ref,
                o_ref, kse_ref, sa_ref, sb_ref, sc_ref, selt_ref, m_ref, l_ref, acc_ref, *, s_len, nc):
    i = pl.program_id(2)
    qb, cols = Q_BLOCK, NSA_GROUP * Q_BLOCK
    ns = s_len // SLC_BLOCK
    top_k = min(SLC_TOPK, ns)
    kt_w = min(SEL_TILE, s_len)
    span = min(WINDOW + Q_BLOCK, s_len)
    lane = lax.broadcasted_iota(jnp.int32, (qb, LANES), 1)

    @pl.when(i == 0)
    def _():
        kse_ref[:, 0:LANES] = ks_ref[...]
        kse_ref[:, LANES:2 * LANES] = e_ref[...]

    parts = []
    for r in range(NSA_GROUP):
        blk = q_ref[:, (r // 2) * LANES:(r // 2 + 1) * LANES].astype(F32)
        parts.append(jnp.where(lane // HEAD_DIM == r % 2, blk, 0.0).T.astype(BF16))
    qs_t = jnp.concatenate(parts, axis=1)
    tcol = i * qb + lax.broadcasted_iota(jnp.int32, (1, cols), 1) % qb

    s_c = _mm(kc_ref[...], qs_t)
    start = pl.multiple_of(jnp.maximum(i - WINDOW // qb, 0) * qb, qb)
    s_w = _mm(kw_ref[pl.ds(start, span), :], qs_t)

    cend = lax.broadcasted_iota(jnp.int32, s_c.shape, 0) * CMP_STRIDE + (CMP_BLOCK - 1)
    p_c, inv_c = _softmax_cols(s_c, cend <= tcol)
    p_c = p_c * inv_c
    o_c = _mm(vct_ref[...], p_c.astype(BF16))

    ps = p_c[:, 0:qb] + p_c[:, qb:2 * qb] + p_c[:, 2 * qb:3 * qb] + p_c[:, 3 * qb:4 * qb]
    ps_hi = ps.astype(BF16)
    ps_lo = (ps - ps_hi.astype(F32)).astype(BF16)
    imp = _mm(ovt_ref[...], ps_hi) + _mm(ovt_ref[...], ps_lo)

    jrow = lax.broadcasted_iota(jnp.int32, (LANES, qb), 0)
    jt = (i * qb + lax.broadcasted_iota(jnp.int32, (LANES, qb), 1)) // SLC_BLOCK
    valid = (jrow <= jt) & (jrow < ns)
    forced = (jrow == 0) | (valid & (jrow > jt - SLC_LOCAL))
    score = jnp.where(forced, SLC_FORCE_SCORE, jnp.where(valid, imp, -1.0))
    sc_ref[...] = jnp.where(jrow < ns, score, -2.0)

    def rank_blocks(n_cand):
        for k in range(n_cand // 8):
            mine = sc_ref[8 * k:8 * k + 8, :]
            sub = 8 * k + lax.broadcasted_iota(jnp.int32, (8, qb), 0)
            cnt = jnp.zeros((8, qb), F32)
            for jp in range(n_cand):
                other = sc_ref[jp:jp + 1, :]
                if jp < 8 * k:
                    beats = other >= mine
                elif jp >= 8 * k + 8:
                    beats = other > mine
                else:
                    beats = (other > mine) | ((other == mine) & (sub > jp))
                cnt = cnt + jnp.where(beats, 1.0, 0.0)
            selt_ref[8 * k:8 * k + 8, :] = jnp.where((cnt < top_k) & (mine >= 0.0), 0.0, SEL_NEG)
        if n_cand < LANES:
            selt_ref[n_cand:LANES, :] = jnp.full((LANES - n_cand, qb), SEL_NEG, F32)

    cand_step = 16
    blocks_per_qb = qb // SLC_BLOCK
    classes = list(range(cand_step, ns, cand_step)) + [ns]
    lo_i = 0
    for n_cand in classes:
        hi_i = n_cand // blocks_per_qb if n_cand < ns else s_len // qb
        if n_cand <= top_k:
            @pl.when((i >= lo_i) & (i < hi_i))
            def _():
                selt_ref[...] = jnp.where(valid, 0.0, SEL_NEG)
        else:
            pl.when((i >= lo_i) & (i < hi_i))(functools.partial(rank_blocks, n_cand))
        lo_i = hi_i
    selb = selt_ref[...].astype(BF16)
    lhs_t = jnp.concatenate([qs_t, jnp.concatenate([selb] * NSA_GROUP, axis=1)], axis=0)

    def sel_scores(kt):
        k0 = pl.multiple_of(kt * kt_w, kt_w)
        return _mm(kse_ref[pl.ds(k0, kt_w), :], lhs_t)

    def sel_update(kt, s):
        k0 = pl.multiple_of(kt * kt_w, kt_w)
        m_prev = m_ref[...]
        m_new = jnp.maximum(m_prev, jnp.max(s, axis=0, keepdims=True))
        alpha = jnp.exp2(m_prev - m_new)
        pr = jnp.exp2(s - m_new)
        l_ref[...] = alpha * l_ref[...] + jnp.sum(pr, axis=0, keepdims=True)
        acc_ref[...] = alpha * acc_ref[...] + _mm(vs_ref[0:HEAD_DIM, pl.ds(k0, kt_w)], pr.astype(BF16))
        m_ref[...] = m_new

    m_ref[...] = jnp.full_like(m_ref, NEG_INF)
    l_ref[...] = jnp.zeros_like(l_ref)
    acc_ref[...] = jnp.zeros_like(acc_ref)
    n_full = (i * qb) // kt_w
    sa_ref[...] = sel_scores(0)

    pos = start + lax.broadcasted_iota(jnp.int32, s_w.shape, 0)
    p_w, inv_w = _softmax_cols(s_w, (pos <= tcol) & (pos > tcol - WINDOW))
    o_w = _mm(vw_ref[0:HEAD_DIM, pl.ds(start, span)], p_w.astype(BF16)) * inv_w

    def body(k2, carry):
        sb_ref[...] = sel_scores(2 * k2 + 1)
        sel_update(2 * k2, sa_ref[...])
        sa_ref[...] = sel_scores(2 * k2 + 2)
        sel_update(2 * k2 + 1, sb_ref[...])
        return carry

    lax.fori_loop(0, n_full // 2, body, 0)

    def last_update(s_last):
        pos_k = n_full * kt_w + lax.broadcasted_iota(jnp.int32, s_last.shape, 0)
        sel_update(n_full, jnp.where(pos_k <= tcol, s_last, NEG_INF))

    @pl.when(n_full % 2 == 1)
    def _():
        sb_ref[...] = sel_scores(n_full)
        sel_update(n_full - 1, sa_ref[...])
        last_update(sb_ref[...])

    @pl.when(n_full % 2 == 0)
    def _():
        last_update(sa_ref[...])

    o_s = acc_ref[...] * (1.0 / l_ref[...])

    gts = _sigmoid(gt_ref[...]).T
    heads = []
    for r in range(NSA_GROUP):
        cs = slice(r * qb, (r + 1) * qb)
        heads.append(gts[r:r + 1, :] * o_c[:, cs] + gts[4 + r:5 + r, :] * o_s[:, cs]
                     + gts[8 + r:9 + r, :] * o_w[:, cs])
    y = jnp.concatenate(heads, axis=0).T
    o_ref[...] = (y * _silu(gc_ref[...].astype(F32))).astype(BF16)


def _nsa(z, vt, small, kc, vct, e128, ovt, b, s):
    n = b * s
    nqb = s // Q_BLOCK
    nc = s // CMP_STRIDE
    gw = NSA_GROUP * HEAD_DIM
    cols = NSA_GROUP * Q_BLOCK
    qspec = lambda off: pl.BlockSpec((Q_BLOCK, gw), lambda bi, g, i: (bi * nqb + i, off // gw + g))
    seq = lambda off: pl.BlockSpec((s, LANES), lambda bi, g, i: (bi, off // LANES + g))
    vseq = lambda row: pl.BlockSpec((LANES, s), lambda bi, g, i: (row // LANES + g, bi))
    return pl.pallas_call(
        functools.partial(_nsa_kernel, s_len=s, nc=nc),
        grid=(b, NSA_KV_HEADS, nqb),
        in_specs=[qspec(OFF_QC), qspec(OFF_GC), seq(OFF_KSD), seq(OFF_KWD),
                  vseq(ZT), vseq(ZT + 256),
                  pl.BlockSpec((None, None, nc, LANES), lambda bi, g, i: (bi, g, 0, 0)),
                  pl.BlockSpec((None, None, HEAD_DIM, nc), lambda bi, g, i: (bi, g, 0, 0)),
                  pl.BlockSpec((Q_BLOCK, LANES), lambda bi, g, i: (bi * nqb + i, 1 + g)),
                  pl.BlockSpec((s, LANES), lambda bi, g, i: (0, 0)),
                  pl.BlockSpec((LANES, nc), lambda bi, g, i: (0, 0))],
        out_specs=pl.BlockSpec((Q_BLOCK, gw), lambda bi, g, i: (bi * nqb + i, g)),
        out_shape=jax.ShapeDtypeStruct((n, BRANCH_WIDTH), BF16),
        scratch_shapes=[pltpu.VMEM((s, 2 * LANES), BF16),
                        pltpu.VMEM((min(SEL_TILE, s), cols), F32),
                        pltpu.VMEM((min(SEL_TILE, s), cols), F32),
                        pltpu.VMEM((LANES, Q_BLOCK), F32), pltpu.VMEM((LANES, Q_BLOCK), F32),
                        pltpu.VMEM((1, cols), F32),
                        pltpu.VMEM((1, cols), F32), pltpu.VMEM((HEAD_DIM, cols), F32)],
        compiler_params=pltpu.CompilerParams(
            dimension_semantics=("parallel", "parallel", "arbitrary"),
            vmem_limit_bytes=V7X_VMEM_LIMIT),
        name="nsa",
    )(z, z, z, z, vt, vt, kc, vct, small, e128, ovt)


def _merge_kernel(ya_ref, yb_ref, yc_ref, yd_ref, mg_ref, x_ref, wb_ref, wo_ref, o_ref):
    merged = None
    for nb, y_ref in enumerate((ya_ref, yb_ref, yc_ref, yd_ref)):
        gate = _sigmoid(mg_ref[:, nb * D_MODEL:(nb + 1) * D_MODEL].astype(F32))
        term = gate * _mm(y_ref[...], wb_ref[nb])
        merged = term if merged is None else merged + term
    o_ref[...] = x_ref[...] + _mm(merged.astype(BF16), wo_ref[...])


def _merge(ys, z, x2, wb, wo, tm):
    n = x2.shape[0]
    W = BRANCH_WIDTH
    yspec = pl.BlockSpec((tm, W), lambda i: (i, 0))
    return pl.pallas_call(
        _merge_kernel,
        grid=(n // tm,),
        in_specs=[yspec, yspec, yspec, yspec,
                  pl.BlockSpec((tm, N_BRANCHES * D_MODEL), lambda i: (i, 0)),
                  pl.BlockSpec((tm, D_MODEL), lambda i: (i, 0)),
                  pl.BlockSpec((N_BRANCHES, W, D_MODEL), lambda i: (0, 0, 0)),
                  pl.BlockSpec((D_MODEL, D_MODEL), lambda i: (0, 0))],
        out_specs=pl.BlockSpec((tm, D_MODEL), lambda i: (i, 0)),
        out_shape=jax.ShapeDtypeStruct((n, D_MODEL), F32),
        compiler_params=pltpu.CompilerParams(
            dimension_semantics=("parallel",), vmem_limit_bytes=V7X_VMEM_LIMIT),
        name="merge",
    )(*ys, z, x2, wb, wo)


def _block_diag(blocks):
    nb, a, b_ = blocks.shape
    eye = jnp.eye(nb, dtype=blocks.dtype)
    return jnp.einsum('nab,nm->namb', blocks, eye).reshape(nb * a, nb * b_)


def _mean_matrix(width):
    idx = np.arange(width) // HEAD_DIM
    return jnp.asarray((idx[:, None] == idx[None, :]).astype(np.float32) / HEAD_DIM, BF16)


def _layout_w_in(w_in, qn_a, kn_a, qn_c, kn_c):
    sizes = _split_sizes()
    offs = np.concatenate([[0], np.cumsum(sizes)])
    seg = lambda k: w_in[:, int(offs[k]):int(offs[k + 1])]
    (qa, ka, va, fa, ga, xb, gb, qc, kcc, vcc, ksc, vsc, kwc, vwc, gate_c, gc, ud, vd, gd, mg) = [
        seg(k) for k in range(len(sizes))]
    dup = lambda w: jnp.concatenate([w[:, :HEAD_DIM], w[:, :HEAD_DIM], w[:, HEAD_DIM:], w[:, HEAD_DIM:]], axis=1)
    pad = jnp.zeros((D_MODEL, 256), w_in.dtype)
    w_z = jnp.concatenate([mg, qa, ka, qc, dup(ksc), dup(kwc), va, dup(vsc), dup(vwc),
                           ga, xb, gb, gc, ud, vd, gd, kcc, vcc, pad], axis=1).astype(BF16)
    scale = LOG2E / math.sqrt(HEAD_DIM)
    gain = jnp.ones((ZW,), F32)
    gain = gain.at[OFF_QA:OFF_QA + ZT].set(jnp.tile(qn_a * scale, FOX_HEADS))
    gain = gain.at[OFF_KA:OFF_KA + ZT].set(jnp.tile(kn_a, FOX_HEADS))
    gain = gain.at[OFF_QC:OFF_QC + ZT].set(jnp.tile(qn_c * scale, NSA_HEADS))
    gain = gain.at[OFF_KSD:OFF_KSD + ZT].set(jnp.tile(kn_c, ZT // HEAD_DIM))
    gate3 = gate_c.reshape(D_MODEL, 3, NSA_KV_HEADS, NSA_GROUP)
    zpad = lambda w: jnp.pad(w, ((0, 0), (0, LANES - w.shape[1])))
    w_small = jnp.concatenate(
        [zpad(fa)] + [zpad(gate3[:, :, g, :].reshape(D_MODEL, 3 * NSA_GROUP)) for g in range(NSA_KV_HEADS)],
        axis=1).astype(BF16)
    return w_z, gain.reshape(1, ZW), w_small


def _layout_compress(cmp_pos, wk1, wk2, wv1, wv2, kn_c):
    half = CMP_BLOCK // 2
    eye4 = jnp.eye(4, dtype=F32)

    def w1_half(lo):
        k = wk1.reshape(CMP_BLOCK, HEAD_DIM, CMP_HIDDEN)[lo:lo + half]
        v = wv1.reshape(CMP_BLOCK, HEAD_DIM, CMP_HIDDEN)[lo:lo + half]
        stack = jnp.stack([k, k, v, v])
        return jnp.einsum('spdn,st->psdtn', stack, eye4).reshape(half * 4 * HEAD_DIM, 4 * CMP_HIDDEN).astype(BF16)

    def pos_half(lo):
        p = cmp_pos[lo:lo + half]
        return jnp.broadcast_to(p[:, None, :], (half, 4, HEAD_DIM)).reshape(1, half * 4 * HEAD_DIM)

    w2 = jnp.zeros((4, CMP_HIDDEN, 4, 2, HEAD_DIM), F32)
    for slot, w in enumerate((wk2, wk2, wv2, wv2)):
        w2 = w2.at[slot, :, slot, :, :].set(jnp.broadcast_to(w[:, None, :], (CMP_HIDDEN, 2, HEAD_DIM)))
    w2 = w2.reshape(4 * CMP_HIDDEN, ZT).astype(BF16)
    gain = jnp.tile(kn_c, ZT // HEAD_DIM).reshape(1, ZT)
    return pos_half(0), pos_half(half), w1_half(0), w1_half(half), w2, gain


@functools.lru_cache(maxsize=None)
def _nsa_consts(s):
    nc = s // CMP_STRIDE
    ns = s // SLC_BLOCK
    e = np.zeros((s, LANES), np.float32)
    e[np.arange(s), np.arange(s) // SLC_BLOCK] = 1.0
    c0 = np.arange(nc) * CMP_STRIDE
    s0 = np.arange(ns) * SLC_BLOCK
    ovl = np.minimum(c0[:, None] + CMP_BLOCK, s0[None, :] + SLC_BLOCK) - np.maximum(c0[:, None], s0[None, :])
    ovt = np.zeros((LANES, nc), np.float32)
    ovt[:ns, :] = (np.clip(ovl, 0, None) / CMP_BLOCK).T
    ovt[:, nc - 1] = 0.0
    return e, ovt


def _tile(s, pref):
    t = pref
    while s % t:
        t //= 2
    return t


def _branches(x2, b, s, norm_g, w_in, b_forget, qn_a, kn_a, conv_w, conv_b, w_rg_a, b_rg_a, w_rg_x,
              b_rg_x, lru_lambda, qn_c, kn_c, cmp_pos, wk1, wk2, wv1, wv2, ln_v_g, w_spatial,
              b_spatial):
    n = b * s
    W = BRANCH_WIDTH
    w_z, gain_row, w_small = _layout_w_in(w_in, qn_a, kn_a, qn_c, kn_c)
    bd = _mean_matrix(ZT)
    z, small = _inproj(x2, norm_g.reshape(1, D_MODEL), w_z, gain_row, bd, w_small, _tile(n, 1024))
    vt = _vtrans(z, _tile(s, 512))

    bf_row = jnp.pad(b_forget, (0, LANES - FOX_HEADS)).reshape(1, LANES)
    caq, cak = _cprep(small, bf_row, b, s, _tile(s, 512))
    ys_a = _fox(z, vt, caq, cak, b, s, _tile(s, 512))

    ys_b = _lru(z, conv_w, conv_b.reshape(1, W), _block_diag(w_rg_a).astype(BF16), b_rg_a.reshape(1, W),
                _block_diag(w_rg_x).astype(BF16), b_rg_x.reshape(1, W), lru_lambda.reshape(1, W),
                b, s, _tile(s, 256))

    nc = s // CMP_STRIDE
    t16 = z[:, OFF_CMP:OFF_CMP + 4 * HEAD_DIM].reshape(b, nc, CMP_STRIDE * 4 * HEAD_DIM)
    ptop, pbot, w1t, w1b, w2c, cgain = _layout_compress(cmp_pos, wk1, wk2, wv1, wv2, kn_c)
    kc, vct = _compress(t16, ptop, pbot, w1t, w1b, w2c, bd, cgain, b, nc)
    e_np, ovt_np = _nsa_consts(s)
    ys_c = _nsa(z, vt, small, kc, vct, jnp.asarray(e_np, BF16), jnp.asarray(ovt_np, BF16), b, s)

    causal = jnp.tril(jnp.ones((SG_CHUNK, SG_CHUNK), F32))
    wm = (w_spatial * causal).astype(BF16)
    bs_full = jnp.broadcast_to(b_spatial.T[:, :, None], (SG_CHUNK, SG_GROUPS, HEAD_DIM)).reshape(SG_CHUNK, W)
    ys_d = _sgu(z, ln_v_g.reshape(1, W), wm, bs_full, n, _tile(s, 512))
    return (ys_a, ys_b, ys_c, ys_d), z


def _layer(x2, b, s, *params):
    ys, z = _branches(x2, b, s, *params[:-2])
    w_branch, w_out = params[-2:]
    return _merge(ys, z, x2, w_branch.astype(BF16), w_out.astype(BF16), _tile(b * s, 512))


def kernel(x, norm_g, w_in, b_forget, qn_a, kn_a, conv_w, conv_b, w_rg_a, b_rg_a, w_rg_x, b_rg_x,
           lru_lambda, qn_c, kn_c, cmp_pos, cmp_k_w1, cmp_k_w2, cmp_v_w1, cmp_v_w2, ln_v_g,
           w_spatial, b_spatial, w_branch, w_out):
    b, s, d = x.shape
    x2 = x.reshape(b * s, d)
    params = (norm_g, w_in, b_forget, qn_a, kn_a, conv_w, conv_b, w_rg_a, b_rg_a, w_rg_x, b_rg_x,
              lru_lambda, qn_c, kn_c, cmp_pos, cmp_k_w1, cmp_k_w2, cmp_v_w1, cmp_v_w2, ln_v_g,
              w_spatial, b_spatial, w_branch, w_out)
    for layer in range(norm_g.shape[0]):
        x2 = _layer(x2, b, s, *[p[layer] for p in params])
    return x2.reshape(b, s, d)
```

```python
import functools
import math

import numpy as np
import jax
import jax.numpy as jnp
from jax import lax
from jax.experimental import pallas as pl
from jax.experimental.pallas import tpu as pltpu

F32 = jnp.float32
BF16 = jnp.bfloat16

D_MODEL = 1024
HEAD_DIM = 64
BRANCH_WIDTH = 512
N_BRANCHES = 4
FOX_HEADS = 8
LRU_BLOCKS = 8
CONV_WIDTH = 4
LRU_C = 8.0
NSA_HEADS = 8
NSA_KV_HEADS = 2
NSA_GROUP = 4
CMP_BLOCK = 32
CMP_STRIDE = 16
CMP_HIDDEN = 128
SLC_BLOCK = 64
SLC_TOPK = 16
SLC_LOCAL = 2
SLC_FORCE_SCORE = 1e6
WINDOW = 512
Q_BLOCK = 128
SG_CHUNK = 128
SG_GROUPS = 8
NORM_EPS = 1e-6
NEG_INF = -1e30
SEL_NEG = -32768.0

LANES = 128
V7X_VMEM_LIMIT = 56 * 1024 * 1024
SEL_TILE = 512
LOG2E = 1.4426950408889634

ZT = 512
OFF_MG = 0
OFF_QA = 8 * ZT
OFF_KA = 9 * ZT
OFF_QC = 10 * ZT
OFF_KSD = 11 * ZT
OFF_KWD = 11 * ZT + 256
OFF_VA = 12 * ZT
OFF_VSD = 13 * ZT
OFF_VWD = 13 * ZT + 256
OFF_GA = 14 * ZT
OFF_XB = 15 * ZT
OFF_GB = 16 * ZT
OFF_GC = 17 * ZT
OFF_UD = 18 * ZT
OFF_VD = 19 * ZT
OFF_GD = 20 * ZT
OFF_CMP = 21 * ZT
ZW = 22 * ZT
IN_TN = 2 * ZT
NORM_TILES = (OFF_QA // IN_TN, OFF_QC // IN_TN)
SMALL_W = 384
VT_TILES = (OFF_VA // ZT, OFF_VSD // ZT)


def _mm(a, b):
    return jnp.dot(a, b, preferred_element_type=F32)


def _sigmoid(x):
    return 0.5 * jnp.tanh(0.5 * x) + 0.5


def _silu(x):
    return x * _sigmoid(x)


def _split_sizes():
    W, kv = BRANCH_WIDTH, NSA_KV_HEADS * HEAD_DIM
    return (W, W, W, FOX_HEADS, W, W, W, W, kv, kv, kv, kv, kv, kv, 3 * NSA_HEADS, W,
            W, W, W, N_BRANCHES * D_MODEL)


def _inproj_kernel(x_ref, g_ref, w_ref, gain_ref, bd_ref, ws_ref, z_ref, small_ref, xn_ref):
    j = pl.program_id(1)

    @pl.when(j == 0)
    def _():
        x = x_ref[...]
        ms = jnp.mean(x * x, axis=-1, keepdims=True)
        xn = ((x * lax.rsqrt(ms + NORM_EPS)) * g_ref[...]).astype(BF16)
        xn_ref[...] = xn
        small_ref[...] = _mm(xn, ws_ref[...])

    acc = _mm(xn_ref[...], w_ref[...])
    is_norm = functools.reduce(jnp.logical_or, [j == t for t in NORM_TILES])

    @pl.when(is_norm)
    def _():
        sq = (acc * acc).astype(BF16)
        ms = jnp.concatenate([_mm(sq[:, c:c + ZT], bd_ref[...]) for c in range(0, IN_TN, ZT)], axis=1)
        z_ref[...] = (acc * lax.rsqrt(ms + NORM_EPS) * gain_ref[...]).astype(BF16)

    @pl.when(jnp.logical_not(is_norm))
    def _():
        z_ref[...] = acc.astype(BF16)


def _inproj(x2, norm_g, w_z, gain_row, bd, w_small, tm):
    n = x2.shape[0]
    grid = (n // tm, ZW // IN_TN)
    return pl.pallas_call(
        _inproj_kernel,
        grid=grid,
        in_specs=[
            pl.BlockSpec((tm, D_MODEL), lambda i, j: (i, 0)),
            pl.BlockSpec((1, D_MODEL), lambda i, j: (0, 0)),
            pl.BlockSpec((D_MODEL, IN_TN), lambda i, j: (0, j)),
            pl.BlockSpec((1, IN_TN), lambda i, j: (0, j)),
            pl.BlockSpec((ZT, ZT), lambda i, j: (0, 0)),
            pl.BlockSpec((D_MODEL, SMALL_W), lambda i, j: (0, 0)),
        ],
        out_specs=[
            pl.BlockSpec((tm, IN_TN), lambda i, j: (i, j)),
            pl.BlockSpec((tm, SMALL_W), lambda i, j: (i, 0)),
        ],
        out_shape=[
            jax.ShapeDtypeStruct((n, ZW), BF16),
            jax.ShapeDtypeStruct((n, SMALL_W), F32),
        ],
        scratch_shapes=[pltpu.VMEM((tm, D_MODEL), BF16)],
        compiler_params=pltpu.CompilerParams(
            dimension_semantics=("parallel", "arbitrary"), vmem_limit_bytes=V7X_VMEM_LIMIT),
        name="inproj",
    )(x2, norm_g, w_z, gain_row, bd, w_small)


def _vtrans_kernel(z_ref, o_ref):
    o_ref[...] = z_ref[...].T


def _vtrans(z, tm):
    n = z.shape[0]
    step = VT_TILES[1] - VT_TILES[0]
    return pl.pallas_call(
        _vtrans_kernel,
        grid=(n // tm, len(VT_TILES)),
        in_specs=[pl.BlockSpec((tm, ZT), lambda i, j: (i, VT_TILES[0] + step * j))],
        out_specs=pl.BlockSpec((ZT, tm), lambda i, j: (j, i)),
        out_shape=jax.ShapeDtypeStruct((len(VT_TILES) * ZT, n), BF16),
        compiler_params=pltpu.CompilerParams(dimension_semantics=("parallel", "parallel")),
        name="vtrans",
    )(z)


def _cprep_kernel(fa_ref, bf_ref, pq_ref, pk_ref, oq_ref, ok_ref, caq_ref, cak_ref, carry_ref,
                  *, tt):
    @pl.when(pl.program_id(1) == 0)
    def _():
        carry_ref[...] = jnp.zeros_like(carry_ref)

    x = fa_ref[...] + bf_ref[...]
    c = jnp.minimum(x, 0.0) - jnp.log(1.0 + jnp.exp(-jnp.abs(x)))
    row = lax.broadcasted_iota(jnp.int32, c.shape, 0)
    d = 1
    while d < tt:
        c = c + jnp.where(row >= d, pltpu.roll(c, d, axis=0), 0.0)
        d *= 2
    c = c + carry_ref[...]
    carry_ref[...] = c[tt - 1:tt, :]
    c = c * LOG2E
    hi = c.astype(BF16)
    r1 = c - hi.astype(F32)
    mid = r1.astype(BF16)
    lo = (r1 - mid.astype(F32)).astype(BF16)
    parts = jnp.concatenate([hi, mid, lo], axis=1)
    caq_ref[...] = (_mm(parts, pq_ref[...]) + oq_ref[...]).astype(BF16)
    cak_ref[...] = (_mm(parts, pk_ref[...]) + ok_ref[...]).astype(BF16)


def _aug_lane(h):
    return (h // 2) * LANES + (h % 2) * 8


@functools.lru_cache(maxsize=None)
def _cprep_consts():
    pq = np.zeros((3 * LANES, 4 * LANES), np.float32)
    pk = np.zeros((3 * LANES, 4 * LANES), np.float32)
    oq = np.zeros((1, 4 * LANES), np.float32)
    ok = np.zeros((1, 4 * LANES), np.float32)
    for h in range(FOX_HEADS):
        base = _aug_lane(h)
        for part in range(3):
            pq[part * LANES + h, base + part] = 1.0
            pk[part * LANES + h, base + 3 + part] = -1.0
            oq[0, base + 3 + part] = 1.0
            ok[0, base + part] = 1.0
    return pq, pk, oq, ok


def _cprep(small, bf_row, b, s, tt):
    n = b * s
    nt = s // tt
    pq, pk, oq, ok = _cprep_consts()
    full = lambda shape: pl.BlockSpec(shape, lambda bi, ti: (0, 0))
    return pl.pallas_call(
        functools.partial(_cprep_kernel, tt=tt),
        grid=(b, nt),
        in_specs=[
            pl.BlockSpec((tt, LANES), lambda bi, ti: (bi * nt + ti, 0)),
            full((1, LANES)),
            full((3 * LANES, 4 * LANES)),
            full((3 * LANES, 4 * LANES)),
            full((1, 4 * LANES)),
            full((1, 4 * LANES)),
        ],
        out_specs=[
            pl.BlockSpec((tt, 4 * LANES), lambda bi, ti: (bi * nt + ti, 0)),
            pl.BlockSpec((tt, 4 * LANES), lambda bi, ti: (bi * nt + ti, 0)),
        ],
        out_shape=[jax.ShapeDtypeStruct((n, 4 * LANES), BF16)] * 2,
        scratch_shapes=[pltpu.VMEM((1, LANES), F32)],
        compiler_params=pltpu.CompilerParams(dimension_semantics=("parallel", "arbitrary")),
        name="cprep",
    )(small, bf_row, jnp.asarray(pq, BF16), jnp.asarray(pk, BF16), jnp.asarray(oq), jnp.asarray(ok))


def _fox_kernel(qi_ref, ki_ref, q_ref, caq_ref, k_ref, cak_ref, vt_ref, ga_ref, o_ref,
                lhs_ref, rhs_ref, m_ref, l_ref, acc_ref, *, t):
    pair = pl.program_id(1)
    qi = qi_ref[pair]
    ki = ki_ref[pair]
    n_pairs = FOX_HEADS // 2

    @pl.when(ki == 0)
    def _():
        lane = lax.broadcasted_iota(jnp.int32, (t, 2 * LANES), 1)
        head_of_lane = jnp.where(lane < LANES, lane // HEAD_DIM, (lane - LANES) // 8)
        for hp in range(n_pairs):
            cols = slice(hp * LANES, (hp + 1) * LANES)
            lhs = jnp.concatenate([q_ref[:, cols], caq_ref[:, cols]], axis=1).astype(F32)
            for hh in range(2):
                lhs_ref[2 * hp + hh] = jnp.where(head_of_lane == hh, lhs, 0.0).T.astype(BF16)
        m_ref[...] = jnp.full_like(m_ref, NEG_INF)
        l_ref[...] = jnp.zeros_like(l_ref)
        acc_ref[...] = jnp.zeros_like(acc_ref)

    def step(diag):
        for hp in range(n_pairs):
            cols = slice(hp * LANES, (hp + 1) * LANES)
            rhs_ref[hp] = jnp.concatenate([k_ref[:, cols], cak_ref[:, cols]], axis=1)

        s_next = _mm(rhs_ref[0], lhs_ref[0])
        for h in range(FOX_HEADS):
            s = s_next
            if h + 1 < FOX_HEADS:
                s_next = _mm(rhs_ref[(h + 1) // 2], lhs_ref[h + 1])
            if diag:
                key = lax.broadcasted_iota(jnp.int32, s.shape, 0)
                qry = lax.broadcasted_iota(jnp.int32, s.shape, 1)
                s = jnp.where(key <= qry, s, NEG_INF)
            m_prev = m_ref[h]
            m_new = jnp.maximum(m_prev, jnp.max(s, axis=0, keepdims=True))
            alpha = jnp.exp2(m_prev - m_new)
            p = jnp.exp2(s - m_new)
            l_ref[h] = alpha * l_ref[h] + jnp.sum(p, axis=0, keepdims=True)
            vt = vt_ref[h * HEAD_DIM:(h + 1) * HEAD_DIM, :]
            acc_ref[h] = alpha * acc_ref[h] + _mm(vt, p.astype(BF16))
            m_ref[h] = m_new

    @pl.when(ki < qi)
    def _():
        step(False)

    @pl.when(ki == qi)
    def _():
        step(True)
        for hp in range(n_pairs):
            cols = slice(hp * LANES, (hp + 1) * LANES)
            o_t = jnp.concatenate([acc_ref[2 * hp] / l_ref[2 * hp],
                                   acc_ref[2 * hp + 1] / l_ref[2 * hp + 1]], axis=0)
            o_ref[:, cols] = (o_t.T * _silu(ga_ref[:, cols].astype(F32))).astype(BF16)


def _fox(z, vt, caq, cak, b, s, t):
    n = b * s
    nq = s // t
    W = BRANCH_WIDTH
    qi_of = np.concatenate([np.full(q + 1, q) for q in range(nq)]).astype(np.int32)
    ki_of = np.concatenate([np.arange(q + 1) for q in range(nq)]).astype(np.int32)
    qmap = lambda off: (lambda bi, p, qi, ki: (bi * nq + qi[p], off // W))
    kmap = lambda off: (lambda bi, p, qi, ki: (bi * nq + ki[p], off // W))
    grid_spec = pltpu.PrefetchScalarGridSpec(
        num_scalar_prefetch=2,
        grid=(b, len(qi_of)),
        in_specs=[
            pl.BlockSpec((t, W), qmap(OFF_QA)),
            pl.BlockSpec((t, W), qmap(0)),
            pl.BlockSpec((t, W), kmap(OFF_KA)),
            pl.BlockSpec((t, W), kmap(0)),
            pl.BlockSpec((W, t), lambda bi, p, qi, ki: (0, bi * nq + ki[p])),
            pl.BlockSpec((t, W), qmap(OFF_GA)),
        ],
        out_specs=pl.BlockSpec((t, W), qmap(0)),
        scratch_shapes=[
            pltpu.VMEM((FOX_HEADS, 2 * LANES, t), BF16),
            pltpu.VMEM((FOX_HEADS // 2, t, 2 * LANES), BF16),
            pltpu.VMEM((FOX_HEADS, 1, t), F32),
            pltpu.VMEM((FOX_HEADS, 1, t), F32),
            pltpu.VMEM((FOX_HEADS, HEAD_DIM, t), F32),
        ],
    )
    return pl.pallas_call(
        functools.partial(_fox_kernel, t=t),
        grid_spec=grid_spec,
        out_shape=jax.ShapeDtypeStruct((n, W), BF16),
        compiler_params=pltpu.CompilerParams(
            dimension_semantics=("parallel", "arbitrary"), vmem_limit_bytes=V7X_VMEM_LIMIT),
        name="fox",
    )(jnp.asarray(qi_of), jnp.asarray(ki_of), z, caq, z, cak, vt, z)


def _lru_kernel(xb_ref, gb_ref, cw_ref, cb_ref, wa_ref, ba_ref, wx_ref, bx_ref, lam_ref, o_ref,
                xpad_ref, h_ref, *, tt):
    @pl.when(pl.program_id(1) == 0)
    def _():
        xpad_ref[0:8, :] = jnp.zeros((8, BRANCH_WIDTH), F32)
        h_ref[...] = jnp.zeros_like(h_ref)

    x = xb_ref[...].astype(F32)
    xpad_ref[8:tt + 8, :] = x
    xc = cb_ref[...] + cw_ref[3:4, :] * x
    for j in range(CONV_WIDTH - 1):
        sh = CONV_WIDTH - 1 - j
        xc = xc + cw_ref[j:j + 1, :] * xpad_ref[8 - sh:8 - sh + tt, :]
    xpad_ref[0:8, :] = x[tt - 8:tt, :]

    xcb = xc.astype(BF16)
    r = _sigmoid(_mm(xcb, wa_ref[...]) + ba_ref[...])
    ig = _sigmoid(_mm(xcb, wx_ref[...]) + bx_ref[...])
    nl = -lam_ref[...]
    softplus = jnp.maximum(nl, 0.0) + jnp.log(1.0 + jnp.exp(-jnp.abs(nl)))
    log_a = (-LRU_C) * r * softplus
    a = jnp.exp(log_a)
    bb = jnp.sqrt(1.0 - jnp.exp(2.0 * log_a)) * (ig * xc)

    row = lax.broadcasted_iota(jnp.int32, a.shape, 0)
    d = 1
    while d < tt:
        keep = row >= d
        bb = bb + a * jnp.where(keep, pltpu.roll(bb, d, axis=0), 0.0)
        a = a * jnp.where(keep, pltpu.roll(a, d, axis=0), 1.0)
        d *= 2
    h = bb + a * h_ref[...]
    h_ref[...] = h[tt - 1:tt, :]
    o_ref[...] = (h * _silu(gb_ref[...].astype(F32))).astype(BF16)


def _lru(z, conv_w, conv_b, wa_bd, b_a, wx_bd, b_x, lam, b, s, tt):
    n = b * s
    nt = s // tt
    W = BRANCH_WIDTH
    full = lambda shape: pl.BlockSpec(shape, lambda bi, ti: (0, 0))
    zb = lambda off: pl.BlockSpec((tt, W), lambda bi, ti: (bi * nt + ti, off // W))
    return pl.pallas_call(
        functools.partial(_lru_kernel, tt=tt),
        grid=(b, nt),
        in_specs=[zb(OFF_XB), zb(OFF_GB), full((CONV_WIDTH, W)), full((1, W)), full((W, W)),
                  full((1, W)), full((W, W)), full((1, W)), full((1, W))],
        out_specs=pl.BlockSpec((tt, W), lambda bi, ti: (bi * nt + ti, 0)),
        out_shape=jax.ShapeDtypeStruct((n, W), BF16),
        scratch_shapes=[pltpu.VMEM((tt + 8, W), F32), pltpu.VMEM((1, W), F32)],
        compiler_params=pltpu.CompilerParams(dimension_semantics=("parallel", "arbitrary")),
        name="lru",
    )(z, z, conv_w, conv_b, wa_bd, b_a, wx_bd, b_x, lam)


def _gelu(x):
    return 0.5 * x * (1.0 + lax.erf(x * (1.0 / math.sqrt(2.0))))


def _sgu_kernel(u_ref, v_ref, g_ref, lng_ref, wm_ref, bs_ref, o_ref, *, tr):
    u = _gelu(u_ref[...].astype(F32))
    v = _gelu(v_ref[...].astype(F32))
    mu = jnp.mean(v, axis=-1, keepdims=True)
    vc = v - mu
    vn = (vc * lax.rsqrt(jnp.mean(vc * vc, axis=-1, keepdims=True) + NORM_EPS) * lng_ref[...]).astype(BF16)
    lane = lax.broadcasted_iota(jnp.int32, (SG_CHUNK, LANES), 1)
    chunks = []
    for c in range(tr // SG_CHUNK):
        vch = vn[c * SG_CHUNK:(c + 1) * SG_CHUNK, :]
        pairs = []
        for p in range(SG_GROUPS // 2):
            vp = vch[:, p * LANES:(p + 1) * LANES]
            pairs.append(jnp.where(lane < HEAD_DIM, _mm(wm_ref[2 * p], vp), _mm(wm_ref[2 * p + 1], vp)))
        chunks.append(jnp.concatenate(pairs, axis=1) + bs_ref[...])
    mixed = jnp.concatenate(chunks, axis=0)
    o_ref[...] = (u * mixed * _silu(g_ref[...].astype(F32))).astype(BF16)


def _sgu(z, ln_g, wm, bs_full, n, tr):
    W = BRANCH_WIDTH
    zb = lambda off: pl.BlockSpec((tr, W), lambda i: (i, off // W))
    return pl.pallas_call(
        functools.partial(_sgu_kernel, tr=tr),
        grid=(n // tr,),
        in_specs=[zb(OFF_UD), zb(OFF_VD), zb(OFF_GD),
                  pl.BlockSpec((1, W), lambda i: (0, 0)),
                  pl.BlockSpec((SG_GROUPS, SG_CHUNK, SG_CHUNK), lambda i: (0, 0, 0)),
                  pl.BlockSpec((SG_CHUNK, W), lambda i: (0, 0))],
        out_specs=pl.BlockSpec((tr, W), lambda i: (i, 0)),
        out_shape=jax.ShapeDtypeStruct((n, W), BF16),
        compiler_params=pltpu.CompilerParams(dimension_semantics=("parallel",)),
        name="sgu",
    )(z, z, z, ln_g, wm, bs_full)


def _compress_kernel(t16_ref, ptop_ref, pbot_ref, w1t_ref, w1b_ref, w2_ref, bd_ref, gain_ref,
                     kc_ref, vct_ref, *, nc):
    t16 = t16_ref[...].astype(F32)
    a = _mm((t16 + ptop_ref[...]).astype(BF16), w1t_ref[...])
    bm = _mm((t16 + pbot_ref[...]).astype(BF16), w1b_ref[...])
    h = a + pltpu.roll(bm, nc - 1, axis=0)
    out = _mm(_silu(h).astype(BF16), w2_ref[...])
    ms = _mm((out * out).astype(BF16), bd_ref[...])
    kn = out * lax.rsqrt(ms + NORM_EPS) * gain_ref[...]
    out_t = out.T
    for g in range(NSA_KV_HEADS):
        kc_ref[g] = kn[:, g * LANES:(g + 1) * LANES].astype(BF16)
        vct_ref[g] = out_t[(2 + g) * LANES:(2 + g) * LANES + HEAD_DIM, :].astype(BF16)


def _compress(t16, ptop, pbot, w1t, w1b, w2, bd, gain, b, nc):
    kw = 16 * 4 * HEAD_DIM
    full = lambda shape: pl.BlockSpec(shape, lambda bi: tuple(0 for _ in shape))
    return pl.pallas_call(
        functools.partial(_compress_kernel, nc=nc),
        grid=(b,),
        in_specs=[pl.BlockSpec((None, nc, kw), lambda bi: (bi, 0, 0)),
                  full((1, kw)), full((1, kw)), full((kw, 4 * CMP_HIDDEN)), full((kw, 4 * CMP_HIDDEN)),
                  full((4 * CMP_HIDDEN, ZT)), full((ZT, ZT)), full((1, ZT))],
        out_specs=[pl.BlockSpec((None, NSA_KV_HEADS, nc, LANES), lambda bi: (bi, 0, 0, 0)),
                   pl.BlockSpec((None, NSA_KV_HEADS, HEAD_DIM, nc), lambda bi: (bi, 0, 0, 0))],
        out_shape=[jax.ShapeDtypeStruct((b, NSA_KV_HEADS, nc, LANES), BF16),
                   jax.ShapeDtypeStruct((b, NSA_KV_HEADS, HEAD_DIM, nc), BF16)],
        compiler_params=pltpu.CompilerParams(
            dimension_semantics=("parallel",), vmem_limit_bytes=V7X_VMEM_LIMIT),
        name="compress",
    )(t16, ptop, pbot, w1t, w1b, w2, bd, gain)


def _softmax_cols(s, mask):
    sm = jnp.where(mask, s, NEG_INF)
    m = jnp.max(sm, axis=0, keepdims=True)
    p = jnp.exp2(sm - jnp.where(m > 0.5 * NEG_INF, m, 0.0))
    l = jnp.sum(p, axis=0, keepdims=True)
    return p, jnp.where(l > 0.0, 1.0 / l, 0.0)


def _nsa_kernel(q_ref, gc_ref, ks_ref, kw_ref, vs_ref, vw_ref, kc_ref, vct_ref, gt_ref, e_ref, ovt_ref,
                o_ref, kse_ref, sa_ref, sb_ref, sc_ref, selt_ref, m_ref, l_ref, acc_ref, *, s_len, nc):
    i = pl.program_id(2)
    qb, cols = Q_BLOCK, NSA_GROUP * Q_BLOCK
    ns = s_len // SLC_BLOCK
    top_k = min(SLC_TOPK, ns)
    kt_w = min(SEL_TILE, s_len)
    span = min(WINDOW + Q_BLOCK, s_len)
    lane = lax.broadcasted_iota(jnp.int32, (qb, LANES), 1)

    @pl.when(i == 0)
    def _():
        kse_ref[:, 0:LANES] = ks_ref[...]
        kse_ref[:, LANES:2 * LANES] = e_ref[...]

    parts = []
    for r in range(NSA_GROUP):
        blk = q_ref[:, (r // 2) * LANES:(r // 2 + 1) * LANES].astype(F32)
        parts.append(jnp.where(lane // HEAD_DIM == r % 2, blk, 0.0).T.astype(BF16))
    qs_t = jnp.concatenate(parts, axis=1)
    tcol = i * qb + lax.broadcasted_iota(jnp.int32, (1, cols), 1) % qb

    s_c = _mm(kc_ref[...], qs_t)
    start = pl.multiple_of(jnp.maximum(i - WINDOW // qb, 0) * qb, qb)
    s_w = _mm(kw_ref[pl.ds(start, span), :], qs_t)

    cend = lax.broadcasted_iota(jnp.int32, s_c.shape, 0) * CMP_STRIDE + (CMP_BLOCK - 1)
    p_c, inv_c = _softmax_cols(s_c, cend <= tcol)
    p_c = p_c * inv_c
    o_c = _mm(vct_ref[...], p_c.astype(BF16))

    ps = p_c[:, 0:qb] + p_c[:, qb:2 * qb] + p_c[:, 2 * qb:3 * qb] + p_c[:, 3 * qb:4 * qb]
    ps_hi = ps.astype(BF16)
    ps_lo = (ps - ps_hi.astype(F32)).astype(BF16)
    imp = _mm(ovt_ref[...], ps_hi) + _mm(ovt_ref[...], ps_lo)

    jrow = lax.broadcasted_iota(jnp.int32, (LANES, qb), 0)
    jt = (i * qb + lax.broadcasted_iota(jnp.int32, (LANES, qb), 1)) // SLC_BLOCK
    valid = (jrow <= jt) & (jrow < ns)
    forced = (jrow == 0) | (valid & (jrow > jt - SLC_LOCAL))
    score = jnp.where(forced, SLC_FORCE_SCORE, jnp.where(valid, imp, -1.0))
    sc_ref[...] = jnp.where(jrow < ns, score, -2.0)

    def rank_blocks(n_cand):
        for k in range(n_cand // 8):
            mine = sc_ref[8 * k:8 * k + 8, :]
            sub = 8 * k + lax.broadcasted_iota(jnp.int32, (8, qb), 0)
            cnt = jnp.zeros((8, qb), F32)
            for jp in range(n_cand):
                other = sc_ref[jp:jp + 1, :]
                if jp < 8 * k:
                    beats = other >= mine
                elif jp >= 8 * k + 8:
                    beats = other > mine
                else:
                    beats = (other > mine) | ((other == mine) & (sub > jp))
                cnt = cnt + jnp.where(beats, 1.0, 0.0)
            selt_ref[8 * k:8 * k + 8, :] = jnp.where((cnt < top_k) & (mine >= 0.0), 0.0, SEL_NEG)
        if n_cand < LANES:
            selt_ref[n_cand:LANES, :] = jnp.full((LANES - n_cand, qb), SEL_NEG, F32)

    cand_step = 16
    blocks_per_qb = qb // SLC_BLOCK
    classes = list(range(cand_step, ns, cand_step)) + [ns]
    lo_i = 0
    for n_cand in classes:
        hi_i = n_cand // blocks_per_qb if n_cand < ns else s_len // qb
        if n_cand <= top_k:
            @pl.when((i >= lo_i) & (i < hi_i))
            def _():
                selt_ref[...] = jnp.where(valid, 0.0, SEL_NEG)
        else:
            pl.when((i >= lo_i) & (i < hi_i))(functools.partial(rank_blocks, n_cand))
        lo_i = hi_i
    selb = selt_ref[...].astype(BF16)
    lhs_t = jnp.concatenate([qs_t, jnp.concatenate([selb] * NSA_GROUP, axis=1)], axis=0)

    def sel_scores(kt):
        k0 = pl.multiple_of(kt * kt_w, kt_w)
        return _mm(kse_ref[pl.ds(k0, kt_w), :], lhs_t)

    def sel_update(kt, s):
        k0 = pl.multiple_of(kt * kt_w, kt_w)
        m_prev = m_ref[...]
        m_new = jnp.maximum(m_prev, jnp.max(s, axis=0, keepdims=True))
        alpha = jnp.exp2(m_prev - m_new)
        pr = jnp.exp2(s - m_new)
        l_ref[...] = alpha * l_ref[...] + jnp.sum(pr, axis=0, keepdims=True)
        acc_ref[...] = alpha * acc_ref[...] + _mm(vs_ref[0:HEAD_DIM, pl.ds(k0, kt_w)], pr.astype(BF16))
        m_ref[...] = m_new

    m_ref[...] = jnp.full_like(m_ref, NEG_INF)
    l_ref[...] = jnp.zeros_like(l_ref)
    acc_ref[...] = jnp.zeros_like(acc_ref)
    n_full = (i * qb) // kt_w
    sa_ref[...] = sel_scores(0)

    pos = start + lax.broadcasted_iota(jnp.int32, s_w.shape, 0)
    p_w, inv_w = _softmax_cols(s_w, (pos <= tcol) & (pos > tcol - WINDOW))
    o_w = _mm(vw_ref[0:HEAD_DIM, pl.ds(start, span)], p_w.astype(BF16)) * inv_w

    def body(k2, carry):
        sb_ref[...] = sel_scores(2 * k2 + 1)
        sel_update(2 * k2, sa_ref[...])
        sa_ref[...] = sel_scores(2 * k2 + 2)
        sel_update(2 * k2 + 1, sb_ref[...])
        return carry

    lax.fori_loop(0, n_full // 2, body, 0)

    def last_update(s_last):
        pos_k = n_full * kt_w + lax.broadcasted_iota(jnp.int32, s_last.shape, 0)
        sel_update(n_full, jnp.where(pos_k <= tcol, s_last, NEG_INF))

    @pl.when(n_full % 2 == 1)
    def _():
        sb_ref[...] = sel_scores(n_full)
        sel_update(n_full - 1, sa_ref[...])
        last_update(sb_ref[...])

    @pl.when(n_full % 2 == 0)
    def _():
        last_update(sa_ref[...])

    o_s = acc_ref[...] * (1.0 / l_ref[...])

    gts = _sigmoid(gt_ref[...]).T
    heads = []
    for r in range(NSA_GROUP):
        cs = slice(r * qb, (r + 1) * qb)
        heads.append(gts[r:r + 1, :] * o_c[:, cs] + gts[4 + r:5 + r, :] * o_s[:, cs]
                     + gts[8 + r:9 + r, :] * o_w[:, cs])
    y = jnp.concatenate(heads, axis=0).T
    o_ref[...] = (y * _silu(gc_ref[...].astype(F32))).astype(BF16)


def _nsa(z, vt, small, kc, vct, e128, ovt, b, s):
    n = b * s
    nqb = s // Q_BLOCK
    nc = s // CMP_STRIDE
    gw = NSA_GROUP * HEAD_DIM
    cols = NSA_GROUP * Q_BLOCK
    qspec = lambda off: pl.BlockSpec((Q_BLOCK, gw), lambda bi, g, i: (bi * nqb + i, off // gw + g))
    seq = lambda off: pl.BlockSpec((s, LANES), lambda bi, g, i: (bi, off // LANES + g))
    vseq = lambda row: pl.BlockSpec((LANES, s), lambda bi, g, i: (row // LANES + g, bi))
    return pl.pallas_call(
        functools.partial(_nsa_kernel, s_len=s, nc=nc),
        grid=(b, NSA_KV_HEADS, nqb),
        in_specs=[qspec(OFF_QC), qspec(OFF_GC), seq(OFF_KSD), seq(OFF_KWD),
                  vseq(ZT), vseq(ZT + 256),
                  pl.BlockSpec((None, None, nc, LANES), lambda bi, g, i: (bi, g, 0, 0)),
                  pl.BlockSpec((None, None, HEAD_DIM, nc), lambda bi, g, i: (bi, g, 0, 0)),
                  pl.BlockSpec((Q_BLOCK, LANES), lambda bi, g, i: (bi * nqb + i, 1 + g)),
                  pl.BlockSpec((s, LANES), lambda bi, g, i: (0, 0)),
                  pl.BlockSpec((LANES, nc), lambda bi, g, i: (0, 0))],
        out_specs=pl.BlockSpec((Q_BLOCK, gw), lambda bi, g, i: (bi * nqb + i, g)),
        out_shape=jax.ShapeDtypeStruct((n, BRANCH_WIDTH), BF16),
        scratch_shapes=[pltpu.VMEM((s, 2 * LANES), BF16),
                        pltpu.VMEM((min(SEL_TILE, s), cols), F32),
                        pltpu.VMEM((min(SEL_TILE, s), cols), F32),
                        pltpu.VMEM((LANES, Q_BLOCK), F32), pltpu.VMEM((LANES, Q_BLOCK), F32),
                        pltpu.VMEM((1, cols), F32),
                        pltpu.VMEM((1, cols), F32), pltpu.VMEM((HEAD_DIM, cols), F32)],
        compiler_params=pltpu.CompilerParams(
            dimension_semantics=("parallel", "parallel", "arbitrary"),
            vmem_limit_bytes=V7X_VMEM_LIMIT),
        name="nsa",
    )(z, z, z, z, vt, vt, kc, vct, small, e128, ovt)


def _merge_kernel(ya_ref, yb_ref, yc_ref, yd_ref, mg_ref, x_ref, wb_ref, wo_ref, o_ref):
    merged = None
    for nb, y_ref in enumerate((ya_ref, yb_ref, yc_ref, yd_ref)):
        gate = _sigmoid(mg_ref[:, nb * D_MODEL:(nb + 1) * D_MODEL].astype(F32))
        term = gate * _mm(y_ref[...], wb_ref[nb])
        merged = term if merged is None else merged + term
    o_ref[...] = x_ref[...] + _mm(merged.astype(BF16), wo_ref[...])


def _merge(ys, z, x2, wb, wo, tm):
    n = x2.shape[0]
    W = BRANCH_WIDTH
    yspec = pl.BlockSpec((tm, W), lambda i: (i, 0))
    return pl.pallas_call(
        _merge_kernel,
        grid=(n // tm,),
        in_specs=[yspec, yspec, yspec, yspec,
                  pl.BlockSpec((tm, N_BRANCHES * D_MODEL), lambda i: (i, 0)),
                  pl.BlockSpec((tm, D_MODEL), lambda i: (i, 0)),
                  pl.BlockSpec((N_BRANCHES, W, D_MODEL), lambda i: (0, 0, 0)),
                  pl.BlockSpec((D_MODEL, D_MODEL), lambda i: (0, 0))],
        out_specs=pl.BlockSpec((tm, D_MODEL), lambda i: (i, 0)),
        out_shape=jax.ShapeDtypeStruct((n, D_MODEL), F32),
        compiler_params=pltpu.CompilerParams(
            dimension_semantics=("parallel",), vmem_limit_bytes=V7X_VMEM_LIMIT),
        name="merge",
    )(*ys, z, x2, wb, wo)


def _block_diag(blocks):
    nb, a, b_ = blocks.shape
    eye = jnp.eye(nb, dtype=blocks.dtype)
    return jnp.einsum('nab,nm->namb', blocks, eye).reshape(nb * a, nb * b_)


def _mean_matrix(width):
    idx = np.arange(width) // HEAD_DIM
    return jnp.asarray((idx[:, None] == idx[None, :]).astype(np.float32) / HEAD_DIM, BF16)


def _layout_w_in(w_in, qn_a, kn_a, qn_c, kn_c):
    sizes = _split_sizes()
    offs = np.concatenate([[0], np.cumsum(sizes)])
    seg = lambda k: w_in[:, int(offs[k]):int(offs[k + 1])]
    (qa, ka, va, fa, ga, xb, gb, qc, kcc, vcc, ksc, vsc, kwc, vwc, gate_c, gc, ud, vd, gd, mg) = [
        seg(k) for k in range(len(sizes))]
    dup = lambda w: jnp.concatenate([w[:, :HEAD_DIM], w[:, :HEAD_DIM], w[:, HEAD_DIM:], w[:, HEAD_DIM:]], axis=1)
    pad = jnp.zeros((D_MODEL, 256), w_in.dtype)
    w_z = jnp.concatenate([mg, qa, ka, qc, dup(ksc), dup(kwc), va, dup(vsc), dup(vwc),
                           ga, xb, gb, gc, ud, vd, gd, kcc, vcc, pad], axis=1).astype(BF16)
    scale = LOG2E / math.sqrt(HEAD_DIM)
    gain = jnp.ones((ZW,), F32)
    gain = gain.at[OFF_QA:OFF_QA + ZT].set(jnp.tile(qn_a * scale, FOX_HEADS))
    gain = gain.at[OFF_KA:OFF_KA + ZT].set(jnp.tile(kn_a, FOX_HEADS))
    gain = gain.at[OFF_QC:OFF_QC + ZT].set(jnp.tile(qn_c * scale, NSA_HEADS))
    gain = gain.at[OFF_KSD:OFF_KSD + ZT].set(jnp.tile(kn_c, ZT // HEAD_DIM))
    gate3 = gate_c.reshape(D_MODEL, 3, NSA_KV_HEADS, NSA_GROUP)
    zpad = lambda w: jnp.pad(w, ((0, 0), (0, LANES - w.shape[1])))
    w_small = jnp.concatenate(
        [zpad(fa)] + [zpad(gate3[:, :, g, :].reshape(D_MODEL, 3 * NSA_GROUP)) for g in range(NSA_KV_HEADS)],
        axis=1).astype(BF16)
    return w_z, gain.reshape(1, ZW), w_small


def _layout_compress(cmp_pos, wk1, wk2, wv1, wv2, kn_c):
    half = CMP_BLOCK // 2
    eye4 = jnp.eye(4, dtype=F32)

    def w1_half(lo):
        k = wk1.reshape(CMP_BLOCK, HEAD_DIM, CMP_HIDDEN)[lo:lo + half]
        v = wv1.reshape(CMP_BLOCK, HEAD_DIM, CMP_HIDDEN)[lo:lo + half]
        stack = jnp.stack([k, k, v, v])
        return jnp.einsum('spdn,st->psdtn', stack, eye4).reshape(half * 4 * HEAD_DIM, 4 * CMP_HIDDEN).astype(BF16)

    def pos_half(lo):
        p = cmp_pos[lo:lo + half]
        return jnp.broadcast_to(p[:, None, :], (half, 4, HEAD_DIM)).reshape(1, half * 4 * HEAD_DIM)

    w2 = jnp.zeros((4, CMP_HIDDEN, 4, 2, HEAD_DIM), F32)
    for slot, w in enumerate((wk2, wk2, wv2, wv2)):
        w2 = w2.at[slot, :, slot, :, :].set(jnp.broadcast_to(w[:, None, :], (CMP_HIDDEN, 2, HEAD_DIM)))
    w2 = w2.reshape(4 * CMP_HIDDEN, ZT).astype(BF16)
    gain = jnp.tile(kn_c, ZT // HEAD_DIM).reshape(1, ZT)
    return pos_half(0), pos_half(half), w1_half(0), w1_half(half), w2, gain


@functools.lru_cache(maxsize=None)
def _nsa_consts(s):
    nc = s // CMP_STRIDE
    ns = s // SLC_BLOCK
    e = np.zeros((s, LANES), np.float32)
    e[np.arange(s), np.arange(s) // SLC_BLOCK] = 1.0
    c0 = np.arange(nc) * CMP_STRIDE
    s0 = np.arange(ns) * SLC_BLOCK
    ovl = np.minimum(c0[:, None] + CMP_BLOCK, s0[None, :] + SLC_BLOCK) - np.maximum(c0[:, None], s0[None, :])
    ovt = np.zeros((LANES, nc), np.float32)
    ovt[:ns, :] = (np.clip(ovl, 0, None) / CMP_BLOCK).T
    ovt[:, nc - 1] = 0.0
    return e, ovt


def _tile(s, pref):
    t = pref
    while s % t:
        t //= 2
    return t


def _branches(x2, b, s, norm_g, w_in, b_forget, qn_a, kn_a, conv_w, conv_b, w_rg_a, b_rg_a, w_rg_x,
              b_rg_x, lru_lambda, qn_c, kn_c, cmp_pos, wk1, wk2, wv1, wv2, ln_v_g, w_spatial,
              b_spatial):
    n = b * s
    W = BRANCH_WIDTH
    w_z, gain_row, w_small = _layout_w_in(w_in, qn_a, kn_a, qn_c, kn_c)
    bd = _mean_matrix(ZT)
    z, small = _inproj(x2, norm_g.reshape(1, D_MODEL), w_z, gain_row, bd, w_small, _tile(n, 1024))
    vt = _vtrans(z, _tile(s, 2048))

    bf_row = jnp.pad(b_forget, (0, LANES - FOX_HEADS)).reshape(1, LANES)
    caq, cak = _cprep(small, bf_row, b, s, _tile(s, 512))
    ys_a = _fox(z, vt, caq, cak, b, s, _tile(s, 512))

    ys_b = _lru(z, conv_w, conv_b.reshape(1, W), _block_diag(w_rg_a).astype(BF16), b_rg_a.reshape(1, W),
                _block_diag(w_rg_x).astype(BF16), b_rg_x.reshape(1, W), lru_lambda.reshape(1, W),
                b, s, _tile(s, 256))

    nc = s // CMP_STRIDE
    t16 = z[:, OFF_CMP:OFF_CMP + 4 * HEAD_DIM].reshape(b, nc, CMP_STRIDE * 4 * HEAD_DIM)
    ptop, pbot, w1t, w1b, w2c, cgain = _layout_compress(cmp_pos, wk1, wk2, wv1, wv2, kn_c)
    kc, vct = _compress(t16, ptop, pbot, w1t, w1b, w2c, bd, cgain, b, nc)
    e_np, ovt_np = _nsa_consts(s)
    ys_c = _nsa(z, vt, small, kc, vct, jnp.asarray(e_np, BF16), jnp.asarray(ovt_np, BF16), b, s)

    causal = jnp.tril(jnp.ones((SG_CHUNK, SG_CHUNK), F32))
    wm = (w_spatial * causal).astype(BF16)
    bs_full = jnp.broadcast_to(b_spatial.T[:, :, None], (SG_CHUNK, SG_GROUPS, HEAD_DIM)).reshape(SG_CHUNK, W)
    ys_d = _sgu(z, ln_v_g.reshape(1, W), wm, bs_full, n, _tile(s, 512))
    return (ys_a, ys_b, ys_c, ys_d), z


def _layer(x2, b, s, *params):
    ys, z = _branches(x2, b, s, *params[:-2])
    w_branch, w_out = params[-2:]
    return _merge(ys, z, x2, w_branch.astype(BF16), w_out.astype(BF16), _tile(b * s, 512))


def kernel(x, norm_g, w_in, b_forget, qn_a, kn_a, conv_w, conv_b, w_rg_a, b_rg_a, w_rg_x, b_rg_x,
           lru_lambda, qn_c, kn_c, cmp_pos, cmp_k_w1, cmp_k_w2, cmp_v_w1, cmp_v_w2, ln_v_g,
           w_spatial, b_spatial, w_branch, w_out):
    b, s, d = x.shape
    x2 = x.reshape(b * s, d)
    params = (norm_g, w_in, b_forget, qn_a, kn_a, conv_w, conv_b, w_rg_a, b_rg_a, w_rg_x, b_rg_x,
              lru_lambda, qn_c, kn_c, cmp_pos, cmp_k_w1, cmp_k_w2, cmp_v_w1, cmp_v_w2, ln_v_g,
              w_spatial, b_spatial, w_branch, w_out)
    for layer in range(norm_g.shape[0]):
        x2 = _layer(x2, b, s, *[p[layer] for p in params])
    return x2.reshape(b, s, d)
```

```python
import functools
import math

import numpy as np
import jax
import jax.numpy as jnp
from jax import lax
from jax.experimental import pallas as pl
from jax.experimental.pallas import tpu as pltpu

F32 = jnp.float32
BF16 = jnp.bfloat16

D_MODEL = 1024
HEAD_DIM = 64
BRANCH_WIDTH = 512
N_BRANCHES = 4
FOX_HEADS = 8
LRU_BLOCKS = 8
CONV_WIDTH = 4
LRU_C = 8.0
NSA_HEADS = 8
NSA_KV_HEADS = 2
NSA_GROUP = 4
CMP_BLOCK = 32
CMP_STRIDE = 16
CMP_HIDDEN = 128
SLC_BLOCK = 64
SLC_TOPK = 16
SLC_LOCAL = 2
SLC_FORCE_SCORE = 1e6
WINDOW = 512
Q_BLOCK = 128
SG_CHUNK = 128
SG_GROUPS = 8
NORM_EPS = 1e-6
NEG_INF = -1e30
SEL_NEG = -32768.0

LANES = 128
V7X_VMEM_LIMIT = 56 * 1024 * 1024
SEL_TILE = 512
LOG2E = 1.4426950408889634

ZT = 512
OFF_MG = 0
OFF_QA = 8 * ZT
OFF_KA = 9 * ZT
OFF_QC = 10 * ZT
OFF_KSD = 11 * ZT
OFF_KWD = 11 * ZT + 256
OFF_VA = 12 * ZT
OFF_VSD = 13 * ZT
OFF_VWD = 13 * ZT + 256
OFF_GA = 14 * ZT
OFF_XB = 15 * ZT
OFF_GB = 16 * ZT
OFF_GC = 17 * ZT
OFF_UD = 18 * ZT
OFF_VD = 19 * ZT
OFF_GD = 20 * ZT
OFF_CMP = 21 * ZT
ZW = 22 * ZT
IN_TN = 2 * ZT
NORM_TILES = (OFF_QA // IN_TN, OFF_QC // IN_TN)
SMALL_W = 384
VT_TILES = (OFF_VA // ZT, OFF_VSD // ZT)


def _mm(a, b):
    return jnp.dot(a, b, preferred_element_type=F32)


def _sigmoid(x):
    return 0.5 * jnp.tanh(0.5 * x) + 0.5


def _silu(x):
    return x * _sigmoid(x)


def _split_sizes():
    W, kv = BRANCH_WIDTH, NSA_KV_HEADS * HEAD_DIM
    return (W, W, W, FOX_HEADS, W, W, W, W, kv, kv, kv, kv, kv, kv, 3 * NSA_HEADS, W,
            W, W, W, N_BRANCHES * D_MODEL)


def _inproj_kernel(x_ref, g_ref, w_ref, gain_ref, bd_ref, ws_ref, z_ref, small_ref, xn_ref):
    j = pl.program_id(1)

    @pl.when(j == 0)
    def _():
        x = x_ref[...]
        ms = jnp.mean(x * x, axis=-1, keepdims=True)
        xn = ((x * lax.rsqrt(ms + NORM_EPS)) * g_ref[...]).astype(BF16)
        xn_ref[...] = xn
        small_ref[...] = _mm(xn, ws_ref[...])

    acc = _mm(xn_ref[...], w_ref[...])
    is_norm = functools.reduce(jnp.logical_or, [j == t for t in NORM_TILES])

    @pl.when(is_norm)
    def _():
        sq = (acc * acc).astype(BF16)
        ms = jnp.concatenate([_mm(sq[:, c:c + ZT], bd_ref[...]) for c in range(0, IN_TN, ZT)], axis=1)
        z_ref[...] = (acc * lax.rsqrt(ms + NORM_EPS) * gain_ref[...]).astype(BF16)

    @pl.when(jnp.logical_not(is_norm))
    def _():
        z_ref[...] = acc.astype(BF16)


def _inproj(x2, norm_g, w_z, gain_row, bd, w_small, tm):
    n = x2.shape[0]
    grid = (n // tm, ZW // IN_TN)
    return pl.pallas_call(
        _inproj_kernel,
        grid=grid,
        in_specs=[
            pl.BlockSpec((tm, D_MODEL), lambda i, j: (i, 0)),
            pl.BlockSpec((1, D_MODEL), lambda i, j: (0, 0)),
            pl.BlockSpec((D_MODEL, IN_TN), lambda i, j: (0, j)),
            pl.BlockSpec((1, IN_TN), lambda i, j: (0, j)),
            pl.BlockSpec((ZT, ZT), lambda i, j: (0, 0)),
            pl.BlockSpec((D_MODEL, SMALL_W), lambda i, j: (0, 0)),
        ],
        out_specs=[
            pl.BlockSpec((tm, IN_TN), lambda i, j: (i, j)),
            pl.BlockSpec((tm, SMALL_W), lambda i, j: (i, 0)),
        ],
        out_shape=[
            jax.ShapeDtypeStruct((n, ZW), BF16),
            jax.ShapeDtypeStruct((n, SMALL_W), F32),
        ],
        scratch_shapes=[pltpu.VMEM((tm, D_MODEL), BF16)],
        compiler_params=pltpu.CompilerParams(
            dimension_semantics=("parallel", "arbitrary"), vmem_limit_bytes=V7X_VMEM_LIMIT),
        name="inproj",
    )(x2, norm_g, w_z, gain_row, bd, w_small)


def _vtrans_kernel(z_ref, o_ref):
    o_ref[...] = z_ref[...].T


def _vtrans(z, tm):
    n = z.shape[0]
    step = VT_TILES[1] - VT_TILES[0]
    return pl.pallas_call(
        _vtrans_kernel,
        grid=(n // tm, len(VT_TILES)),
        in_specs=[pl.BlockSpec((tm, ZT), lambda i, j: (i, VT_TILES[0] + step * j))],
        out_specs=pl.BlockSpec((ZT, tm), lambda i, j: (j, i)),
        out_shape=jax.ShapeDtypeStruct((len(VT_TILES) * ZT, n), BF16),
        compiler_params=pltpu.CompilerParams(dimension_semantics=("parallel", "parallel")),
        name="vtrans",
    )(z)


def _cprep_kernel(fa_ref, bf_ref, pq_ref, pk_ref, oq_ref, ok_ref, caq_ref, cak_ref, carry_ref,
                  *, tt):
    @pl.when(pl.program_id(1) == 0)
    def _():
        carry_ref[...] = jnp.zeros_like(carry_ref)

    x = fa_ref[...] + bf_ref[...]
    c = jnp.minimum(x, 0.0) - jnp.log(1.0 + jnp.exp(-jnp.abs(x)))
    row = lax.broadcasted_iota(jnp.int32, c.shape, 0)
    d = 1
    while d < tt:
        c = c + jnp.where(row >= d, pltpu.roll(c, d, axis=0), 0.0)
        d *= 2
    c = c + carry_ref[...]
    carry_ref[...] = c[tt - 1:tt, :]
    c = c * LOG2E
    hi = c.astype(BF16)
    r1 = c - hi.astype(F32)
    mid = r1.astype(BF16)
    lo = (r1 - mid.astype(F32)).astype(BF16)
    parts = jnp.concatenate([hi, mid, lo], axis=1)
    caq_ref[...] = (_mm(parts, pq_ref[...]) + oq_ref[...]).astype(BF16)
    cak_ref[...] = (_mm(parts, pk_ref[...]) + ok_ref[...]).astype(BF16)


def _aug_lane(h):
    return (h // 2) * LANES + (h % 2) * 8


@functools.lru_cache(maxsize=None)
def _cprep_consts():
    pq = np.zeros((3 * LANES, 4 * LANES), np.float32)
    pk = np.zeros((3 * LANES, 4 * LANES), np.float32)
    oq = np.zeros((1, 4 * LANES), np.float32)
    ok = np.zeros((1, 4 * LANES), np.float32)
    for h in range(FOX_HEADS):
        base = _aug_lane(h)
        for part in range(3):
            pq[part * LANES + h, base + part] = 1.0
            pk[part * LANES + h, base + 3 + part] = -1.0
            oq[0, base + 3 + part] = 1.0
            ok[0, base + part] = 1.0
    return pq, pk, oq, ok


def _cprep(small, bf_row, b, s, tt):
    n = b * s
    nt = s // tt
    pq, pk, oq, ok = _cprep_consts()
    full = lambda shape: pl.BlockSpec(shape, lambda bi, ti: (0, 0))
    return pl.pallas_call(
        functools.partial(_cprep_kernel, tt=tt),
        grid=(b, nt),
        in_specs=[
            pl.BlockSpec((tt, LANES), lambda bi, ti: (bi * nt + ti, 0)),
            full((1, LANES)),
            full((3 * LANES, 4 * LANES)),
            full((3 * LANES, 4 * LANES)),
            full((1, 4 * LANES)),
            full((1, 4 * LANES)),
        ],
        out_specs=[
            pl.BlockSpec((tt, 4 * LANES), lambda bi, ti: (bi * nt + ti, 0)),
            pl.BlockSpec((tt, 4 * LANES), lambda bi, ti: (bi * nt + ti, 0)),
        ],
        out_shape=[jax.ShapeDtypeStruct((n, 4 * LANES), BF16)] * 2,
        scratch_shapes=[pltpu.VMEM((1, LANES), F32)],
        compiler_params=pltpu.CompilerParams(dimension_semantics=("parallel", "arbitrary")),
        name="cprep",
    )(small, bf_row, jnp.asarray(pq, BF16), jnp.asarray(pk, BF16), jnp.asarray(oq), jnp.asarray(ok))


def _fox_kernel(qi_ref, ki_ref, q_ref, caq_ref, k_ref, cak_ref, vt_ref, ga_ref, o_ref,
                lhs_ref, rhs_ref, m_ref, l_ref, acc_ref, *, t):
    pair = pl.program_id(1)
    qi = qi_ref[pair]
    ki = ki_ref[pair]
    n_pairs = FOX_HEADS // 2

    @pl.when(ki == 0)
    def _():
        lane = lax.broadcasted_iota(jnp.int32, (t, 2 * LANES), 1)
        head_of_lane = jnp.where(lane < LANES, lane // HEAD_DIM, (lane - LANES) // 8)
        for hp in range(n_pairs):
            cols = slice(hp * LANES, (hp + 1) * LANES)
            lhs = jnp.concatenate([q_ref[:, cols], caq_ref[:, cols]], axis=1).astype(F32)
            for hh in range(2):
                lhs_ref[2 * hp + hh] = jnp.where(head_of_lane == hh, lhs, 0.0).T.astype(BF16)
        m_ref[...] = jnp.full_like(m_ref, NEG_INF)
        l_ref[...] = jnp.zeros_like(l_ref)
        acc_ref[...] = jnp.zeros_like(acc_ref)

    def step(diag):
        for hp in range(n_pairs):
            cols = slice(hp * LANES, (hp + 1) * LANES)
            rhs_ref[hp] = jnp.concatenate([k_ref[:, cols], cak_ref[:, cols]], axis=1)

        s_next = _mm(rhs_ref[0], lhs_ref[0])
        for h in range(FOX_HEADS):
            s = s_next
            if h + 1 < FOX_HEADS:
                s_next = _mm(rhs_ref[(h + 1) // 2], lhs_ref[h + 1])
            if diag:
                key = lax.broadcasted_iota(jnp.int32, s.shape, 0)
                qry = lax.broadcasted_iota(jnp.int32, s.shape, 1)
                s = jnp.where(key <= qry, s, NEG_INF)
            m_prev = m_ref[h]
            m_new = jnp.maximum(m_prev, jnp.max(s, axis=0, keepdims=True))
            alpha = jnp.exp2(m_prev - m_new)
            p = jnp.exp2(s - m_new)
            l_ref[h] = alpha * l_ref[h] + jnp.sum(p, axis=0, keepdims=True)
            vt = vt_ref[h * HEAD_DIM:(h + 1) * HEAD_DIM, :]
            acc_ref[h] = alpha * acc_ref[h] + _mm(vt, p.astype(BF16))
            m_ref[h] = m_new

    @pl.when(ki < qi)
    def _():
        step(False)

    @pl.when(ki == qi)
    def _():
        step(True)
        for hp in range(n_pairs):
            cols = slice(hp * LANES, (hp + 1) * LANES)
            o_t = jnp.concatenate([acc_ref[2 * hp] / l_ref[2 * hp],
                                   acc_ref[2 * hp + 1] / l_ref[2 * hp + 1]], axis=0)
            o_ref[:, cols] = (o_t.T * _silu(ga_ref[:, cols].astype(F32))).astype(BF16)


def _fox(z, vt, caq, cak, b, s, t):
    n = b * s
    nq = s // t
    W = BRANCH_WIDTH
    qi_of = np.concatenate([np.full(q + 1, q) for q in range(nq)]).astype(np.int32)
    ki_of = np.concatenate([np.arange(q + 1) for q in range(nq)]).astype(np.int32)
    qmap = lambda off: (lambda bi, p, qi, ki: (bi * nq + qi[p], off // W))
    kmap = lambda off: (lambda bi, p, qi, ki: (bi * nq + ki[p], off // W))
    grid_spec = pltpu.PrefetchScalarGridSpec(
        num_scalar_prefetch=2,
        grid=(b, len(qi_of)),
        in_specs=[
            pl.BlockSpec((t, W), qmap(OFF_QA)),
            pl.BlockSpec((t, W), qmap(0)),
            pl.BlockSpec((t, W), kmap(OFF_KA)),
            pl.BlockSpec((t, W), kmap(0)),
            pl.BlockSpec((W, t), lambda bi, p, qi, ki: (0, bi * nq + ki[p])),
            pl.BlockSpec((t, W), qmap(OFF_GA)),
        ],
        out_specs=pl.BlockSpec((t, W), qmap(0)),
        scratch_shapes=[
            pltpu.VMEM((FOX_HEADS, 2 * LANES, t), BF16),
            pltpu.VMEM((FOX_HEADS // 2, t, 2 * LANES), BF16),
            pltpu.VMEM((FOX_HEADS, 1, t), F32),
            pltpu.VMEM((FOX_HEADS, 1, t), F32),
            pltpu.VMEM((FOX_HEADS, HEAD_DIM, t), F32),
        ],
    )
    return pl.pallas_call(
        functools.partial(_fox_kernel, t=t),
        grid_spec=grid_spec,
        out_shape=jax.ShapeDtypeStruct((n, W), BF16),
        compiler_params=pltpu.CompilerParams(
            dimension_semantics=("parallel", "arbitrary"), vmem_limit_bytes=V7X_VMEM_LIMIT),
        name="fox",
    )(jnp.asarray(qi_of), jnp.asarray(ki_of), z, caq, z, cak, vt, z)


def _lru_kernel(xb_ref, gb_ref, cw_ref, cb_ref, wa_ref, ba_ref, wx_ref, bx_ref, lam_ref, o_ref,
                xpad_ref, h_ref, *, tt):
    @pl.when(pl.program_id(1) == 0)
    def _():
        xpad_ref[0:8, :] = jnp.zeros((8, BRANCH_WIDTH), F32)
        h_ref[...] = jnp.zeros_like(h_ref)

    x = xb_ref[...].astype(F32)
    xpad_ref[8:tt + 8, :] = x
    xc = cb_ref[...] + cw_ref[3:4, :] * x
    for j in range(CONV_WIDTH - 1):
        sh = CONV_WIDTH - 1 - j
        xc = xc + cw_ref[j:j + 1, :] * xpad_ref[8 - sh:8 - sh + tt, :]
    xpad_ref[0:8, :] = x[tt - 8:tt, :]

    xcb = xc.astype(BF16)
    r = _sigmoid(_mm(xcb, wa_ref[...]) + ba_ref[...])
    ig = _sigmoid(_mm(xcb, wx_ref[...]) + bx_ref[...])
    nl = -lam_ref[...]
    softplus = jnp.maximum(nl, 0.0) + jnp.log(1.0 + jnp.exp(-jnp.abs(nl)))
    log_a = (-LRU_C) * r * softplus
    a = jnp.exp(log_a)
    bb = jnp.sqrt(1.0 - jnp.exp(2.0 * log_a)) * (ig * xc)

    row = lax.broadcasted_iota(jnp.int32, a.shape, 0)
    d = 1
    while d < tt:
        keep = row >= d
        bb = bb + a * jnp.where(keep, pltpu.roll(bb, d, axis=0), 0.0)
        a = a * jnp.where(keep, pltpu.roll(a, d, axis=0), 1.0)
        d *= 2
    h = bb + a * h_ref[...]
    h_ref[...] = h[tt - 1:tt, :]
    o_ref[...] = (h * _silu(gb_ref[...].astype(F32))).astype(BF16)


def _lru(z, conv_w, conv_b, wa_bd, b_a, wx_bd, b_x, lam, b, s, tt):
    n = b * s
    nt = s // tt
    W = BRANCH_WIDTH
    full = lambda shape: pl.BlockSpec(shape, lambda bi, ti: (0, 0))
    zb = lambda off: pl.BlockSpec((tt, W), lambda bi, ti: (bi * nt + ti, off // W))
    return pl.pallas_call(
        functools.partial(_lru_kernel, tt=tt),
        grid=(b, nt),
        in_specs=[zb(OFF_XB), zb(OFF_GB), full((CONV_WIDTH, W)), full((1, W)), full((W, W)),
                  full((1, W)), full((W, W)), full((1, W)), full((1, W))],
        out_specs=pl.BlockSpec((tt, W), lambda bi, ti: (bi * nt + ti, 0)),
        out_shape=jax.ShapeDtypeStruct((n, W), BF16),
        scratch_shapes=[pltpu.VMEM((tt + 8, W), F32), pltpu.VMEM((1, W), F32)],
        compiler_params=pltpu.CompilerParams(dimension_semantics=("parallel", "arbitrary")),
        name="lru",
    )(z, z, conv_w, conv_b, wa_bd, b_a, wx_bd, b_x, lam)


def _gelu(x):
    return 0.5 * x * (1.0 + lax.erf(x * (1.0 / math.sqrt(2.0))))


def _sgu_kernel(u_ref, v_ref, g_ref, lng_ref, wm_ref, bs_ref, o_ref, *, tr):
    u = _gelu(u_ref[...].astype(F32))
    v = _gelu(v_ref[...].astype(F32))
    mu = jnp.mean(v, axis=-1, keepdims=True)
    vc = v - mu
    vn = (vc * lax.rsqrt(jnp.mean(vc * vc, axis=-1, keepdims=True) + NORM_EPS) * lng_ref[...]).astype(BF16)
    lane = lax.broadcasted_iota(jnp.int32, (SG_CHUNK, LANES), 1)
    chunks = []
    for c in range(tr // SG_CHUNK):
        vch = vn[c * SG_CHUNK:(c + 1) * SG_CHUNK, :]
        pairs = []
        for p in range(SG_GROUPS // 2):
            vp = vch[:, p * LANES:(p + 1) * LANES]
            pairs.append(jnp.where(lane < HEAD_DIM, _mm(wm_ref[2 * p], vp), _mm(wm_ref[2 * p + 1], vp)))
        chunks.append(jnp.concatenate(pairs, axis=1) + bs_ref[...])
    mixed = jnp.concatenate(chunks, axis=0)
    o_ref[...] = (u * mixed * _silu(g_ref[...].astype(F32))).astype(BF16)


def _sgu(z, ln_g, wm, bs_full, n, tr):
    W = BRANCH_WIDTH
    zb = lambda off: pl.BlockSpec((tr, W), lambda i: (i, off // W))
    return pl.pallas_call(
        functools.partial(_sgu_kernel, tr=tr),
        grid=(n // tr,),
        in_specs=[zb(OFF_UD), zb(OFF_VD), zb(OFF_GD),
                  pl.BlockSpec((1, W), lambda i: (0, 0)),
                  pl.BlockSpec((SG_GROUPS, SG_CHUNK, SG_CHUNK), lambda i: (0, 0, 0)),
                  pl.BlockSpec((SG_CHUNK, W), lambda i: (0, 0))],
        out_specs=pl.BlockSpec((tr, W), lambda i: (i, 0)),
        out_shape=jax.ShapeDtypeStruct((n, W), BF16),
        compiler_params=pltpu.CompilerParams(dimension_semantics=("parallel",)),
        name="sgu",
    )(z, z, z, ln_g, wm, bs_full)


def _compress_kernel(t16_ref, ptop_ref, pbot_ref, w1t_ref, w1b_ref, w2_ref, bd_ref, gain_ref,
                     kc_ref, vct_ref, *, nc):
    t16 = t16_ref[...].astype(F32)
    a = _mm((t16 + ptop_ref[...]).astype(BF16), w1t_ref[...])
    bm = _mm((t16 + pbot_ref[...]).astype(BF16), w1b_ref[...])
    h = a + pltpu.roll(bm, nc - 1, axis=0)
    out = _mm(_silu(h).astype(BF16), w2_ref[...])
    ms = _mm((out * out).astype(BF16), bd_ref[...])
    kn = out * lax.rsqrt(ms + NORM_EPS) * gain_ref[...]
    out_t = out.T
    for g in range(NSA_KV_HEADS):
        kc_ref[g] = kn[:, g * LANES:(g + 1) * LANES].astype(BF16)
        vct_ref[g] = out_t[(2 + g) * LANES:(2 + g) * LANES + HEAD_DIM, :].astype(BF16)


def _compress(t16, ptop, pbot, w1t, w1b, w2, bd, gain, b, nc):
    kw = 16 * 4 * HEAD_DIM
    full = lambda shape: pl.BlockSpec(shape, lambda bi: tuple(0 for _ in shape))
    return pl.pallas_call(
        functools.partial(_compress_kernel, nc=nc),
        grid=(b,),
        in_specs=[pl.BlockSpec((None, nc, kw), lambda bi: (bi, 0, 0)),
                  full((1, kw)), full((1, kw)), full((kw, 4 * CMP_HIDDEN)), full((kw, 4 * CMP_HIDDEN)),
                  full((4 * CMP_HIDDEN, ZT)), full((ZT, ZT)), full((1, ZT))],
        out_specs=[pl.BlockSpec((None, NSA_KV_HEADS, nc, LANES), lambda bi: (bi, 0, 0, 0)),
                   pl.BlockSpec((None, NSA_KV_HEADS, HEAD_DIM, nc), lambda bi: (bi, 0, 0, 0))],
        out_shape=[jax.ShapeDtypeStruct((b, NSA_KV_HEADS, nc, LANES), BF16),
                   jax.ShapeDtypeStruct((b, NSA_KV_HEADS, HEAD_DIM, nc), BF16)],
        compiler_params=pltpu.CompilerParams(
            dimension_semantics=("parallel",), vmem_limit_bytes=V7X_VMEM_LIMIT),
        name="compress",
    )(t16, ptop, pbot, w1t, w1b, w2, bd, gain)


def _softmax_cols(s, mask):
    sm = jnp.where(mask, s, NEG_INF)
    m = jnp.max(sm, axis=0, keepdims=True)
    p = jnp.exp2(sm - jnp.where(m > 0.5 * NEG_INF, m, 0.0))
    l = jnp.sum(p, axis=0, keepdims=True)
    return p, jnp.where(l > 0.0, 1.0 / l, 0.0)


def _nsa_kernel(q_ref, gc_ref, ks_ref, kw_ref, vs_ref, vw_ref, kc_ref, vct_ref, gt_ref, e_ref, ovt_ref,
                o_ref, kse_ref, sa_ref, sb_ref, sc_ref, selt_ref, m_ref, l_ref, acc_ref, *, s_len, nc):
    i = pl.program_id(2)
    qb, cols = Q_BLOCK, NSA_GROUP * Q_BLOCK
    ns = s_len // SLC_BLOCK
    top_k = min(SLC_TOPK, ns)
    kt_w = min(SEL_TILE, s_len)
    span = min(WINDOW + Q_BLOCK, s_len)
    lane = lax.broadcasted_iota(jnp.int32, (qb, LANES), 1)

    @pl.when(i == 0)
    def _():
        kse_ref[:, 0:LANES] = ks_ref[...]
        kse_ref[:, LANES:2 * LANES] = e_ref[...]

    parts = []
    for r in range(NSA_GROUP):
        blk = q_ref[:, (r // 2) * LANES:(r // 2 + 1) * LANES].astype(F32)
        parts.append(jnp.where(lane // HEAD_DIM == r % 2, blk, 0.0).T.astype(BF16))
    qs_t = jnp.concatenate(parts, axis=1)
    tcol = i * qb + lax.broadcasted_iota(jnp.int32, (1, cols), 1) % qb

    s_c = _mm(kc_ref[...], qs_t)
    start = pl.multiple_of(jnp.maximum(i - WINDOW // qb, 0) * qb, qb)
    s_w = _mm(kw_ref[pl.ds(start, span), :], qs_t)

    cend = lax.broadcasted_iota(jnp.int32, s_c.shape, 0) * CMP_STRIDE + (CMP_BLOCK - 1)
    p_c, inv_c = _softmax_cols(s_c, cend <= tcol)
    p_c = p_c * inv_c
    o_c = _mm(vct_ref[...], p_c.astype(BF16))

    ps = p_c[:, 0:qb] + p_c[:, qb:2 * qb] + p_c[:, 2 * qb:3 * qb] + p_c[:, 3 * qb:4 * qb]
    ps_hi = ps.astype(BF16)
    ps_lo = (ps - ps_hi.astype(F32)).astype(BF16)
    imp = _mm(ovt_ref[...], ps_hi) + _mm(ovt_ref[...], ps_lo)

    jrow = lax.broadcasted_iota(jnp.int32, (LANES, qb), 0)
    jt = (i * qb + lax.broadcasted_iota(jnp.int32, (LANES, qb), 1)) // SLC_BLOCK
    valid = (jrow <= jt) & (jrow < ns)
    forced = (jrow == 0) | (valid & (jrow > jt - SLC_LOCAL))
    score = jnp.where(forced, SLC_FORCE_SCORE, jnp.where(valid, imp, -1.0))
    sc_ref[...] = jnp.where(jrow < ns, score, -2.0)

    def rank_blocks(n_cand):
        for k in range(n_cand // 8):
            mine = sc_ref[8 * k:8 * k + 8, :]
            sub = 8 * k + lax.broadcasted_iota(jnp.int32, (8, qb), 0)
            cnt = jnp.zeros((8, qb), F32)
            for jp in range(n_cand):
                other = sc_ref[jp:jp + 1, :]
                if jp < 8 * k:
                    beats = other >= mine
                elif jp >= 8 * k + 8:
                    beats = other > mine
                else:
                    beats = (other > mine) | ((other == mine) & (sub > jp))
                cnt = cnt + jnp.where(beats, 1.0, 0.0)
            selt_ref[8 * k:8 * k + 8, :] = jnp.where((cnt < top_k) & (mine >= 0.0), 0.0, SEL_NEG)
        if n_cand < LANES:
            selt_ref[n_cand:LANES, :] = jnp.full((LANES - n_cand, qb), SEL_NEG, F32)

    cand_step = 16
    blocks_per_qb = qb // SLC_BLOCK
    classes = list(range(cand_step, ns, cand_step)) + [ns]
    lo_i = 0
    for n_cand in classes:
        hi_i = n_cand // blocks_per_qb if n_cand < ns else s_len // qb
        if n_cand <= top_k:
            @pl.when((i >= lo_i) & (i < hi_i))
            def _():
                selt_ref[...] = jnp.where(valid, 0.0, SEL_NEG)
        else:
            pl.when((i >= lo_i) & (i < hi_i))(functools.partial(rank_blocks, n_cand))
        lo_i = hi_i
    selb = selt_ref[...].astype(BF16)
    lhs_t = jnp.concatenate([qs_t, jnp.concatenate([selb] * NSA_GROUP, axis=1)], axis=0)

    def sel_scores(kt):
        k0 = pl.multiple_of(kt * kt_w, kt_w)
        return _mm(kse_ref[pl.ds(k0, kt_w), :], lhs_t)

    def sel_update(kt, s):
        k0 = pl.multiple_of(kt * kt_w, kt_w)
        m_prev = m_ref[...]
        m_new = jnp.maximum(m_prev, jnp.max(s, axis=0, keepdims=True))
        alpha = jnp.exp2(m_prev - m_new)
        pr = jnp.exp2(s - m_new)
        l_ref[...] = alpha * l_ref[...] + jnp.sum(pr, axis=0, keepdims=True)
        acc_ref[...] = alpha * acc_ref[...] + _mm(vs_ref[0:HEAD_DIM, pl.ds(k0, kt_w)], pr.astype(BF16))
        m_ref[...] = m_new

    m_ref[...] = jnp.full_like(m_ref, NEG_INF)
    l_ref[...] = jnp.zeros_like(l_ref)
    acc_ref[...] = jnp.zeros_like(acc_ref)
    n_full = (i * qb) // kt_w
    sa_ref[...] = sel_scores(0)

    pos = start + lax.broadcasted_iota(jnp.int32, s_w.shape, 0)
    p_w, inv_w = _softmax_cols(s_w, (pos <= tcol) & (pos > tcol - WINDOW))
    o_w = _mm(vw_ref[0:HEAD_DIM, pl.ds(start, span)], p_w.astype(BF16)) * inv_w

    def body(k2, carry):
        sb_ref[...] = sel_scores(2 * k2 + 1)
        sel_update(2 * k2, sa_ref[...])
        sa_ref[...] = sel_scores(2 * k2 + 2)
        sel_update(2 * k2 + 1, sb_ref[...])
        return carry

    lax.fori_loop(0, n_full // 2, body, 0)

    def last_update(s_last):
        pos_k = n_full * kt_w + lax.broadcasted_iota(jnp.int32, s_last.shape, 0)
        sel_update(n_full, jnp.where(pos_k <= tcol, s_last, NEG_INF))

    @pl.when(n_full % 2 == 1)
    def _():
        sb_ref[...] = sel_scores(n_full)
        sel_update(n_full - 1, sa_ref[...])
        last_update(sb_ref[...])

    @pl.when(n_full % 2 == 0)
    def _():
        last_update(sa_ref[...])

    o_s = acc_ref[...] * (1.0 / l_ref[...])

    gts = _sigmoid(gt_ref[...]).T
    heads = []
    for r in range(NSA_GROUP):
        cs = slice(r * qb, (r + 1) * qb)
        heads.append(gts[r:r + 1, :] * o_c[:, cs] + gts[4 + r:5 + r, :] * o_s[:, cs]
                     + gts[8 + r:9 + r, :] * o_w[:, cs])
    y = jnp.concatenate(heads, axis=0).T
    o_ref[...] = (y * _silu(gc_ref[...].astype(F32))).astype(BF16)


def _nsa(z, vt, small, kc, vct, e128, ovt, b, s):
    n = b * s
    nqb = s // Q_BLOCK
    nc = s // CMP_STRIDE
    gw = NSA_GROUP * HEAD_DIM
    cols = NSA_GROUP * Q_BLOCK
    qspec = lambda off: pl.BlockSpec((Q_BLOCK, gw), lambda bi, g, i: (bi * nqb + i, off // gw + g))
    seq = lambda off: pl.BlockSpec((s, LANES), lambda bi, g, i: (bi, off // LANES + g))
    vseq = lambda row: pl.BlockSpec((LANES, s), lambda bi, g, i: (row // LANES + g, bi))
    return pl.pallas_call(
        functools.partial(_nsa_kernel, s_len=s, nc=nc),
        grid=(b, NSA_KV_HEADS, nqb),
        in_specs=[qspec(OFF_QC), qspec(OFF_GC), seq(OFF_KSD), seq(OFF_KWD),
                  vseq(ZT), vseq(ZT + 256),
                  pl.BlockSpec((None, None, nc, LANES), lambda bi, g, i: (bi, g, 0, 0)),
                  pl.BlockSpec((None, None, HEAD_DIM, nc), lambda bi, g, i: (bi, g, 0, 0)),
                  pl.BlockSpec((Q_BLOCK, LANES), lambda bi, g, i: (bi * nqb + i, 1 + g)),
                  pl.BlockSpec((s, LANES), lambda bi, g, i: (0, 0)),
                  pl.BlockSpec((LANES, nc), lambda bi, g, i: (0, 0))],
        out_specs=pl.BlockSpec((Q_BLOCK, gw), lambda bi, g, i: (bi * nqb + i, g)),
        out_shape=jax.ShapeDtypeStruct((n, BRANCH_WIDTH), BF16),
        scratch_shapes=[pltpu.VMEM((s, 2 * LANES), BF16),
                        pltpu.VMEM((min(SEL_TILE, s), cols), F32),
                        pltpu.VMEM((min(SEL_TILE, s), cols), F32),
                        pltpu.VMEM((LANES, Q_BLOCK), F32), pltpu.VMEM((LANES, Q_BLOCK), F32),
                        pltpu.VMEM((1, cols), F32),
                        pltpu.VMEM((1, cols), F32), pltpu.VMEM((HEAD_DIM, cols), F32)],
        compiler_params=pltpu.CompilerParams(
            dimension_semantics=("parallel", "parallel", "arbitrary"),
            vmem_limit_bytes=V7X_VMEM_LIMIT),
        name="nsa",
    )(z, z, z, z, vt, vt, kc, vct, small, e128, ovt)


def _merge_kernel(ya_ref, yb_ref, yc_ref, yd_ref, mg_ref, x_ref, wb_ref, wo_ref, o_ref):
    merged = None
    for nb, y_ref in enumerate((ya_ref, yb_ref, yc_ref, yd_ref)):
        gate = _sigmoid(mg_ref[:, nb * D_MODEL:(nb + 1) * D_MODEL].astype(F32))
        term = gate * _mm(y_ref[...], wb_ref[nb])
        merged = term if merged is None else merged + term
    o_ref[...] = x_ref[...] + _mm(merged.astype(BF16), wo_ref[...])


def _merge(ys, z, x2, wb, wo, tm):
    n = x2.shape[0]
    W = BRANCH_WIDTH
    yspec = pl.BlockSpec((tm, W), lambda i: (i, 0))
    return pl.pallas_call(
        _merge_kernel,
        grid=(n // tm,),
        in_specs=[yspec, yspec, yspec, yspec,
                  pl.BlockSpec((tm, N_BRANCHES * D_MODEL), lambda i: (i, 0)),
                  pl.BlockSpec((tm, D_MODEL), lambda i: (i, 0)),
                  pl.BlockSpec((N_BRANCHES, W, D_MODEL), lambda i: (0, 0, 0)),
                  pl.BlockSpec((D_MODEL, D_MODEL), lambda i: (0, 0))],
        out_specs=pl.BlockSpec((tm, D_MODEL), lambda i: (i, 0)),
        out_shape=jax.ShapeDtypeStruct((n, D_MODEL), F32),
        compiler_params=pltpu.CompilerParams(
            dimension_semantics=("parallel",), vmem_limit_bytes=V7X_VMEM_LIMIT),
        name="merge",
    )(*ys, z, x2, wb, wo)


def _block_diag(blocks):
    nb, a, b_ = blocks.shape
    eye = jnp.eye(nb, dtype=blocks.dtype)
    return jnp.einsum('nab,nm->namb', blocks, eye).reshape(nb * a, nb * b_)


def _mean_matrix(width):
    idx = np.arange(width) // HEAD_DIM
    return jnp.asarray((idx[:, None] == idx[None, :]).astype(np.float32) / HEAD_DIM, BF16)


def _layout_w_in(w_in, qn_a, kn_a, qn_c, kn_c):
    sizes = _split_sizes()
    offs = np.concatenate([[0], np.cumsum(sizes)])
    seg = lambda k: w_in[:, int(offs[k]):int(offs[k + 1])]
    (qa, ka, va, fa, ga, xb, gb, qc, kcc, vcc, ksc, vsc, kwc, vwc, gate_c, gc, ud, vd, gd, mg) = [
        seg(k) for k in range(len(sizes))]
    dup = lambda w: jnp.concatenate([w[:, :HEAD_DIM], w[:, :HEAD_DIM], w[:, HEAD_DIM:], w[:, HEAD_DIM:]], axis=1)
    pad = jnp.zeros((D_MODEL, 256), w_in.dtype)
    w_z = jnp.concatenate([mg, qa, ka, qc, dup(ksc), dup(kwc), va, dup(vsc), dup(vwc),
                           ga, xb, gb, gc, ud, vd, gd, kcc, vcc, pad], axis=1).astype(BF16)
    scale = LOG2E / math.sqrt(HEAD_DIM)
    gain = jnp.ones((ZW,), F32)
    gain = gain.at[OFF_QA:OFF_QA + ZT].set(jnp.tile(qn_a * scale, FOX_HEADS))
    gain = gain.at[OFF_KA:OFF_KA + ZT].set(jnp.tile(kn_a, FOX_HEADS))
    gain = gain.at[OFF_QC:OFF_QC + ZT].set(jnp.tile(qn_c * scale, NSA_HEADS))
    gain = gain.at[OFF_KSD:OFF_KSD + ZT].set(jnp.tile(kn_c, ZT // HEAD_DIM))
    gate3 = gate_c.reshape(D_MODEL, 3, NSA_KV_HEADS, NSA_GROUP)
    zpad = lambda w: jnp.pad(w, ((0, 0), (0, LANES - w.shape[1])))
    w_small = jnp.concatenate(
        [zpad(fa)] + [zpad(gate3[:, :, g, :].reshape(D_MODEL, 3 * NSA_GROUP)) for g in range(NSA_KV_HEADS)],
        axis=1).astype(BF16)
    return w_z, gain.reshape(1, ZW), w_small


def _layout_compress(cmp_pos, wk1, wk2, wv1, wv2, kn_c):
    half = CMP_BLOCK // 2
    eye4 = jnp.eye(4, dtype=F32)

    def w1_half(lo):
        k = wk1.reshape(CMP_BLOCK, HEAD_DIM, CMP_HIDDEN)[lo:lo + half]
        v = wv1.reshape(CMP_BLOCK, HEAD_DIM, CMP_HIDDEN)[lo:lo + half]
        stack = jnp.stack([k, k, v, v])
        return jnp.einsum('spdn,st->psdtn', stack, eye4).reshape(half * 4 * HEAD_DIM, 4 * CMP_HIDDEN).astype(BF16)

    def pos_half(lo):
        p = cmp_pos[lo:lo + half]
        return jnp.broadcast_to(p[:, None, :], (half, 4, HEAD_DIM)).reshape(1, half * 4 * HEAD_DIM)

    w2 = jnp.zeros((4, CMP_HIDDEN, 4, 2, HEAD_DIM), F32)
    for slot, w in enumerate((wk2, wk2, wv2, wv2)):
        w2 = w2.at[slot, :, slot, :, :].set(jnp.broadcast_to(w[:, None, :], (CMP_HIDDEN, 2, HEAD_DIM)))
    w2 = w2.reshape(4 * CMP_HIDDEN, ZT).astype(BF16)
    gain = jnp.tile(kn_c, ZT // HEAD_DIM).reshape(1, ZT)
    return pos_half(0), pos_half(half), w1_half(0), w1_half(half), w2, gain


@functools.lru_cache(maxsize=None)
def _nsa_consts(s):
    nc = s // CMP_STRIDE
    ns = s // SLC_BLOCK
    e = np.zeros((s, LANES), np.float32)
    e[np.arange(s), np.arange(s) // SLC_BLOCK] = 1.0
    c0 = np.arange(nc) * CMP_STRIDE
    s0 = np.arange(ns) * SLC_BLOCK
    ovl = np.minimum(c0[:, None] + CMP_BLOCK, s0[None, :] + SLC_BLOCK) - np.maximum(c0[:, None], s0[None, :])
    ovt = np.zeros((LANES, nc), np.float32)
    ovt[:ns, :] = (np.clip(ovl, 0, None) / CMP_BLOCK).T
    ovt[:, nc - 1] = 0.0
    return e, ovt


def _tile(s, pref):
    t = pref
    while s % t:
        t //= 2
    return t


def _branches(x2, b, s, norm_g, w_in, b_forget, qn_a, kn_a, conv_w, conv_b, w_rg_a, b_rg_a, w_rg_x,
              b_rg_x, lru_lambda, qn_c, kn_c, cmp_pos, wk1, wk2, wv1, wv2, ln_v_g, w_spatial,
              b_spatial):
    n = b * s
    W = BRANCH_WIDTH
    w_z, gain_row, w_small = _layout_w_in(w_in, qn_a, kn_a, qn_c, kn_c)
    bd = _mean_matrix(ZT)
    z, small = _inproj(x2, norm_g.reshape(1, D_MODEL), w_z, gain_row, bd, w_small, _tile(n, 2048))
    vt = _vtrans(z, _tile(s, 2048))

    bf_row = jnp.pad(b_forget, (0, LANES - FOX_HEADS)).reshape(1, LANES)
    caq, cak = _cprep(small, bf_row, b, s, _tile(s, 512))
    ys_a = _fox(z, vt, caq, cak, b, s, _tile(s, 512))

    ys_b = _lru(z, conv_w, conv_b.reshape(1, W), _block_diag(w_rg_a).astype(BF16), b_rg_a.reshape(1, W),
                _block_diag(w_rg_x).astype(BF16), b_rg_x.reshape(1, W), lru_lambda.reshape(1, W),
                b, s, _tile(s, 256))

    nc = s // CMP_STRIDE
    t16 = z[:, OFF_CMP:OFF_CMP + 4 * HEAD_DIM].reshape(b, nc, CMP_STRIDE * 4 * HEAD_DIM)
    ptop, pbot, w1t, w1b, w2c, cgain = _layout_compress(cmp_pos, wk1, wk2, wv1, wv2, kn_c)
    kc, vct = _compress(t16, ptop, pbot, w1t, w1b, w2c, bd, cgain, b, nc)
    e_np, ovt_np = _nsa_consts(s)
    ys_c = _nsa(z, vt, small, kc, vct, jnp.asarray(e_np, BF16), jnp.asarray(ovt_np, BF16), b, s)

    causal = jnp.tril(jnp.ones((SG_CHUNK, SG_CHUNK), F32))
    wm = (w_spatial * causal).astype(BF16)
    bs_full = jnp.broadcast_to(b_spatial.T[:, :, None], (SG_CHUNK, SG_GROUPS, HEAD_DIM)).reshape(SG_CHUNK, W)
    ys_d = _sgu(z, ln_v_g.reshape(1, W), wm, bs_full, n, _tile(s, 1024))
    return (ys_a, ys_b, ys_c, ys_d), z


def _layer(x2, b, s, *params):
    ys, z = _branches(x2, b, s, *params[:-2])
    w_branch, w_out = params[-2:]
    return _merge(ys, z, x2, w_branch.astype(BF16), w_out.astype(BF16), _tile(b * s, 512))


def kernel(x, norm_g, w_in, b_forget, qn_a, kn_a, conv_w, conv_b, w_rg_a, b_rg_a, w_rg_x, b_rg_x,
           lru_lambda, qn_c, kn_c, cmp_pos, cmp_k_w1, cmp_k_w2, cmp_v_w1, cmp_v_w2, ln_v_g,
           w_spatial, b_spatial, w_branch, w_out):
    b, s, d = x.shape
    x2 = x.reshape(b * s, d)
    params = (norm_g, w_in, b_forget, qn_a, kn_a, conv_w, conv_b, w_rg_a, b_rg_a, w_rg_x, b_rg_x,
              lru_lambda, qn_c, kn_c, cmp_pos, cmp_k_w1, cmp_k_w2, cmp_v_w1, cmp_v_w2, ln_v_g,
              w_spatial, b_spatial, w_branch, w_out)
    for layer in range(norm_g.shape[0]):
        x2 = _layer(x2, b, s, *[p[layer] for p in params])
    return x2.reshape(b, s, d)
```

```python
import functools
import math

import numpy as np
import jax
import jax.numpy as jnp
from jax import lax
from jax.experimental import pallas as pl
from jax.experimental.pallas import tpu as pltpu

F32 = jnp.float32
BF16 = jnp.bfloat16

D_MODEL = 1024
HEAD_DIM = 64
BRANCH_WIDTH = 512
N_BRANCHES = 4
FOX_HEADS = 8
LRU_BLOCKS = 8
CONV_WIDTH = 4
LRU_C = 8.0
NSA_HEADS = 8
NSA_KV_HEADS = 2
NSA_GROUP = 4
CMP_BLOCK = 32
CMP_STRIDE = 16
CMP_HIDDEN = 128
SLC_BLOCK = 64
SLC_TOPK = 16
SLC_LOCAL = 2
SLC_FORCE_SCORE = 1e6
WINDOW = 512
Q_BLOCK = 128
SG_CHUNK = 128
SG_GROUPS = 8
NORM_EPS = 1e-6
NEG_INF = -1e30
SEL_NEG = -32768.0

LANES = 128
V7X_VMEM_LIMIT = 56 * 1024 * 1024
SEL_TILE = 512
LOG2E = 1.4426950408889634

ZT = 512
OFF_MG = 0
OFF_QA = 8 * ZT
OFF_KA = 9 * ZT
OFF_QC = 10 * ZT
OFF_KSD = 11 * ZT
OFF_KWD = 11 * ZT + 256
OFF_VA = 12 * ZT
OFF_VSD = 13 * ZT
OFF_VWD = 13 * ZT + 256
OFF_GA = 14 * ZT
OFF_XB = 15 * ZT
OFF_GB = 16 * ZT
OFF_GC = 17 * ZT
OFF_UD = 18 * ZT
OFF_VD = 19 * ZT
OFF_GD = 20 * ZT
OFF_CMP = 21 * ZT
ZW = 22 * ZT
IN_TN = 2 * ZT
NORM_TILES = (OFF_QA // IN_TN, OFF_QC // IN_TN)
SMALL_W = 384
VT_TILES = (OFF_VA // ZT, OFF_VSD // ZT)


def _mm(a, b):
    return jnp.dot(a, b, preferred_element_type=F32)


def _sigmoid(x):
    return 0.5 * jnp.tanh(0.5 * x) + 0.5


def _silu(x):
    return x * _sigmoid(x)


def _split_sizes():
    W, kv = BRANCH_WIDTH, NSA_KV_HEADS * HEAD_DIM
    return (W, W, W, FOX_HEADS, W, W, W, W, kv, kv, kv, kv, kv, kv, 3 * NSA_HEADS, W,
            W, W, W, N_BRANCHES * D_MODEL)


def _inproj_kernel(x_ref, g_ref, w_ref, gain_ref, bd_ref, ws_ref, z_ref, small_ref, xn_ref):
    j = pl.program_id(1)

    @pl.when(j == 0)
    def _():
        x = x_ref[...]
        ms = jnp.mean(x * x, axis=-1, keepdims=True)
        xn = ((x * lax.rsqrt(ms + NORM_EPS)) * g_ref[...]).astype(BF16)
        xn_ref[...] = xn
        small_ref[...] = _mm(xn, ws_ref[...])

    acc = _mm(xn_ref[...], w_ref[...])
    is_norm = functools.reduce(jnp.logical_or, [j == t for t in NORM_TILES])

    @pl.when(is_norm)
    def _():
        sq = (acc * acc).astype(BF16)
        ms = jnp.concatenate([_mm(sq[:, c:c + ZT], bd_ref[...]) for c in range(0, IN_TN, ZT)], axis=1)
        z_ref[...] = (acc * lax.rsqrt(ms + NORM_EPS) * gain_ref[...]).astype(BF16)

    @pl.when(jnp.logical_not(is_norm))
    def _():
        z_ref[...] = acc.astype(BF16)


def _inproj(x2, norm_g, w_z, gain_row, bd, w_small, tm):
    n = x2.shape[0]
    grid = (n // tm, ZW // IN_TN)
    return pl.pallas_call(
        _inproj_kernel,
        grid=grid,
        in_specs=[
            pl.BlockSpec((tm, D_MODEL), lambda i, j: (i, 0)),
            pl.BlockSpec((1, D_MODEL), lambda i, j: (0, 0)),
            pl.BlockSpec((D_MODEL, IN_TN), lambda i, j: (0, j)),
            pl.BlockSpec((1, IN_TN), lambda i, j: (0, j)),
            pl.BlockSpec((ZT, ZT), lambda i, j: (0, 0)),
            pl.BlockSpec((D_MODEL, SMALL_W), lambda i, j: (0, 0)),
        ],
        out_specs=[
            pl.BlockSpec((tm, IN_TN), lambda i, j: (i, j)),
            pl.BlockSpec((tm, SMALL_W), lambda i, j: (i, 0)),
        ],
        out_shape=[
            jax.ShapeDtypeStruct((n, ZW), BF16),
            jax.ShapeDtypeStruct((n, SMALL_W), F32),
        ],
        scratch_shapes=[pltpu.VMEM((tm, D_MODEL), BF16)],
        compiler_params=pltpu.CompilerParams(
            dimension_semantics=("parallel", "arbitrary"), vmem_limit_bytes=V7X_VMEM_LIMIT),
        name="inproj",
    )(x2, norm_g, w_z, gain_row, bd, w_small)


def _vtrans_kernel(z_ref, o_ref):
    o_ref[...] = z_ref[...].T


def _vtrans(z, tm):
    n = z.shape[0]
    step = VT_TILES[1] - VT_TILES[0]
    return pl.pallas_call(
        _vtrans_kernel,
        grid=(n // tm, len(VT_TILES)),
        in_specs=[pl.BlockSpec((tm, ZT), lambda i, j: (i, VT_TILES[0] + step * j))],
        out_specs=pl.BlockSpec((ZT, tm), lambda i, j: (j, i)),
        out_shape=jax.ShapeDtypeStruct((len(VT_TILES) * ZT, n), BF16),
        compiler_params=pltpu.CompilerParams(dimension_semantics=("parallel", "parallel")),
        name="vtrans",
    )(z)


def _cprep_kernel(fa_ref, bf_ref, pq_ref, pk_ref, oq_ref, ok_ref, caq_ref, cak_ref, carry_ref,
                  *, tt):
    @pl.when(pl.program_id(1) == 0)
    def _():
        carry_ref[...] = jnp.zeros_like(carry_ref)

    x = fa_ref[...] + bf_ref[...]
    c = jnp.minimum(x, 0.0) - jnp.log(1.0 + jnp.exp(-jnp.abs(x)))
    row = lax.broadcasted_iota(jnp.int32, c.shape, 0)
    d = 1
    while d < tt:
        c = c + jnp.where(row >= d, pltpu.roll(c, d, axis=0), 0.0)
        d *= 2
    c = c + carry_ref[...]
    carry_ref[...] = c[tt - 1:tt, :]
    c = c * LOG2E
    hi = c.astype(BF16)
    r1 = c - hi.astype(F32)
    mid = r1.astype(BF16)
    lo = (r1 - mid.astype(F32)).astype(BF16)
    parts = jnp.concatenate([hi, mid, lo], axis=1)
    caq_ref[...] = (_mm(parts, pq_ref[...]) + oq_ref[...]).astype(BF16)
    cak_ref[...] = (_mm(parts, pk_ref[...]) + ok_ref[...]).astype(BF16)


def _aug_lane(h):
    return (h // 2) * LANES + (h % 2) * 8


@functools.lru_cache(maxsize=None)
def _cprep_consts():
    pq = np.zeros((3 * LANES, 4 * LANES), np.float32)
    pk = np.zeros((3 * LANES, 4 * LANES), np.float32)
    oq = np.zeros((1, 4 * LANES), np.float32)
    ok = np.zeros((1, 4 * LANES), np.float32)
    for h in range(FOX_HEADS):
        base = _aug_lane(h)
        for part in range(3):
            pq[part * LANES + h, base + part] = 1.0
            pk[part * LANES + h, base + 3 + part] = -1.0
            oq[0, base + 3 + part] = 1.0
            ok[0, base + part] = 1.0
    return pq, pk, oq, ok


def _cprep(small, bf_row, b, s, tt):
    n = b * s
    nt = s // tt
    pq, pk, oq, ok = _cprep_consts()
    full = lambda shape: pl.BlockSpec(shape, lambda bi, ti: (0, 0))
    return pl.pallas_call(
        functools.partial(_cprep_kernel, tt=tt),
        grid=(b, nt),
        in_specs=[
            pl.BlockSpec((tt, LANES), lambda bi, ti: (bi * nt + ti, 0)),
            full((1, LANES)),
            full((3 * LANES, 4 * LANES)),
            full((3 * LANES, 4 * LANES)),
            full((1, 4 * LANES)),
            full((1, 4 * LANES)),
        ],
        out_specs=[
            pl.BlockSpec((tt, 4 * LANES), lambda bi, ti: (bi * nt + ti, 0)),
            pl.BlockSpec((tt, 4 * LANES), lambda bi, ti: (bi * nt + ti, 0)),
        ],
        out_shape=[jax.ShapeDtypeStruct((n, 4 * LANES), BF16)] * 2,
        scratch_shapes=[pltpu.VMEM((1, LANES), F32)],
        compiler_params=pltpu.CompilerParams(dimension_semantics=("parallel", "arbitrary")),
        name="cprep",
    )(small, bf_row, jnp.asarray(pq, BF16), jnp.asarray(pk, BF16), jnp.asarray(oq), jnp.asarray(ok))


def _fox_kernel(qi_ref, ki_ref, q_ref, caq_ref, k_ref, cak_ref, vt_ref, ga_ref, o_ref,
                lhs_ref, rhs_ref, m_ref, l_ref, acc_ref, *, t):
    pair = pl.program_id(1)
    qi = qi_ref[pair]
    ki = ki_ref[pair]
    n_pairs = FOX_HEADS // 2

    @pl.when(ki == 0)
    def _():
        lane = lax.broadcasted_iota(jnp.int32, (t, 2 * LANES), 1)
        head_of_lane = jnp.where(lane < LANES, lane // HEAD_DIM, (lane - LANES) // 8)
        for hp in range(n_pairs):
            cols = slice(hp * LANES, (hp + 1) * LANES)
            lhs = jnp.concatenate([q_ref[:, cols], caq_ref[:, cols]], axis=1).astype(F32)
            for hh in range(2):
                lhs_ref[2 * hp + hh] = jnp.where(head_of_lane == hh, lhs, 0.0).T.astype(BF16)
        m_ref[...] = jnp.full_like(m_ref, NEG_INF)
        l_ref[...] = jnp.zeros_like(l_ref)
        acc_ref[...] = jnp.zeros_like(acc_ref)

    def step(diag):
        for hp in range(n_pairs):
            cols = slice(hp * LANES, (hp + 1) * LANES)
            rhs_ref[hp] = jnp.concatenate([k_ref[:, cols], cak_ref[:, cols]], axis=1)

        s_next = _mm(rhs_ref[0], lhs_ref[0])
        for h in range(FOX_HEADS):
            s = s_next
            if h + 1 < FOX_HEADS:
                s_next = _mm(rhs_ref[(h + 1) // 2], lhs_ref[h + 1])
            if diag:
                key = lax.broadcasted_iota(jnp.int32, s.shape, 0)
                qry = lax.broadcasted_iota(jnp.int32, s.shape, 1)
                s = jnp.where(key <= qry, s, NEG_INF)
            m_prev = m_ref[h]
            m_new = jnp.maximum(m_prev, jnp.max(s, axis=0, keepdims=True))
            alpha = jnp.exp2(m_prev - m_new)
            p = jnp.exp2(s - m_new)
            l_ref[h] = alpha * l_ref[h] + jnp.sum(p, axis=0, keepdims=True)
            vt = vt_ref[h * HEAD_DIM:(h + 1) * HEAD_DIM, :]
            acc_ref[h] = alpha * acc_ref[h] + _mm(vt, p.astype(BF16))
            m_ref[h] = m_new

    @pl.when(ki < qi)
    def _():
        step(False)

    @pl.when(ki == qi)
    def _():
        step(True)
        for hp in range(n_pairs):
            cols = slice(hp * LANES, (hp + 1) * LANES)
            o_t = jnp.concatenate([acc_ref[2 * hp] / l_ref[2 * hp],
                                   acc_ref[2 * hp + 1] / l_ref[2 * hp + 1]], axis=0)
            o_ref[:, cols] = (o_t.T * _silu(ga_ref[:, cols].astype(F32))).astype(BF16)


def _fox(z, vt, caq, cak, b, s, t):
    n = b * s
    nq = s // t
    W = BRANCH_WIDTH
    qi_of = np.concatenate([np.full(q + 1, q) for q in range(nq)]).astype(np.int32)
    ki_of = np.concatenate([np.arange(q + 1) for q in range(nq)]).astype(np.int32)
    qmap = lambda off: (lambda bi, p, qi, ki: (bi * nq + qi[p], off // W))
    kmap = lambda off: (lambda bi, p, qi, ki: (bi * nq + ki[p], off // W))
    grid_spec = pltpu.PrefetchScalarGridSpec(
        num_scalar_prefetch=2,
        grid=(b, len(qi_of)),
        in_specs=[
            pl.BlockSpec((t, W), qmap(OFF_QA)),
            pl.BlockSpec((t, W), qmap(0)),
            pl.BlockSpec((t, W), kmap(OFF_KA)),
            pl.BlockSpec((t, W), kmap(0)),
            pl.BlockSpec((W, t), lambda bi, p, qi, ki: (0, bi * nq + ki[p])),
            pl.BlockSpec((t, W), qmap(OFF_GA)),
        ],
        out_specs=pl.BlockSpec((t, W), qmap(0)),
        scratch_shapes=[
            pltpu.VMEM((FOX_HEADS, 2 * LANES, t), BF16),
            pltpu.VMEM((FOX_HEADS // 2, t, 2 * LANES), BF16),
            pltpu.VMEM((FOX_HEADS, 1, t), F32),
            pltpu.VMEM((FOX_HEADS, 1, t), F32),
            pltpu.VMEM((FOX_HEADS, HEAD_DIM, t), F32),
        ],
    )
    return pl.pallas_call(
        functools.partial(_fox_kernel, t=t),
        grid_spec=grid_spec,
        out_shape=jax.ShapeDtypeStruct((n, W), BF16),
        compiler_params=pltpu.CompilerParams(
            dimension_semantics=("parallel", "arbitrary"), vmem_limit_bytes=V7X_VMEM_LIMIT),
        name="fox",
    )(jnp.asarray(qi_of), jnp.asarray(ki_of), z, caq, z, cak, vt, z)


def _lru_kernel(xb_ref, gb_ref, cw_ref, cb_ref, wa_ref, ba_ref, wx_ref, bx_ref, lam_ref, o_ref,
                xpad_ref, h_ref, *, tt):
    @pl.when(pl.program_id(1) == 0)
    def _():
        xpad_ref[0:8, :] = jnp.zeros((8, BRANCH_WIDTH), F32)
        h_ref[...] = jnp.zeros_like(h_ref)

    x = xb_ref[...].astype(F32)
    xpad_ref[8:tt + 8, :] = x
    xc = cb_ref[...] + cw_ref[3:4, :] * x
    for j in range(CONV_WIDTH - 1):
        sh = CONV_WIDTH - 1 - j
        xc = xc + cw_ref[j:j + 1, :] * xpad_ref[8 - sh:8 - sh + tt, :]
    xpad_ref[0:8, :] = x[tt - 8:tt, :]

    xcb = xc.astype(BF16)
    r = _sigmoid(_mm(xcb, wa_ref[...]) + ba_ref[...])
    ig = _sigmoid(_mm(xcb, wx_ref[...]) + bx_ref[...])
    nl = -lam_ref[...]
    softplus = jnp.maximum(nl, 0.0) + jnp.log(1.0 + jnp.exp(-jnp.abs(nl)))
    log_a = (-LRU_C) * r * softplus
    a = jnp.exp(log_a)
    bb = jnp.sqrt(1.0 - jnp.exp(2.0 * log_a)) * (ig * xc)

    row = lax.broadcasted_iota(jnp.int32, a.shape, 0)
    d = 1
    while d < tt:
        keep = row >= d
        bb = bb + a * jnp.where(keep, pltpu.roll(bb, d, axis=0), 0.0)
        a = a * jnp.where(keep, pltpu.roll(a, d, axis=0), 1.0)
        d *= 2
    h = bb + a * h_ref[...]
    h_ref[...] = h[tt - 1:tt, :]
    o_ref[...] = (h * _silu(gb_ref[...].astype(F32))).astype(BF16)


def _lru(z, conv_w, conv_b, wa_bd, b_a, wx_bd, b_x, lam, b, s, tt):
    n = b * s
    nt = s // tt
    W = BRANCH_WIDTH
    full = lambda shape: pl.BlockSpec(shape, lambda bi, ti: (0, 0))
    zb = lambda off: pl.BlockSpec((tt, W), lambda bi, ti: (bi * nt + ti, off // W))
    return pl.pallas_call(
        functools.partial(_lru_kernel, tt=tt),
        grid=(b, nt),
        in_specs=[zb(OFF_XB), zb(OFF_GB), full((CONV_WIDTH, W)), full((1, W)), full((W, W)),
                  full((1, W)), full((W, W)), full((1, W)), full((1, W))],
        out_specs=pl.BlockSpec((tt, W), lambda bi, ti: (bi * nt + ti, 0)),
        out_shape=jax.ShapeDtypeStruct((n, W), BF16),
        scratch_shapes=[pltpu.VMEM((tt + 8, W), F32), pltpu.VMEM((1, W), F32)],
        compiler_params=pltpu.CompilerParams(dimension_semantics=("parallel", "arbitrary")),
        name="lru",
    )(z, z, conv_w, conv_b, wa_bd, b_a, wx_bd, b_x, lam)


def _gelu(x):
    return 0.5 * x * (1.0 + lax.erf(x * (1.0 / math.sqrt(2.0))))


def _sgu_kernel(u_ref, v_ref, g_ref, lng_ref, wm_ref, bs_ref, o_ref, *, tr):
    u = _gelu(u_ref[...].astype(F32))
    v = _gelu(v_ref[...].astype(F32))
    mu = jnp.mean(v, axis=-1, keepdims=True)
    vc = v - mu
    vn = (vc * lax.rsqrt(jnp.mean(vc * vc, axis=-1, keepdims=True) + NORM_EPS) * lng_ref[...]).astype(BF16)
    lane = lax.broadcasted_iota(jnp.int32, (SG_CHUNK, LANES), 1)
    chunks = []
    for c in range(tr // SG_CHUNK):
        vch = vn[c * SG_CHUNK:(c + 1) * SG_CHUNK, :]
        pairs = []
        for p in range(SG_GROUPS // 2):
            vp = vch[:, p * LANES:(p + 1) * LANES]
            pairs.append(jnp.where(lane < HEAD_DIM, _mm(wm_ref[2 * p], vp), _mm(wm_ref[2 * p + 1], vp)))
        chunks.append(jnp.concatenate(pairs, axis=1) + bs_ref[...])
    mixed = jnp.concatenate(chunks, axis=0)
    o_ref[...] = (u * mixed * _silu(g_ref[...].astype(F32))).astype(BF16)


def _sgu(z, ln_g, wm, bs_full, n, tr):
    W = BRANCH_WIDTH
    zb = lambda off: pl.BlockSpec((tr, W), lambda i: (i, off // W))
    return pl.pallas_call(
        functools.partial(_sgu_kernel, tr=tr),
        grid=(n // tr,),
        in_specs=[zb(OFF_UD), zb(OFF_VD), zb(OFF_GD),
                  pl.BlockSpec((1, W), lambda i: (0, 0)),
                  pl.BlockSpec((SG_GROUPS, SG_CHUNK, SG_CHUNK), lambda i: (0, 0, 0)),
                  pl.BlockSpec((SG_CHUNK, W), lambda i: (0, 0))],
        out_specs=pl.BlockSpec((tr, W), lambda i: (i, 0)),
        out_shape=jax.ShapeDtypeStruct((n, W), BF16),
        compiler_params=pltpu.CompilerParams(dimension_semantics=("parallel",)),
        name="sgu",
    )(z, z, z, ln_g, wm, bs_full)


def _compress_kernel(t16_ref, ptop_ref, pbot_ref, w1t_ref, w1b_ref, w2_ref, bd_ref, gain_ref,
                     kc_ref, vct_ref, *, nc):
    t16 = t16_ref[...].astype(F32)
    a = _mm((t16 + ptop_ref[...]).astype(BF16), w1t_ref[...])
    bm = _mm((t16 + pbot_ref[...]).astype(BF16), w1b_ref[...])
    h = a + pltpu.roll(bm, nc - 1, axis=0)
    out = _mm(_silu(h).astype(BF16), w2_ref[...])
    ms = _mm((out * out).astype(BF16), bd_ref[...])
    kn = out * lax.rsqrt(ms + NORM_EPS) * gain_ref[...]
    out_t = out.T
    for g in range(NSA_KV_HEADS):
        kc_ref[g] = kn[:, g * LANES:(g + 1) * LANES].astype(BF16)
        vct_ref[g] = out_t[(2 + g) * LANES:(2 + g) * LANES + HEAD_DIM, :].astype(BF16)


def _compress(t16, ptop, pbot, w1t, w1b, w2, bd, gain, b, nc):
    kw = 16 * 4 * HEAD_DIM
    full = lambda shape: pl.BlockSpec(shape, lambda bi: tuple(0 for _ in shape))
    return pl.pallas_call(
        functools.partial(_compress_kernel, nc=nc),
        grid=(b,),
        in_specs=[pl.BlockSpec((None, nc, kw), lambda bi: (bi, 0, 0)),
                  full((1, kw)), full((1, kw)), full((kw, 4 * CMP_HIDDEN)), full((kw, 4 * CMP_HIDDEN)),
                  full((4 * CMP_HIDDEN, ZT)), full((ZT, ZT)), full((1, ZT))],
        out_specs=[pl.BlockSpec((None, NSA_KV_HEADS, nc, LANES), lambda bi: (bi, 0, 0, 0)),
                   pl.BlockSpec((None, NSA_KV_HEADS, HEAD_DIM, nc), lambda bi: (bi, 0, 0, 0))],
        out_shape=[jax.ShapeDtypeStruct((b, NSA_KV_HEADS, nc, LANES), BF16),
                   jax.ShapeDtypeStruct((b, NSA_KV_HEADS, HEAD_DIM, nc), BF16)],
        compiler_params=pltpu.CompilerParams(
            dimension_semantics=("parallel",), vmem_limit_bytes=V7X_VMEM_LIMIT),
        name="compress",
    )(t16, ptop, pbot, w1t, w1b, w2, bd, gain)


def _softmax_cols(s, mask):
    sm = jnp.where(mask, s, NEG_INF)
    m = jnp.max(sm, axis=0, keepdims=True)
    p = jnp.exp2(sm - jnp.where(m > 0.5 * NEG_INF, m, 0.0))
    l = jnp.sum(p, axis=0, keepdims=True)
    return p, jnp.where(l > 0.0, 1.0 / l, 0.0)


def _nsa_kernel(q_ref, gc_ref, ks_ref, kw_ref, vs_ref, vw_ref, kc_ref, vct_ref, gt_ref, e_ref, ovt_ref,
                o_ref, kse_ref, sa_ref, sb_ref, sc_ref, selt_ref, m_ref, l_ref, acc_ref, *, s_len, nc):
    i = pl.program_id(2)
    qb, cols = Q_BLOCK, NSA_GROUP * Q_BLOCK
    ns = s_len // SLC_BLOCK
    top_k = min(SLC_TOPK, ns)
    kt_w = min(SEL_TILE, s_len)
    span = min(WINDOW + Q_BLOCK, s_len)
    lane = lax.broadcasted_iota(jnp.int32, (qb, LANES), 1)

    @pl.when(i == 0)
    def _():
        kse_ref[:, 0:LANES] = ks_ref[...]
        kse_ref[:, LANES:2 * LANES] = e_ref[...]

    parts = []
    for r in range(NSA_GROUP):
        blk = q_ref[:, (r // 2) * LANES:(r // 2 + 1) * LANES].astype(F32)
        parts.append(jnp.where(lane // HEAD_DIM == r % 2, blk, 0.0).T.astype(BF16))
    qs_t = jnp.concatenate(parts, axis=1)
    tcol = i * qb + lax.broadcasted_iota(jnp.int32, (1, cols), 1) % qb

    s_c = _mm(kc_ref[...], qs_t)
    start = pl.multiple_of(jnp.maximum(i - WINDOW // qb, 0) * qb, qb)
    s_w = _mm(kw_ref[pl.ds(start, span), :], qs_t)

    cend = lax.broadcasted_iota(jnp.int32, s_c.shape, 0) * CMP_STRIDE + (CMP_BLOCK - 1)
    p_c, inv_c = _softmax_cols(s_c, cend <= tcol)
    p_c = p_c * inv_c
    o_c = _mm(vct_ref[...], p_c.astype(BF16))

    ps = p_c[:, 0:qb] + p_c[:, qb:2 * qb] + p_c[:, 2 * qb:3 * qb] + p_c[:, 3 * qb:4 * qb]
    ps_hi = ps.astype(BF16)
    ps_lo = (ps - ps_hi.astype(F32)).astype(BF16)
    imp = _mm(ovt_ref[...], ps_hi) + _mm(ovt_ref[...], ps_lo)

    jrow = lax.broadcasted_iota(jnp.int32, (LANES, qb), 0)
    jt = (i * qb + lax.broadcasted_iota(jnp.int32, (LANES, qb), 1)) // SLC_BLOCK
    valid = (jrow <= jt) & (jrow < ns)
    forced = (jrow == 0) | (valid & (jrow > jt - SLC_LOCAL))
    score = jnp.where(forced, SLC_FORCE_SCORE, jnp.where(valid, imp, -1.0))
    sc_ref[...] = jnp.where(jrow < ns, score, -2.0)

    def rank_blocks(n_cand):
        for k in range(n_cand // 8):
            mine = sc_ref[8 * k:8 * k + 8, :]
            sub = 8 * k + lax.broadcasted_iota(jnp.int32, (8, qb), 0)
            cnt = jnp.zeros((8, qb), F32)
            for jp in range(n_cand):
                other = sc_ref[jp:jp + 1, :]
                if jp < 8 * k:
                    beats = other >= mine
                elif jp >= 8 * k + 8:
                    beats = other > mine
                else:
                    beats = (other > mine) | ((other == mine) & (sub > jp))
                cnt = cnt + jnp.where(beats, 1.0, 0.0)
            selt_ref[8 * k:8 * k + 8, :] = jnp.where((cnt < top_k) & (mine >= 0.0), 0.0, SEL_NEG)
        if n_cand < LANES:
            selt_ref[n_cand:LANES, :] = jnp.full((LANES - n_cand, qb), SEL_NEG, F32)

    cand_step = 16
    blocks_per_qb = qb // SLC_BLOCK
    classes = list(range(cand_step, ns, cand_step)) + [ns]
    lo_i = 0
    for n_cand in classes:
        hi_i = n_cand // blocks_per_qb if n_cand < ns else s_len // qb
        if n_cand <= top_k:
            @pl.when((i >= lo_i) & (i < hi_i))
            def _():
                selt_ref[...] = jnp.where(valid, 0.0, SEL_NEG)
        else:
            pl.when((i >= lo_i) & (i < hi_i))(functools.partial(rank_blocks, n_cand))
        lo_i = hi_i
    selb = selt_ref[...].astype(BF16)
    lhs_t = jnp.concatenate([qs_t, jnp.concatenate([selb] * NSA_GROUP, axis=1)], axis=0)

    def sel_scores(kt):
        k0 = pl.multiple_of(kt * kt_w, kt_w)
        return _mm(kse_ref[pl.ds(k0, kt_w), :], lhs_t)

    def sel_update(kt, s):
        k0 = pl.multiple_of(kt * kt_w, kt_w)
        m_prev = m_ref[...]
        m_new = jnp.maximum(m_prev, jnp.max(s, axis=0, keepdims=True))
        alpha = jnp.exp2(m_prev - m_new)
        pr = jnp.exp2(s - m_new)
        l_ref[...] = alpha * l_ref[...] + jnp.sum(pr, axis=0, keepdims=True)
        acc_ref[...] = alpha * acc_ref[...] + _mm(vs_ref[0:HEAD_DIM, pl.ds(k0, kt_w)], pr.astype(BF16))
        m_ref[...] = m_new

    m_ref[...] = jnp.full_like(m_ref, NEG_INF)
    l_ref[...] = jnp.zeros_like(l_ref)
    acc_ref[...] = jnp.zeros_like(acc_ref)
    n_full = (i * qb) // kt_w
    sa_ref[...] = sel_scores(0)

    pos = start + lax.broadcasted_iota(jnp.int32, s_w.shape, 0)
    p_w, inv_w = _softmax_cols(s_w, (pos <= tcol) & (pos > tcol - WINDOW))
    o_w = _mm(vw_ref[0:HEAD_DIM, pl.ds(start, span)], p_w.astype(BF16)) * inv_w

    def body(k2, carry):
        sb_ref[...] = sel_scores(2 * k2 + 1)
        sel_update(2 * k2, sa_ref[...])
        sa_ref[...] = sel_scores(2 * k2 + 2)
        sel_update(2 * k2 + 1, sb_ref[...])
        return carry

    lax.fori_loop(0, n_full // 2, body, 0)

    def last_update(s_last):
        pos_k = n_full * kt_w + lax.broadcasted_iota(jnp.int32, s_last.shape, 0)
        sel_update(n_full, jnp.where(pos_k <= tcol, s_last, NEG_INF))

    @pl.when(n_full % 2 == 1)
    def _():
        sb_ref[...] = sel_scores(n_full)
        sel_update(n_full - 1, sa_ref[...])
        last_update(sb_ref[...])

    @pl.when(n_full % 2 == 0)
    def _():
        last_update(sa_ref[...])

    o_s = acc_ref[...] * (1.0 / l_ref[...])

    gts = _sigmoid(gt_ref[...]).T
    heads = []
    for r in range(NSA_GROUP):
        cs = slice(r * qb, (r + 1) * qb)
        heads.append(gts[r:r + 1, :] * o_c[:, cs] + gts[4 + r:5 + r, :] * o_s[:, cs]
                     + gts[8 + r:9 + r, :] * o_w[:, cs])
    y = jnp.concatenate(heads, axis=0).T
    o_ref[...] = (y * _silu(gc_ref[...].astype(F32))).astype(BF16)


def _nsa(z, vt, small, kc, vct, e128, ovt, b, s):
    n = b * s
    nqb = s // Q_BLOCK
    nc = s // CMP_STRIDE
    gw = NSA_GROUP * HEAD_DIM
    cols = NSA_GROUP * Q_BLOCK
    qspec = lambda off: pl.BlockSpec((Q_BLOCK, gw), lambda bi, g, i: (bi * nqb + i, off // gw + g))
    seq = lambda off: pl.BlockSpec((s, LANES), lambda bi, g, i: (bi, off // LANES + g))
    vseq = lambda row: pl.BlockSpec((LANES, s), lambda bi, g, i: (row // LANES + g, bi))
    return pl.pallas_call(
        functools.partial(_nsa_kernel, s_len=s, nc=nc),
        grid=(b, NSA_KV_HEADS, nqb),
        in_specs=[qspec(OFF_QC), qspec(OFF_GC), seq(OFF_KSD), seq(OFF_KWD),
                  vseq(ZT), vseq(ZT + 256),
                  pl.BlockSpec((None, None, nc, LANES), lambda bi, g, i: (bi, g, 0, 0)),
                  pl.BlockSpec((None, None, HEAD_DIM, nc), lambda bi, g, i: (bi, g, 0, 0)),
                  pl.BlockSpec((Q_BLOCK, LANES), lambda bi, g, i: (bi * nqb + i, 1 + g)),
                  pl.BlockSpec((s, LANES), lambda bi, g, i: (0, 0)),
                  pl.BlockSpec((LANES, nc), lambda bi, g, i: (0, 0))],
        out_specs=pl.BlockSpec((Q_BLOCK, gw), lambda bi, g, i: (bi * nqb + i, g)),
        out_shape=jax.ShapeDtypeStruct((n, BRANCH_WIDTH), BF16),
        scratch_shapes=[pltpu.VMEM((s, 2 * LANES), BF16),
                        pltpu.VMEM((min(SEL_TILE, s), cols), F32),
                        pltpu.VMEM((min(SEL_TILE, s), cols), F32),
                        pltpu.VMEM((LANES, Q_BLOCK), F32), pltpu.VMEM((LANES, Q_BLOCK), F32),
                        pltpu.VMEM((1, cols), F32),
                        pltpu.VMEM((1, cols), F32), pltpu.VMEM((HEAD_DIM, cols), F32)],
        compiler_params=pltpu.CompilerParams(
            dimension_semantics=("parallel", "parallel", "arbitrary"),
            vmem_limit_bytes=V7X_VMEM_LIMIT),
        name="nsa",
    )(z, z, z, z, vt, vt, kc, vct, small, e128, ovt)


def _merge_kernel(ya_ref, yb_ref, yc_ref, yd_ref, mg_ref, x_ref, wb_ref, wo_ref, o_ref):
    merged = None
    for nb, y_ref in enumerate((ya_ref, yb_ref, yc_ref, yd_ref)):
        gate = _sigmoid(mg_ref[:, nb * D_MODEL:(nb + 1) * D_MODEL].astype(F32))
        term = gate * _mm(y_ref[...], wb_ref[nb])
        merged = term if merged is None else merged + term
    o_ref[...] = x_ref[...] + _mm(merged.astype(BF16), wo_ref[...])


def _merge(ys, z, x2, wb, wo, tm):
    n = x2.shape[0]
    W = BRANCH_WIDTH
    yspec = pl.BlockSpec((tm, W), lambda i: (i, 0))
    return pl.pallas_call(
        _merge_kernel,
        grid=(n // tm,),
        in_specs=[yspec, yspec, yspec, yspec,
                  pl.BlockSpec((tm, N_BRANCHES * D_MODEL), lambda i: (i, 0)),
                  pl.BlockSpec((tm, D_MODEL), lambda i: (i, 0)),
                  pl.BlockSpec((N_BRANCHES, W, D_MODEL), lambda i: (0, 0, 0)),
                  pl.BlockSpec((D_MODEL, D_MODEL), lambda i: (0, 0))],
        out_specs=pl.BlockSpec((tm, D_MODEL), lambda i: (i, 0)),
        out_shape=jax.ShapeDtypeStruct((n, D_MODEL), F32),
        compiler_params=pltpu.CompilerParams(
            dimension_semantics=("parallel",), vmem_limit_bytes=V7X_VMEM_LIMIT),
        name="merge",
    )(*ys, z, x2, wb, wo)


def _block_diag(blocks):
    nb, a, b_ = blocks.shape
    eye = jnp.eye(nb, dtype=blocks.dtype)
    return jnp.einsum('nab,nm->namb', blocks, eye).reshape(nb * a, nb * b_)


def _mean_matrix(width):
    idx = np.arange(width) // HEAD_DIM
    return jnp.asarray((idx[:, None] == idx[None, :]).astype(np.float32) / HEAD_DIM, BF16)


def _layout_w_in(w_in, qn_a, kn_a, qn_c, kn_c):
    sizes = _split_sizes()
    offs = np.concatenate([[0], np.cumsum(sizes)])
    seg = lambda k: w_in[:, int(offs[k]):int(offs[k + 1])]
    (qa, ka, va, fa, ga, xb, gb, qc, kcc, vcc, ksc, vsc, kwc, vwc, gate_c, gc, ud, vd, gd, mg) = [
        seg(k) for k in range(len(sizes))]
    dup = lambda w: jnp.concatenate([w[:, :HEAD_DIM], w[:, :HEAD_DIM], w[:, HEAD_DIM:], w[:, HEAD_DIM:]], axis=1)
    pad = jnp.zeros((D_MODEL, 256), w_in.dtype)
    w_z = jnp.concatenate([mg, qa, ka, qc, dup(ksc), dup(kwc), va, dup(vsc), dup(vwc),
                           ga, xb, gb, gc, ud, vd, gd, kcc, vcc, pad], axis=1).astype(BF16)
    scale = LOG2E / math.sqrt(HEAD_DIM)
    gain = jnp.ones((ZW,), F32)
    gain = gain.at[OFF_QA:OFF_QA + ZT].set(jnp.tile(qn_a * scale, FOX_HEADS))
    gain = gain.at[OFF_KA:OFF_KA + ZT].set(jnp.tile(kn_a, FOX_HEADS))
    gain = gain.at[OFF_QC:OFF_QC + ZT].set(jnp.tile(qn_c * scale, NSA_HEADS))
    gain = gain.at[OFF_KSD:OFF_KSD + ZT].set(jnp.tile(kn_c, ZT // HEAD_DIM))
    gate3 = gate_c.reshape(D_MODEL, 3, NSA_KV_HEADS, NSA_GROUP)
    zpad = lambda w: jnp.pad(w, ((0, 0), (0, LANES - w.shape[1])))
    w_small = jnp.concatenate(
        [zpad(fa)] + [zpad(gate3[:, :, g, :].reshape(D_MODEL, 3 * NSA_GROUP)) for g in range(NSA_KV_HEADS)],
        axis=1).astype(BF16)
    return w_z, gain.reshape(1, ZW), w_small


def _layout_compress(cmp_pos, wk1, wk2, wv1, wv2, kn_c):
    half = CMP_BLOCK // 2
    eye4 = jnp.eye(4, dtype=F32)

    def w1_half(lo):
        k = wk1.reshape(CMP_BLOCK, HEAD_DIM, CMP_HIDDEN)[lo:lo + half]
        v = wv1.reshape(CMP_BLOCK, HEAD_DIM, CMP_HIDDEN)[lo:lo + half]
        stack = jnp.stack([k, k, v, v])
        return jnp.einsum('spdn,st->psdtn', stack, eye4).reshape(half * 4 * HEAD_DIM, 4 * CMP_HIDDEN).astype(BF16)

    def pos_half(lo):
        p = cmp_pos[lo:lo + half]
        return jnp.broadcast_to(p[:, None, :], (half, 4, HEAD_DIM)).reshape(1, half * 4 * HEAD_DIM)

    w2 = jnp.zeros((4, CMP_HIDDEN, 4, 2, HEAD_DIM), F32)
    for slot, w in enumerate((wk2, wk2, wv2, wv2)):
        w2 = w2.at[slot, :, slot, :, :].set(jnp.broadcast_to(w[:, None, :], (CMP_HIDDEN, 2, HEAD_DIM)))
    w2 = w2.reshape(4 * CMP_HIDDEN, ZT).astype(BF16)
    gain = jnp.tile(kn_c, ZT // HEAD_DIM).reshape(1, ZT)
    return pos_half(0), pos_half(half), w1_half(0), w1_half(half), w2, gain


@functools.lru_cache(maxsize=None)
def _nsa_consts(s):
    nc = s // CMP_STRIDE
    ns = s // SLC_BLOCK
    e = np.zeros((s, LANES), np.float32)
    e[np.arange(s), np.arange(s) // SLC_BLOCK] = 1.0
    c0 = np.arange(nc) * CMP_STRIDE
    s0 = np.arange(ns) * SLC_BLOCK
    ovl = np.minimum(c0[:, None] + CMP_BLOCK, s0[None, :] + SLC_BLOCK) - np.maximum(c0[:, None], s0[None, :])
    ovt = np.zeros((LANES, nc), np.float32)
    ovt[:ns, :] = (np.clip(ovl, 0, None) / CMP_BLOCK).T
    ovt[:, nc - 1] = 0.0
    return e, ovt


def _tile(s, pref):
    t = pref
    while s % t:
        t //= 2
    return t


def _branches(x2, b, s, norm_g, w_in, b_forget, qn_a, kn_a, conv_w, conv_b, w_rg_a, b_rg_a, w_rg_x,
              b_rg_x, lru_lambda, qn_c, kn_c, cmp_pos, wk1, wk2, wv1, wv2, ln_v_g, w_spatial,
              b_spatial):
    n = b * s
    W = BRANCH_WIDTH
    w_z, gain_row, w_small = _layout_w_in(w_in, qn_a, kn_a, qn_c, kn_c)
    bd = _mean_matrix(ZT)
    z, small = _inproj(x2, norm_g.reshape(1, D_MODEL), w_z, gain_row, bd, w_small, _tile(n, 2048))
    vt = _vtrans(z, _tile(s, 2048))

    bf_row = jnp.pad(b_forget, (0, LANES - FOX_HEADS)).reshape(1, LANES)
    caq, cak = _cprep(small, bf_row, b, s, _tile(s, 1024))
    ys_a = _fox(z, vt, caq, cak, b, s, _tile(s, 512))

    ys_b = _lru(z, conv_w, conv_b.reshape(1, W), _block_diag(w_rg_a).astype(BF16), b_rg_a.reshape(1, W),
                _block_diag(w_rg_x).astype(BF16), b_rg_x.reshape(1, W), lru_lambda.reshape(1, W),
                b, s, _tile(s, 256))

    nc = s // CMP_STRIDE
    t16 = z[:, OFF_CMP:OFF_CMP + 4 * HEAD_DIM].reshape(b, nc, CMP_STRIDE * 4 * HEAD_DIM)
    ptop, pbot, w1t, w1b, w2c, cgain = _layout_compress(cmp_pos, wk1, wk2, wv1, wv2, kn_c)
    kc, vct = _compress(t16, ptop, pbot, w1t, w1b, w2c, bd, cgain, b, nc)
    e_np, ovt_np = _nsa_consts(s)
    ys_c = _nsa(z, vt, small, kc, vct, jnp.asarray(e_np, BF16), jnp.asarray(ovt_np, BF16), b, s)

    causal = jnp.tril(jnp.ones((SG_CHUNK, SG_CHUNK), F32))
    wm = (w_spatial * causal).astype(BF16)
    bs_full = jnp.broadcast_to(b_spatial.T[:, :, None], (SG_CHUNK, SG_GROUPS, HEAD_DIM)).reshape(SG_CHUNK, W)
    ys_d = _sgu(z, ln_v_g.reshape(1, W), wm, bs_full, n, _tile(s, 1024))
    return (ys_a, ys_b, ys_c, ys_d), z


def _layer(x2, b, s, *params):
    ys, z = _branches(x2, b, s, *params[:-2])
    w_branch, w_out = params[-2:]
    return _merge(ys, z, x2, w_branch.astype(BF16), w_out.astype(BF16), _tile(b * s, 512))


def kernel(x, norm_g, w_in, b_forget, qn_a, kn_a, conv_w, conv_b, w_rg_a, b_rg_a, w_rg_x, b_rg_x,
           lru_lambda, qn_c, kn_c, cmp_pos, cmp_k_w1, cmp_k_w2, cmp_v_w1, cmp_v_w2, ln_v_g,
           w_spatial, b_spatial, w_branch, w_out):
    b, s, d = x.shape
    x2 = x.reshape(b * s, d)
    params = (norm_g, w_in, b_forget, qn_a, kn_a, conv_w, conv_b, w_rg_a, b_rg_a, w_rg_x, b_rg_x,
              lru_lambda, qn_c, kn_c, cmp_pos, cmp_k_w1, cmp_k_w2, cmp_v_w1, cmp_v_w2, ln_v_g,
              w_spatial, b_spatial, w_branch, w_out)
    for layer in range(norm_g.shape[0]):
        x2 = _layer(x2, b, s, *[p[layer] for p in params])
    return x2.reshape(b, s, d)
```
